```python
import math
import jax
import jax.numpy as jnp
from jax import lax
import numpy as np


D_MODEL = 1024
BATCH = 16
SEQ = 4096
DEPTH = 2

ATT_HEADS = 4
ATT_DH = 64
ATT_WIDTH = ATT_HEADS * 2 * ATT_DH
Q_BLOCK = 128
CONV_CH = 512
CONV_K = 31
SSM_D_INNER = 512
SSM_HEADDIM = 64
SSM_HEADS = SSM_D_INNER // SSM_HEADDIM
SSM_GROUPS = 2
SSM_HPG = SSM_HEADS // SSM_GROUPS
SSM_STATE = 128
SSM_CONV_K = 4
SSM_CONV_DIM = SSM_D_INNER + 2 * SSM_GROUPS * SSM_STATE
SSM_CHUNK = 128
N_BRANCHES = 3
IN_SPLITS = (ATT_WIDTH, ATT_WIDTH, ATT_WIDTH, 2 * CONV_CH, SSM_D_INNER, SSM_CONV_DIM, SSM_HEADS, N_BRANCHES * D_MODEL)
IN_COLS = sum(IN_SPLITS)
FF_DENSE = 2816
N_EXPERTS = 8
TOP_K = 2
FF_EXPERT = 3584
MOE_BLOCK = 256
NORM_EPS = 1e-6

kernel_name = 'hybrid_diffattn_conformer_ssd_moe_block'


def rmsnorm(x, w, eps=NORM_EPS):
    xf = x.astype(jnp.float32)
    y = xf * lax.rsqrt(jnp.mean(xf * xf, axis=-1, keepdims=True) + eps)
    return (y * w.astype(jnp.float32)).astype(x.dtype)


def layernorm(x, w, b, eps=1e-5):
    xf = x.astype(jnp.float32)
    mu = jnp.mean(xf, axis=-1, keepdims=True)
    var = jnp.mean(jnp.square(xf - mu), axis=-1, keepdims=True)
    y = (xf - mu) * lax.rsqrt(var + eps) * w.astype(jnp.float32) + b.astype(jnp.float32)
    return y.astype(x.dtype)


def causal_depthwise_conv(x, w, b):
    k = w.shape[0]
    y = lax.conv_general_dilated(x, w[:, None, :].astype(x.dtype), window_strides=(1,), padding=((k - 1, 0),), dimension_numbers=('NWC', 'WIO', 'NWC'), feature_group_count=x.shape[-1])
    return y + b.astype(x.dtype)


def diff_attention(q, k, v, lam, subln_w, lam_init):
    b, s = q.shape[0], q.shape[1]
    nqb = s // Q_BLOCK
    qb = (q * (ATT_DH ** -0.5)).reshape(b, nqb, Q_BLOCK, ATT_HEADS, 2, ATT_DH).transpose(1, 0, 2, 3, 4, 5)
    kpos = jnp.arange(s)

    def block(args):
        qblk, start = args
        sc = jnp.einsum('bqhmd,bkhmd->bhmqk', qblk, k).astype(jnp.float32)
        qpos = start + jnp.arange(Q_BLOCK)
        mask = kpos[None, :] <= qpos[:, None]
        p = jax.nn.softmax(jnp.where(mask, sc, -jnp.inf), axis=-1)
        a = p[:, :, 0] - lam * p[:, :, 1]
        return jnp.einsum('bhqk,bkhe->bqhe', a.astype(v.dtype), v)

    starts = jnp.arange(nqb, dtype=jnp.int32) * Q_BLOCK
    o = lax.map(block, (qb, starts))
    o = o.transpose(1, 0, 2, 3, 4).reshape(b, s, ATT_HEADS, 2 * ATT_DH)
    o = rmsnorm(o, subln_w, eps=1e-5) * (1.0 - lam_init)
    return o.reshape(b, s, ATT_WIDTH)


def conformer_conv(u, dw_w, dw_b, ln_w, ln_b):
    a, gate = jnp.split(u, 2, axis=-1)
    y = a * jax.nn.sigmoid(gate)
    y = causal_depthwise_conv(y, dw_w, dw_b)
    return jax.nn.silu(layernorm(y, ln_w, ln_b))


def ssd_chunked(xs, dt, a, bm, cm):
    b, s, g, r, p = xs.shape
    n = bm.shape[-1]
    nc = s // SSM_CHUNK
    L = SSM_CHUNK
    xc = xs.reshape(b, nc, L, g, r, p)
    dtc = dt.reshape(b, nc, L, g, r)
    bc = bm.reshape(b, nc, L, g, n)
    cc = cm.reshape(b, nc, L, g, n)
    a_cum = jnp.cumsum(dtc * a, axis=2)
    xdt = xc * dtc[..., None]
    causal = jnp.tril(jnp.ones((L, L), dtype=bool))
    seg = a_cum[:, :, :, None] - a_cum[:, :, None, :]
    decay = jnp.exp(jnp.where(causal[None, None, :, :, None, None], seg, -jnp.inf))
    cb = jnp.einsum('bclgn,bcsgn->bclsg', cc, bc)
    y_diag = jnp.einsum('bclsgr,bcsgrp->bclgrp', cb[..., None] * decay, xdt)
    decay_end = jnp.exp(a_cum[:, :, -1:] - a_cum)
    states = jnp.einsum('bclgn,bclgrp->bcgrpn', bc, xdt * decay_end[..., None])
    chunk_decay = jnp.exp(a_cum[:, :, -1])

    def step(h, inp):
        st, dec = inp
        return h * dec[..., None, None] + st, h

    h0 = jnp.zeros((b, g, r, p, n), jnp.float32)
    _, h_in = lax.scan(step, h0, (jnp.moveaxis(states, 1, 0), jnp.moveaxis(chunk_decay, 1, 0)))
    h_in = jnp.moveaxis(h_in, 0, 1)
    y_off = jnp.einsum('bclgn,bcgrpn->bclgrp', cc, h_in) * jnp.exp(a_cum)[..., None]
    return (y_diag + y_off).reshape(b, s, g, r, p)


def mamba2_ssd(z, xbc, dt, conv_w, conv_b, dt_bias, a_log, d_skip, norm_w):
    f32 = jnp.float32
    b, s = z.shape[0], z.shape[1]
    xbc = jax.nn.silu(causal_depthwise_conv(xbc, conv_w, conv_b))
    xs, bm, cm = jnp.split(xbc, [SSM_D_INNER, SSM_D_INNER + SSM_GROUPS * SSM_STATE], axis=-1)
    xs = xs.astype(f32).reshape(b, s, SSM_GROUPS, SSM_HPG, SSM_HEADDIM)
    bm = bm.astype(f32).reshape(b, s, SSM_GROUPS, SSM_STATE)
    cm = cm.astype(f32).reshape(b, s, SSM_GROUPS, SSM_STATE)
    dt = jax.nn.softplus(dt.astype(f32) + dt_bias.astype(f32)).reshape(b, s, SSM_GROUPS, SSM_HPG)
    a = -jnp.exp(a_log.astype(f32)).reshape(SSM_GROUPS, SSM_HPG)
    y = ssd_chunked(xs, dt, a, bm, cm)
    y = y + xs * d_skip.astype(f32).reshape(SSM_GROUPS, SSM_HPG, 1)
    y = y.reshape(b, s, SSM_D_INNER) * jax.nn.silu(z.astype(f32))
    y = y.reshape(b, s, SSM_GROUPS, SSM_D_INNER // SSM_GROUPS)
    y = y * lax.rsqrt(jnp.mean(y * y, axis=-1, keepdims=True) + 1e-5)
    y = y.reshape(b, s, SSM_D_INNER) * norm_w.astype(f32)
    return y.astype(z.dtype)


def swiglu(h, w_gate, w_up, w_down):
    return jnp.dot(jax.nn.silu(jnp.dot(h, w_gate)) * jnp.dot(h, w_up), w_down)


def moe_swiglu(h, w_router, b_router, w_gate, w_up, w_down):
    t, d = h.shape
    logits = jnp.dot(h.astype(jnp.float32), w_router.astype(jnp.float32)) + b_router.astype(jnp.float32)
    top_logit, top_idx = lax.top_k(logits, TOP_K)
    top_w = jax.nn.softmax(top_logit, axis=-1)
    tk = t * TOP_K
    flat_e = top_idx.reshape(tk).astype(jnp.int32)
    flat_tok = jnp.repeat(jnp.arange(t, dtype=jnp.int32), TOP_K)
    flat_w = top_w.reshape(tk)
    order = jnp.argsort(flat_e)
    se, stok, sw = flat_e[order], flat_tok[order], flat_w[order]
    counts = jnp.bincount(flat_e, length=N_EXPERTS)
    padded = (counts + MOE_BLOCK - 1) // MOE_BLOCK * MOE_BLOCK
    start_sorted = jnp.cumsum(counts) - counts
    end_padded = jnp.cumsum(padded)
    start_padded = end_padded - padded
    dest = start_padded[se] + (jnp.arange(tk, dtype=jnp.int32) - start_sorted[se])
    n_blocks = -(-tk // MOE_BLOCK) + N_EXPERTS
    n_rows = n_blocks * MOE_BLOCK
    row_tok = jnp.zeros((n_rows,), jnp.int32).at[dest].set(stok)
    row_w = jnp.zeros((n_rows,), jnp.float32).at[dest].set(sw)
    block_start = jnp.arange(n_blocks, dtype=jnp.int32) * MOE_BLOCK
    block_exp = jnp.minimum(jnp.sum(block_start[:, None] >= end_padded[None, :], axis=1), N_EXPERTS - 1)

    def expert_block(args):
        tok, wt, e = args
        xb = h[tok]
        yb = jnp.dot(jax.nn.silu(jnp.dot(xb, w_gate[e])) * jnp.dot(xb, w_up[e]), w_down[e])
        return (yb.astype(jnp.float32) * wt[:, None]).astype(h.dtype)

    rows = lax.map(expert_block, (row_tok.reshape(n_blocks, MOE_BLOCK), row_w.reshape(n_blocks, MOE_BLOCK), block_exp))
    return jnp.zeros((t, d), h.dtype).at[row_tok].add(rows.reshape(n_rows, d))


def setup_inputs(seed: int = 0) -> dict:
    key = jax.random.key(seed)
    ks = iter(jax.random.split(key, 48))
    f32 = jnp.float32

    def nrm(shape, scale):
        return jax.random.normal(next(ks), shape, f32) * scale

    n_dense = (DEPTH + 1) // 2
    n_moe = DEPTH // 2
    x = nrm((BATCH, SEQ, D_MODEL), 1.0)
    ln_mix_w = 1.0 + nrm((DEPTH, D_MODEL), 0.02)
    w_in = nrm((DEPTH, D_MODEL, IN_COLS), D_MODEL ** -0.5)
    attn_lq1 = nrm((DEPTH, ATT_DH), 0.1)
    attn_lk1 = nrm((DEPTH, ATT_DH), 0.1)
    attn_lq2 = nrm((DEPTH, ATT_DH), 0.1)
    attn_lk2 = nrm((DEPTH, ATT_DH), 0.1)
    attn_subln_w = 1.0 + nrm((DEPTH, 2 * ATT_DH), 0.02)
    attn_w_o = nrm((DEPTH, ATT_WIDTH, D_MODEL), ATT_WIDTH ** -0.5)
    conv_dw_w = nrm((DEPTH, CONV_K, CONV_CH), CONV_K ** -0.5)
    conv_dw_b = nrm((DEPTH, CONV_CH), 0.02)
    conv_ln_w = 1.0 + nrm((DEPTH, CONV_CH), 0.02)
    conv_ln_b = nrm((DEPTH, CONV_CH), 0.02)
    conv_w_pw = nrm((DEPTH, CONV_CH, D_MODEL), CONV_CH ** -0.5)
    ssm_conv_w = nrm((DEPTH, SSM_CONV_K, SSM_CONV_DIM), SSM_CONV_K ** -0.5)
    ssm_conv_b = nrm((DEPTH, SSM_CONV_DIM), 0.02)
    u_dt = jax.random.uniform(next(ks), (DEPTH, SSM_HEADS), f32)
    dt0 = jnp.exp(u_dt * (math.log(0.1) - math.log(0.001)) + math.log(0.001))
    ssm_dt_bias = dt0 + jnp.log(-jnp.expm1(-dt0))
    ssm_A_log = jnp.log(jax.random.uniform(next(ks), (DEPTH, SSM_HEADS), f32, 1.0, 16.0))
    ssm_D = 1.0 + nrm((DEPTH, SSM_HEADS), 0.02)
    ssm_norm_w = 1.0 + nrm((DEPTH, SSM_D_INNER), 0.02)
    ssm_w_out = nrm((DEPTH, SSM_D_INNER, D_MODEL), SSM_D_INNER ** -0.5)
    w_mix_out = nrm((DEPTH, D_MODEL, D_MODEL), D_MODEL ** -0.5)
    ln_ffn_w = 1.0 + nrm((DEPTH, D_MODEL), 0.02)
    ffn_w_gate = nrm((n_dense, D_MODEL, FF_DENSE), D_MODEL ** -0.5)
    ffn_w_up = nrm((n_dense, D_MODEL, FF_DENSE), D_MODEL ** -0.5)
    ffn_w_down = nrm((n_dense, FF_DENSE, D_MODEL), FF_DENSE ** -0.5)
    moe_w_router = nrm((n_moe, D_MODEL, N_EXPERTS), D_MODEL ** -0.5)
    moe_b_router = nrm((n_moe, N_EXPERTS), 0.01)
    moe_w_gate = nrm((n_moe, N_EXPERTS, D_MODEL, FF_EXPERT), D_MODEL ** -0.5)
    moe_w_up = nrm((n_moe, N_EXPERTS, D_MODEL, FF_EXPERT), D_MODEL ** -0.5)
    moe_w_down = nrm((n_moe, N_EXPERTS, FF_EXPERT, D_MODEL), FF_EXPERT ** -0.5)
    ln_final_w = 1.0 + nrm((D_MODEL,), 0.02)
    return {'x': x, 'ln_mix_w': ln_mix_w, 'w_in': w_in, 'attn_lq1': attn_lq1, 'attn_lk1': attn_lk1, 'attn_lq2': attn_lq2, 'attn_lk2': attn_lk2, 'attn_subln_w': attn_subln_w, 'attn_w_o': attn_w_o, 'conv_dw_w': conv_dw_w, 'conv_dw_b': conv_dw_b, 'conv_ln_w': conv_ln_w, 'conv_ln_b': conv_ln_b, 'conv_w_pw': conv_w_pw, 'ssm_conv_w': ssm_conv_w, 'ssm_conv_b': ssm_conv_b, 'ssm_dt_bias': ssm_dt_bias, 'ssm_A_log': ssm_A_log, 'ssm_D': ssm_D, 'ssm_norm_w': ssm_norm_w, 'ssm_w_out': ssm_w_out, 'w_mix_out': w_mix_out, 'ln_ffn_w': ln_ffn_w, 'ffn_w_gate': ffn_w_gate, 'ffn_w_up': ffn_w_up, 'ffn_w_down': ffn_w_down, 'moe_w_router': moe_w_router, 'moe_b_router': moe_b_router, 'moe_w_gate': moe_w_gate, 'moe_w_up': moe_w_up, 'moe_w_down': moe_w_down, 'ln_final_w': ln_final_w}


def reference(x, ln_mix_w, w_in, attn_lq1, attn_lk1, attn_lq2, attn_lk2, attn_subln_w, attn_w_o, conv_dw_w, conv_dw_b, conv_ln_w, conv_ln_b, conv_w_pw, ssm_conv_w, ssm_conv_b, ssm_dt_bias, ssm_A_log, ssm_D, ssm_norm_w, ssm_w_out, w_mix_out, ln_ffn_w, ffn_w_gate, ffn_w_up, ffn_w_down, moe_w_router, moe_b_router, moe_w_gate, moe_w_up, moe_w_down, ln_final_w):
    f32 = jnp.float32
    b, s, d = x.shape
    split_at = [int(c) for c in np.cumsum(IN_SPLITS)[:-1]]
    for l in range(DEPTH):
        h = rmsnorm(x, ln_mix_w[l])
        proj = jnp.einsum('bsd,dc->bsc', h, w_in[l])
        q, k, v, u, z, xbc, dt, g = jnp.split(proj, split_at, axis=-1)
        lam_init = 0.8 - 0.6 * math.exp(-0.3 * l)
        lam = (jnp.exp(jnp.sum(attn_lq1[l].astype(f32) * attn_lk1[l].astype(f32)))
               - jnp.exp(jnp.sum(attn_lq2[l].astype(f32) * attn_lk2[l].astype(f32))) + lam_init)
        y_att = diff_attention(q.reshape(b, s, ATT_HEADS, 2, ATT_DH), k.reshape(b, s, ATT_HEADS, 2, ATT_DH), v.reshape(b, s, ATT_HEADS, 2 * ATT_DH), lam, attn_subln_w[l], lam_init)
        y_att = jnp.dot(y_att, attn_w_o[l])
        y_conv = jnp.dot(conformer_conv(u, conv_dw_w[l], conv_dw_b[l], conv_ln_w[l], conv_ln_b[l]), conv_w_pw[l])
        y_ssm = jnp.dot(mamba2_ssd(z, xbc, dt, ssm_conv_w[l], ssm_conv_b[l], ssm_dt_bias[l], ssm_A_log[l], ssm_D[l], ssm_norm_w[l]), ssm_w_out[l])
        g_att, g_conv, g_ssm = jnp.split(jax.nn.sigmoid(g), N_BRANCHES, axis=-1)
        merged = g_att * y_att + g_conv * y_conv + g_ssm * y_ssm
        x = x + jnp.dot(merged, w_mix_out[l])
        h = rmsnorm(x, ln_ffn_w[l])
        i = l // 2
        if l % 2 == 0:
            x = x + swiglu(h, ffn_w_gate[i], ffn_w_up[i], ffn_w_down[i])
        else:
            x = x + moe_swiglu(h.reshape(b * s, d), moe_w_router[i], moe_b_router[i], moe_w_gate[i], moe_w_up[i], moe_w_down[i]).reshape(b, s, d)
    return rmsnorm(x, ln_final_w)
```

```python
import functools
import math

import jax
import jax.numpy as jnp
from jax import lax
from jax.experimental import pallas as pl
from jax.experimental.pallas import tpu as pltpu

F32 = jnp.float32
BF16 = jnp.bfloat16
HIGHEST = lax.Precision.HIGHEST

D_MODEL = 1024
ATT_HEADS = 4
ATT_DH = 64
ATT_WIDTH = ATT_HEADS * 2 * ATT_DH
CONV_CH = 512
CONV_K = 31
SSM_D_INNER = 512
SSM_HEADDIM = 64
SSM_HEADS = SSM_D_INNER // SSM_HEADDIM
SSM_GROUPS = 2
SSM_STATE = 128
SSM_CONV_K = 4
SSM_CONV_DIM = SSM_D_INNER + 2 * SSM_GROUPS * SSM_STATE
SSM_CHUNK = 128
N_EXPERTS = 8
TOP_K = 2
NORM_EPS = 1e-6

LANES = 128
SUBLANES = 8
VMEM_LIMIT_BYTES = 48 * 1024 * 1024

COL_Q = 0
COL_K = COL_Q + ATT_WIDTH
COL_V = COL_K + ATT_WIDTH
COL_U = COL_V + ATT_WIDTH
COL_Z = COL_U + 2 * CONV_CH
COL_XBC = COL_Z + SSM_D_INNER
COL_DT = COL_XBC + SSM_CONV_DIM
COL_G = COL_XBC + SSM_CONV_DIM
PROJ_COLS = COL_G + 3 * D_MODEL


def _params(*semantics):
    return pltpu.CompilerParams(dimension_semantics=semantics, vmem_limit_bytes=VMEM_LIMIT_BYTES)


def _sigmoid(x):
    return 1.0 / (1.0 + jnp.exp(-x))


def _rms(x, eps):
    return x * lax.rsqrt(jnp.mean(x * x, axis=-1, keepdims=True) + eps)


def _norm_matmul_kernel(x_ref, g_ref, w_ref, o_ref, h_ref):
    @pl.when(pl.program_id(1) == 0)
    def _():
        h_ref[...] = (_rms(x_ref[...], NORM_EPS) * g_ref[...]).astype(h_ref.dtype)

    o_ref[...] = jnp.dot(h_ref[...], w_ref[...], preferred_element_type=F32).astype(o_ref.dtype)


def _norm_matmul(x, g, w, out_dtype, tm, tn):
    t, d = x.shape
    n = w.shape[1]
    return pl.pallas_call(
        _norm_matmul_kernel,
        grid=(t // tm, n // tn),
        in_specs=[pl.BlockSpec((tm, d), lambda i, j: (i, 0)),
                  pl.BlockSpec((1, d), lambda i, j: (0, 0)),
                  pl.BlockSpec((d, tn), lambda i, j: (0, j))],
        out_specs=pl.BlockSpec((tm, tn), lambda i, j: (i, j)),
        out_shape=jax.ShapeDtypeStruct((t, n), out_dtype),
        scratch_shapes=[pltpu.VMEM((tm, d), BF16)],
        compiler_params=_params("parallel", "arbitrary"),
        name="norm_matmul",
    )(x, g.reshape(1, d), w)


def _attn_kernel(lam_ref, q_ref, k_ref, v_ref, sw_ref, o_ref, m_ref, l_ref, acc_ref, *, tq, out_scale):
    qi = pl.program_id(2)
    q = q_ref[...] * (ATT_DH ** -0.5)
    lane = lax.broadcasted_iota(jnp.int32, q.shape, 1)
    zero = jnp.zeros_like(q)
    q2x = jnp.concatenate([jnp.where(lane < ATT_DH, q, zero), jnp.where(lane >= ATT_DH, q, zero)], axis=0)

    m_ref[...] = jnp.full(m_ref.shape, -jnp.inf, F32)
    l_ref[...] = jnp.zeros(l_ref.shape, F32)
    acc_ref[...] = jnp.zeros(acc_ref.shape, F32)

    def step(j, masked):
        start = pl.multiple_of(j * tq, tq)
        k = k_ref[pl.ds(start, tq), :]
        v = v_ref[pl.ds(start, tq), :]
        s = lax.dot_general(q2x, k, (((1,), (1,)), ((), ())), preferred_element_type=F32)
        if masked:
            r = lax.broadcasted_iota(jnp.int32, s.shape, 0)
            c = lax.broadcasted_iota(jnp.int32, s.shape, 1)
            s = jnp.where(c <= jnp.where(r >= tq, r - tq, r), s, -jnp.inf)
        m_old = m_ref[...]
        m_new = jnp.maximum(m_old, jnp.max(s, axis=-1, keepdims=True))
        alpha = jnp.exp(m_old - m_new)
        p = jnp.exp(s - m_new)
        l_ref[...] = alpha * l_ref[...] + jnp.sum(p, axis=-1, keepdims=True)
        acc_ref[...] = alpha * acc_ref[...] + jnp.dot(p.astype(v.dtype), v, preferred_element_type=F32)
        m_ref[...] = m_new

    def body(j, carry):
        step(j, False)
        return carry

    lax.fori_loop(0, qi, body, 0)
    step(qi, True)

    o = acc_ref[...] / l_ref[...]
    o = o[:tq] - lam_ref[0] * o[tq:]
    o_ref[...] = (_rms(o, 1e-5) * (sw_ref[...] * out_scale)).astype(o_ref.dtype)


def _diff_attention(proj3, lam, subln_w, lam_init, tq):
    b, s, _ = proj3.shape
    hw = 2 * ATT_DH
    kern = functools.partial(_attn_kernel, tq=tq, out_scale=1.0 - lam_init)
    return pl.pallas_call(
        kern,
        grid=(b, ATT_HEADS, s // tq),
        in_specs=[pl.BlockSpec(memory_space=pltpu.SMEM),
                  pl.BlockSpec((None, tq, hw), lambda bb, h, i: (bb, i, COL_Q // hw + h)),
                  pl.BlockSpec((None, s, hw), lambda bb, h, i: (bb, 0, COL_K // hw + h)),
                  pl.BlockSpec((None, s, hw), lambda bb, h, i: (bb, 0, COL_V // hw + h)),
                  pl.BlockSpec((1, hw), lambda bb, h, i: (0, 0))],
        out_specs=pl.BlockSpec((None, tq, hw), lambda bb, h, i: (bb, i, h)),
        out_shape=jax.ShapeDtypeStruct((b, s, ATT_WIDTH), BF16),
        scratch_shapes=[pltpu.VMEM((2 * tq, 1), F32), pltpu.VMEM((2 * tq, 1), F32),
                        pltpu.VMEM((2 * tq, hw), F32)],
        compiler_params=_params("parallel", "parallel", "arbitrary"),
        name="diff_attention",
    )(lam.reshape(1).astype(F32), proj3, proj3, proj3, subln_w.reshape(1, hw).astype(F32))


CONV_HALO = 32
CONV_ROWS = 64


def _cconv_kernel(a_ref, g_ref, w_ref, b_ref, lnw_ref, lnb_ref, o_ref, buf_ref, *, ts):
    i = pl.program_id(1)

    @pl.when(i == 0)
    def _():
        buf_ref[0:CONV_HALO, :] = jnp.zeros((CONV_HALO, CONV_CH), F32)

    @pl.when(i > 0)
    def _():
        buf_ref[0:CONV_HALO, :] = buf_ref[ts:ts + CONV_HALO, :]

    buf_ref[CONV_HALO:CONV_HALO + ts, :] = a_ref[...].astype(F32) * _sigmoid(g_ref[...].astype(F32))

    first = CONV_HALO - (CONV_K - 1)
    for c in range(ts // CONV_ROWS):
        base = c * CONV_ROWS
        acc = jnp.zeros((CONV_ROWS, CONV_CH), F32) + b_ref[...]
        for k in range(CONV_K):
            acc = acc + w_ref[k:k + 1, :] * buf_ref[base + first + k:base + first + k + CONV_ROWS, :]
        mu = jnp.mean(acc, axis=-1, keepdims=True)
        xc = acc - mu
        var = jnp.mean(xc * xc, axis=-1, keepdims=True)
        y = xc * lax.rsqrt(var + 1e-5) * lnw_ref[...] + lnb_ref[...]
        o_ref[base:base + CONV_ROWS, :] = (y * _sigmoid(y)).astype(o_ref.dtype)


def _conformer_conv(proj3, dw_w, dw_b, ln_w, ln_b, ts):
    b, s, _ = proj3.shape
    c = CONV_CH
    row = lambda a: a.reshape(1, c).astype(F32)
    return pl.pallas_call(
        functools.partial(_cconv_kernel, ts=ts),
        grid=(b, s // ts),
        in_specs=[pl.BlockSpec((None, ts, c), lambda bb, i: (bb, i, COL_U // c)),
                  pl.BlockSpec((None, ts, c), lambda bb, i: (bb, i, COL_U // c + 1)),
                  pl.BlockSpec((CONV_K, c), lambda bb, i: (0, 0)),
                  pl.BlockSpec((1, c), lambda bb, i: (0, 0)),
                  pl.BlockSpec((1, c), lambda bb, i: (0, 0)),
                  pl.BlockSpec((1, c), lambda bb, i: (0, 0))],
        out_specs=pl.BlockSpec((None, ts, c), lambda bb, i: (bb, i, 0)),
        out_shape=jax.ShapeDtypeStruct((b, s, c), BF16),
        scratch_shapes=[pltpu.VMEM((CONV_HALO + ts, c), F32)],
        compiler_params=_params("parallel", "arbitrary"),
        name="conformer_conv",
    )(proj3, proj3, dw_w.astype(F32), row(dw_b), row(ln_w), row(ln_b))


SSD_HALO = 8


def _ssd_kernel(z_ref, xbc_ref, dt_ref, cw_ref, cb_ref, dtb_ref, a_ref, dsk_ref, nw_ref, e_ref,
                o_ref, buf_ref, h_ref, *, chunk):
    L = chunk
    ci = pl.program_id(1)
    gw = SSM_D_INNER // SSM_GROUPS
    hpg = SSM_HEADS // SSM_GROUPS

    @pl.when(ci == 0)
    def _():
        buf_ref[0:SSD_HALO, :] = jnp.zeros((SSD_HALO, SSM_CONV_DIM), F32)
        h_ref[...] = jnp.zeros(h_ref.shape, F32)

    @pl.when(ci > 0)
    def _():
        buf_ref[0:SSD_HALO, :] = buf_ref[L:L + SSD_HALO, :]

    buf_ref[SSD_HALO:SSD_HALO + L, :] = xbc_ref[...].astype(F32)
    first = SSD_HALO - (SSM_CONV_K - 1)
    xc = jnp.zeros((L, SSM_CONV_DIM), F32) + cb_ref[...]
    for k in range(SSM_CONV_K):
        xc = xc + cw_ref[k:k + 1, :] * buf_ref[first + k:first + k + L, :]
    xc = xc * _sigmoid(xc)
    xs = xc[:, :SSM_D_INNER]
    bm = xc[:, SSM_D_INNER:SSM_D_INNER + SSM_GROUPS * SSM_STATE]
    cm = xc[:, SSM_D_INNER + SSM_GROUPS * SSM_STATE:]

    dtr = dt_ref[...] + dtb_ref[...]
    dt = jnp.maximum(dtr, 0.0) + jnp.log(1.0 + jnp.exp(-jnp.abs(dtr)))
    da = dt * a_ref[...]
    row = lax.broadcasted_iota(jnp.int32, (L, L), 0)
    col = lax.broadcasted_iota(jnp.int32, (L, L), 1)
    causal = col <= row
    a_cum = jnp.dot(causal.astype(F32), da, precision=HIGHEST, preferred_element_type=F32)
    a_cum_t = jnp.dot(da.T, (row <= col).astype(F32), precision=HIGHEST, preferred_element_type=F32)

    expand = e_ref[...]
    dt_full = jnp.dot(dt, expand, precision=HIGHEST, preferred_element_type=F32)
    ea_full = jnp.dot(jnp.exp(a_cum), expand, precision=HIGHEST, preferred_element_type=F32)
    dend_full = jnp.dot(jnp.exp(a_cum[L - 1:L, :] - a_cum), expand, precision=HIGHEST,
                        preferred_element_type=F32)
    cdec_full = ea_full[L - 1:L, :]

    xdt = xs * dt_full
    xdt_b = xdt.astype(BF16)
    xw_b = (xdt * dend_full).astype(BF16)

    y_parts = []
    for g in range(SSM_GROUPS):
        cg = cm[:, g * SSM_STATE:(g + 1) * SSM_STATE].astype(BF16)
        bg_t = bm[:, g * SSM_STATE:(g + 1) * SSM_STATE].T.astype(BF16)
        cbm = jnp.dot(cg, bg_t, preferred_element_type=F32)
        h_in = h_ref[:, g * gw:(g + 1) * gw]
        y_off = jnp.dot(cg, h_in.astype(BF16), preferred_element_type=F32)
        states = jnp.dot(bg_t, xw_b[:, g * gw:(g + 1) * gw], preferred_element_type=F32)
        h_ref[:, g * gw:(g + 1) * gw] = h_in * cdec_full[:, g * gw:(g + 1) * gw] + states
        yd = []
        for r in range(hpg):
            hh = g * hpg + r
            seg = a_cum[:, hh:hh + 1] - a_cum_t[hh:hh + 1, :]
            dec = jnp.exp(jnp.where(causal, seg, -jnp.inf))
            yd.append(jnp.dot((cbm * dec).astype(BF16), xdt_b[:, hh * SSM_HEADDIM:(hh + 1) * SSM_HEADDIM],
                              preferred_element_type=F32))
        y_g = (jnp.concatenate(yd, axis=-1) + y_off * ea_full[:, g * gw:(g + 1) * gw]
               + xs[:, g * gw:(g + 1) * gw] * dsk_ref[:, g * gw:(g + 1) * gw])
        zg = z_ref[:, g * gw:(g + 1) * gw].astype(F32)
        y_parts.append(_rms(y_g * (zg * _sigmoid(zg)), 1e-5))
    o_ref[...] = (jnp.concatenate(y_parts, axis=-1) * nw_ref[...]).astype(o_ref.dtype)


def _mamba2_ssd(proj3, dt3, conv_w, conv_b, dt_bias, a_log, d_skip, norm_w):
    b, s, _ = proj3.shape
    L = SSM_CHUNK
    di, cd = SSM_D_INNER, SSM_CONV_DIM
    pad_h = lambda a: jnp.pad(a.astype(F32), (0, LANES - SSM_HEADS)).reshape(1, LANES)
    head_of_channel = jnp.arange(di, dtype=jnp.int32) // SSM_HEADDIM
    expand = (jnp.arange(LANES, dtype=jnp.int32)[:, None] == head_of_channel[None, :]).astype(F32)
    const = lambda shape: pl.BlockSpec(shape, lambda bb, c: (0, 0))
    return pl.pallas_call(
        functools.partial(_ssd_kernel, chunk=L),
        grid=(b, s // L),
        in_specs=[pl.BlockSpec((None, L, di), lambda bb, c: (bb, c, COL_Z // di)),
                  pl.BlockSpec((None, L, cd), lambda bb, c: (bb, c, COL_XBC // cd)),
                  pl.BlockSpec((None, L, LANES), lambda bb, c: (bb, c, 0)),
                  const((SSM_CONV_K, cd)), const((1, cd)), const((1, LANES)), const((1, LANES)),
                  const((1, di)), const((1, di)), const((LANES, di))],
        out_specs=pl.BlockSpec((None, L, di), lambda bb, c: (bb, c, 0)),
        out_shape=jax.ShapeDtypeStruct((b, s, di), BF16),
        scratch_shapes=[pltpu.VMEM((SSD_HALO + L, cd), F32), pltpu.VMEM((SSM_STATE, di), F32)],
        compiler_params=_params("parallel", "arbitrary"),
        name="mamba2_ssd",
    )(proj3, proj3, dt3, conv_w.astype(F32), conv_b.reshape(1, cd).astype(F32), pad_h(dt_bias),
      pad_h(-jnp.exp(a_log.astype(F32))), jnp.repeat(d_skip.astype(F32), SSM_HEADDIM).reshape(1, di),
      norm_w.reshape(1, di).astype(F32), expand)


def _merge_kernel(x_ref, ya_ref, yc_ref, ys_ref, ga_ref, gc_ref, gs_ref, wa_ref, wc_ref, ws_ref, wm_ref, o_ref):
    def branch(y_ref, g_ref, w_ref):
        return _sigmoid(g_ref[...].astype(F32)) * jnp.dot(y_ref[...], w_ref[...], preferred_element_type=F32)

    merged = branch(ya_ref, ga_ref, wa_ref) + branch(yc_ref, gc_ref, wc_ref) + branch(ys_ref, gs_ref, ws_ref)
    o_ref[...] = x_ref[...] + jnp.dot(merged.astype(BF16), wm_ref[...], preferred_element_type=F32)


def _merge(x, ya, yc, ys, proj, wa, wc, ws, wm, tm):
    t, d = x.shape
    w = ya.shape[1]
    rows = lambda width, cb: pl.BlockSpec((tm, width), lambda i: (i, cb))
    const = lambda shape: pl.BlockSpec(shape, lambda i: (0, 0))
    return pl.pallas_call(
        _merge_kernel,
        grid=(t // tm,),
        in_specs=[rows(d, 0), rows(w, 0), rows(w, 0), rows(w, 0),
                  rows(d, COL_G // d), rows(d, COL_G // d + 1), rows(d, COL_G // d + 2),
                  const((w, d)), const((w, d)), const((w, d)), const((d, d))],
        out_specs=rows(d, 0),
        out_shape=jax.ShapeDtypeStruct((t, d), F32),
        compiler_params=_params("parallel"),
        name="gated_merge",
    )(x, ya, yc, ys, proj, proj, proj, wa, wc, ws, wm)


def _ffn_kernel(x_ref, g_ref, wg_ref, wu_ref, wd_ref, o_ref, h_ref, acc_ref):
    j = pl.program_id(1)

    @pl.when(j == 0)
    def _():
        h_ref[...] = (_rms(x_ref[...], NORM_EPS) * g_ref[...]).astype(h_ref.dtype)
        acc_ref[...] = jnp.zeros(acc_ref.shape, F32)

    h = h_ref[...]
    a = jnp.dot(h, wg_ref[...], preferred_element_type=F32)
    u = jnp.dot(h, wu_ref[...], preferred_element_type=F32)
    acc_ref[...] += jnp.dot((a * _sigmoid(a) * u).astype(BF16), wd_ref[...], preferred_element_type=F32)

    @pl.when(j == pl.num_programs(1) - 1)
    def _():
        o_ref[...] = x_ref[...] + acc_ref[...]


def _ffn_dense(x, g, wg, wu, wd, tm, tf):
    t, d = x.shape
    f = wg.shape[1]
    return pl.pallas_call(
        _ffn_kernel,
        grid=(t // tm, f // tf),
        in_specs=[pl.BlockSpec((tm, d), lambda i, j: (i, 0)),
                  pl.BlockSpec((1, d), lambda i, j: (0, 0)),
                  pl.BlockSpec((d, tf), lambda i, j: (0, j)),
                  pl.BlockSpec((d, tf), lambda i, j: (0, j)),
                  pl.BlockSpec((tf, d), lambda i, j: (j, 0))],
        out_specs=pl.BlockSpec((tm, d), lambda i, j: (i, 0)),
        out_shape=jax.ShapeDtypeStruct((t, d), F32),
        scratch_shapes=[pltpu.VMEM((tm, d), BF16), pltpu.VMEM((tm, d), F32)],
        compiler_params=_params("parallel", "arbitrary"),
        name="ffn_dense",
    )(x, g.reshape(1, d), wg, wu, wd)


def _router_kernel(x_ref, g_ref, wr_ref, br_ref, h_ref, idx_ref, wt_ref):
    h = _rms(x_ref[...], NORM_EPS) * g_ref[...]
    h_ref[...] = h.astype(h_ref.dtype)
    logits = jnp.dot(h, wr_ref[...], precision=HIGHEST, preferred_element_type=F32) + br_ref[...]
    lane = lax.broadcasted_iota(jnp.int32, logits.shape, 1)
    logits = jnp.where(lane < N_EXPERTS, logits, -jnp.inf)
    m1 = jnp.max(logits, axis=-1, keepdims=True)
    i1 = jnp.min(jnp.where(logits == m1, lane, LANES), axis=-1, keepdims=True)
    rest = jnp.where(lane == i1, -jnp.inf, logits)
    m2 = jnp.max(rest, axis=-1, keepdims=True)
    i2 = jnp.min(jnp.where(rest == m2, lane, LANES), axis=-1, keepdims=True)
    e = jnp.exp(m2 - m1)
    w1 = 1.0 / (1.0 + e)
    idx_ref[...] = jnp.where(lane == 0, i1, i2)[:, :TOP_K]
    wt_ref[...] = jnp.where(lane == 0, w1, e * w1)[:, :TOP_K]


def _router(x, g, w_router, b_router, tm):
    t, d = x.shape
    wr = jnp.pad(w_router.astype(F32), ((0, 0), (0, LANES - N_EXPERTS)))
    br = jnp.pad(b_router.astype(F32), (0, LANES - N_EXPERTS)).reshape(1, LANES)
    return pl.pallas_call(
        _router_kernel,
        grid=(t // tm,),
        in_specs=[pl.BlockSpec((tm, d), lambda i: (i, 0)),
                  pl.BlockSpec((1, d), lambda i: (0, 0)),
                  pl.BlockSpec((d, LANES), lambda i: (0, 0)),
                  pl.BlockSpec((1, LANES), lambda i: (0, 0))],
        out_specs=[pl.BlockSpec((tm, d), lambda i: (i, 0)),
                   pl.BlockSpec((tm, TOP_K), lambda i: (i, 0)),
                   pl.BlockSpec((tm, TOP_K), lambda i: (i, 0))],
        out_shape=[jax.ShapeDtypeStruct((t, d), BF16),
                   jax.ShapeDtypeStruct((t, TOP_K), jnp.int32),
                   jax.ShapeDtypeStruct((t, TOP_K), F32)],
        compiler_params=_params("parallel"),
        name="moe_router",
    )(x, g.reshape(1, d), wr, br)


def _expert_kernel(be_ref, nu_ref, x_ref, rw_ref, wg_ref, wu_ref, wd_ref, o_ref, acc_ref):
    i = pl.program_id(0)
    j = pl.program_id(1)
    last = pl.num_programs(1) - 1
    used = i < nu_ref[0]

    @pl.when(jnp.logical_and(used, j == 0))
    def _():
        acc_ref[...] = jnp.zeros(acc_ref.shape, F32)

    @pl.when(used)
    def _():
        x = x_ref[...]
        a = jnp.dot(x, wg_ref[...], preferred_element_type=F32)
        u = jnp.dot(x, wu_ref[...], preferred_element_type=F32)
        acc_ref[...] += jnp.dot((a * _sigmoid(a) * u).astype(BF16), wd_ref[...], preferred_element_type=F32)

    @pl.when(jnp.logical_and(used, j == last))
    def _():
        o_ref[...] = acc_ref[...] * rw_ref[...]

    @pl.when(jnp.logical_and(jnp.logical_not(used), j == last))
    def _():
        o_ref[...] = jnp.zeros(o_ref.shape, o_ref.dtype)


def _expert_ffn(xs, row_w, block_exp, n_used, wg, wu, wd, bm, tf):
    n_rows, d = xs.shape
    f = wg.shape[2]
    nj = f // tf

    def jj(i, j, nu):
        return jnp.where(i < nu[0], j, nj - 1)

    grid_spec = pltpu.PrefetchScalarGridSpec(
        num_scalar_prefetch=2,
        grid=(n_rows // bm, nj),
        in_specs=[pl.BlockSpec((bm, d), lambda i, j, be, nu: (i, 0)),
                  pl.BlockSpec((bm, 1), lambda i, j, be, nu: (i, 0)),
                  pl.BlockSpec((None, d, tf), lambda i, j, be, nu: (be[i], 0, jj(i, j, nu))),
                  pl.BlockSpec((None, d, tf), lambda i, j, be, nu: (be[i], 0, jj(i, j, nu))),
                  pl.BlockSpec((None, tf, d), lambda i, j, be, nu: (be[i], jj(i, j, nu), 0))],
        out_specs=pl.BlockSpec((bm, d), lambda i, j, be, nu: (i, 0)),
        scratch_shapes=[pltpu.VMEM((bm, d), F32)],
    )
    return pl.pallas_call(
        _expert_kernel,
        grid_spec=grid_spec,
        out_shape=jax.ShapeDtypeStruct((n_rows, d), F32),
        compiler_params=_params("parallel", "arbitrary"),
        name="moe_experts",
    )(block_exp, n_used, xs, row_w.reshape(n_rows, 1), wg, wu, wd)


def _route(top_idx, top_w, bm):
    t = top_idx.shape[0]
    tk = t * TOP_K
    flat_e = top_idx.reshape(tk)
    flat_tok = jnp.repeat(jnp.arange(t, dtype=jnp.int32), TOP_K)
    order = jnp.argsort(flat_e)
    se = flat_e[order]
    counts = jnp.bincount(flat_e, length=N_EXPERTS).astype(jnp.int32)
    padded = (counts + bm - 1) // bm * bm
    start_sorted = jnp.cumsum(counts) - counts
    end_padded = jnp.cumsum(padded)
    start_padded = end_padded - padded
    dest_sorted = start_padded[se] + (jnp.arange(tk, dtype=jnp.int32) - start_sorted[se])
    n_blocks = tk // bm + N_EXPERTS
    n_rows = n_blocks * bm
    row_tok = jnp.zeros((n_rows,), jnp.int32).at[dest_sorted].set(flat_tok[order])
    row_w = jnp.zeros((n_rows,), F32).at[dest_sorted].set(top_w.reshape(tk)[order])
    dest = jnp.zeros((tk,), jnp.int32).at[order].set(dest_sorted).reshape(t, TOP_K)
    block_start = jnp.arange(n_blocks, dtype=jnp.int32) * bm
    block_exp = jnp.minimum(jnp.sum(block_start[:, None] >= end_padded[None, :], axis=1), N_EXPERTS - 1)
    n_used = (end_padded[-1] // bm).reshape(1)
    return row_tok, row_w, dest, block_exp.astype(jnp.int32), n_used.astype(jnp.int32)


def _add3_kernel(x_ref, a_ref, b_ref, o_ref):
    o_ref[...] = x_ref[...] + a_ref[...] + b_ref[...]


def _add3(x, a, b, tm):
    t, d = x.shape
    spec = pl.BlockSpec((tm, d), lambda i: (i, 0))
    return pl.pallas_call(
        _add3_kernel, grid=(t // tm,), in_specs=[spec, spec, spec], out_specs=spec,
        out_shape=jax.ShapeDtypeStruct((t, d), F32), compiler_params=_params("parallel"),
        name="moe_combine",
    )(x, a, b)


def _moe(x, g, w_router, b_router, wg, wu, wd, tm, bm, tf):
    h, top_idx, top_w = _router(x, g, w_router, b_router, tm)
    row_tok, row_w, dest, block_exp, n_used = _route(top_idx, top_w, bm)
    xs = jnp.take(h, row_tok, axis=0)
    ys = _expert_ffn(xs, row_w, block_exp, n_used, wg, wu, wd, bm, tf)
    return _add3(x, jnp.take(ys, dest[:, 0], axis=0), jnp.take(ys, dest[:, 1], axis=0), tm)


def _final_norm_kernel(x_ref, g_ref, o_ref):
    o_ref[...] = _rms(x_ref[...], NORM_EPS) * g_ref[...]


def _final_norm(x, g, tm):
    t, d = x.shape
    return pl.pallas_call(
        _final_norm_kernel, grid=(t // tm,),
        in_specs=[pl.BlockSpec((tm, d), lambda i: (i, 0)), pl.BlockSpec((1, d), lambda i: (0, 0))],
        out_specs=pl.BlockSpec((tm, d), lambda i: (i, 0)),
        out_shape=jax.ShapeDtypeStruct((t, d), F32), compiler_params=_params("parallel"),
        name="final_norm",
    )(x, g.reshape(1, d))


def _tiles(t, s):
    pick = lambda n, pref: pref if n % pref == 0 else n
    return dict(tm_proj=pick(t, 1024), tn_proj=1024, tq=pick(s, 256), ts_conv=pick(s, 512),
                tm_merge=pick(t, 512), tm_ffn=pick(t, 512), tm_row=pick(t, 1024), bm_moe=pick(t, 1024))


def kernel(x, ln_mix_w, w_in, attn_lq1, attn_lk1, attn_lq2, attn_lk2, attn_subln_w, attn_w_o, conv_dw_w, conv_dw_b, conv_ln_w, conv_ln_b, conv_w_pw, ssm_conv_w, ssm_conv_b, ssm_dt_bias, ssm_A_log, ssm_D, ssm_norm_w, ssm_w_out, w_mix_out, ln_ffn_w, ffn_w_gate, ffn_w_up, ffn_w_down, moe_w_router, moe_b_router, moe_w_gate, moe_w_up, moe_w_down, ln_final_w):
    b, s, d = x.shape
    t = b * s
    depth = w_in.shape[0]
    tl = _tiles(t, s)
    xf = x.reshape(t, d)
    for l in range(depth):
        w_main = jnp.concatenate([w_in[l][:, :COL_DT], w_in[l][:, COL_DT + SSM_HEADS:]], axis=1).astype(BF16)
        w_dt = jnp.pad(w_in[l][:, COL_DT:COL_DT + SSM_HEADS], ((0, 0), (0, LANES - SSM_HEADS))).astype(BF16)
        proj = _norm_matmul(xf, ln_mix_w[l], w_main, BF16, tl["tm_proj"], tl["tn_proj"])
        dt = _norm_matmul(xf, ln_mix_w[l], w_dt, F32, tl["tm_proj"], LANES)
        proj3 = proj.reshape(b, s, PROJ_COLS)

        lam_init = 0.8 - 0.6 * math.exp(-0.3 * l)
        lam = (jnp.exp(jnp.sum(attn_lq1[l].astype(F32) * attn_lk1[l].astype(F32)))
               - jnp.exp(jnp.sum(attn_lq2[l].astype(F32) * attn_lk2[l].astype(F32))) + lam_init)
        y_att = _diff_attention(proj3, lam, attn_subln_w[l], lam_init, tl["tq"])
        y_conv = _conformer_conv(proj3, conv_dw_w[l], conv_dw_b[l], conv_ln_w[l], conv_ln_b[l], tl["ts_conv"])
        y_ssm = _mamba2_ssd(proj3, dt.reshape(b, s, LANES), ssm_conv_w[l], ssm_conv_b[l], ssm_dt_bias[l],
                            ssm_A_log[l], ssm_D[l], ssm_norm_w[l])
        xf = _merge(xf, y_att.reshape(t, ATT_WIDTH), y_conv.reshape(t, CONV_CH), y_ssm.reshape(t, SSM_D_INNER),
                    proj, attn_w_o[l].astype(BF16), conv_w_pw[l].astype(BF16), ssm_w_out[l].astype(BF16),
                    w_mix_out[l].astype(BF16), tl["tm_merge"])

        i = l // 2
        if l % 2 == 0:
            f = ffn_w_gate.shape[2]
            tf = f // 2 if (f // 2) % LANES == 0 else f
            xf = _ffn_dense(xf, ln_ffn_w[l], ffn_w_gate[i].astype(BF16), ffn_w_up[i].astype(BF16),
                            ffn_w_down[i].astype(BF16), tl["tm_ffn"], tf)
        else:
            f = moe_w_gate.shape[3]
            tf = 512 if f % 512 == 0 else f
            xf = _moe(xf, ln_ffn_w[l], moe_w_router[i], moe_b_router[i], moe_w_gate[i].astype(BF16),
                      moe_w_up[i].astype(BF16), moe_w_down[i].astype(BF16), tl["tm_row"], tl["bm_moe"], tf)
    return _final_norm(xf, ln_final_w, tl["tm_row"]).reshape(b, s, d)
```

```python
import functools
import math

import jax
import jax.numpy as jnp
from jax import lax
from jax.experimental import pallas as pl
from jax.experimental.pallas import tpu as pltpu

F32 = jnp.float32
BF16 = jnp.bfloat16
HIGHEST = lax.Precision.HIGHEST

D_MODEL = 1024
ATT_HEADS = 4
ATT_DH = 64
ATT_WIDTH = ATT_HEADS * 2 * ATT_DH
CONV_CH = 512
CONV_K = 31
SSM_D_INNER = 512
SSM_HEADDIM = 64
SSM_HEADS = SSM_D_INNER // SSM_HEADDIM
SSM_GROUPS = 2
SSM_STATE = 128
SSM_CONV_K = 4
SSM_CONV_DIM = SSM_D_INNER + 2 * SSM_GROUPS * SSM_STATE
SSM_CHUNK = 128
N_EXPERTS = 8
TOP_K = 2
NORM_EPS = 1e-6

LANES = 128
SUBLANES = 8
VMEM_LIMIT_BYTES = 48 * 1024 * 1024

COL_Q = 0
COL_K = COL_Q + ATT_WIDTH
COL_V = COL_K + ATT_WIDTH
COL_U = COL_V + ATT_WIDTH
COL_Z = COL_U + 2 * CONV_CH
COL_XBC = COL_Z + SSM_D_INNER
COL_DT = COL_XBC + SSM_CONV_DIM
COL_G = COL_XBC + SSM_CONV_DIM
PROJ_COLS = COL_G + 3 * D_MODEL


def _params(*semantics):
    return pltpu.CompilerParams(dimension_semantics=semantics, vmem_limit_bytes=VMEM_LIMIT_BYTES)


def _sigmoid(x):
    return 1.0 / (1.0 + jnp.exp(-x))


def _rms(x, eps):
    return x * lax.rsqrt(jnp.mean(x * x, axis=-1, keepdims=True) + eps)


def _norm_matmul_kernel(x_ref, g_ref, w_ref, o_ref, h_ref):
    @pl.when(pl.program_id(1) == 0)
    def _():
        h_ref[...] = (_rms(x_ref[...], NORM_EPS) * g_ref[...]).astype(h_ref.dtype)

    o_ref[...] = jnp.dot(h_ref[...], w_ref[...], preferred_element_type=F32).astype(o_ref.dtype)


def _norm_matmul(x, g, w, out_dtype, tm, tn):
    t, d = x.shape
    n = w.shape[1]
    return pl.pallas_call(
        _norm_matmul_kernel,
        grid=(t // tm, n // tn),
        in_specs=[pl.BlockSpec((tm, d), lambda i, j: (i, 0)),
                  pl.BlockSpec((1, d), lambda i, j: (0, 0)),
                  pl.BlockSpec((d, tn), lambda i, j: (0, j))],
        out_specs=pl.BlockSpec((tm, tn), lambda i, j: (i, j)),
        out_shape=jax.ShapeDtypeStruct((t, n), out_dtype),
        scratch_shapes=[pltpu.VMEM((tm, d), BF16)],
        compiler_params=_params("parallel", "arbitrary"),
        name="norm_matmul",
    )(x, g.reshape(1, d), w)


def _attn_kernel(lam_ref, q_ref, k_ref, v_ref, sw_ref, o_ref, vt_ref, m_ref, l_ref, acc_ref, *, tq, tk, out_scale):
    qi = pl.program_id(2)

    @pl.when(qi == 0)
    def _():
        for c in range(vt_ref.shape[0]):
            vt_ref[c] = v_ref[c * tk:(c + 1) * tk, :].astype(F32).T.astype(BF16)

    q = q_ref[...] * (ATT_DH ** -0.5)
    lane = lax.broadcasted_iota(jnp.int32, q.shape, 1)
    zero = jnp.zeros_like(q)
    q2x = jnp.concatenate([jnp.where(lane < ATT_DH, q, zero), jnp.where(lane >= ATT_DH, q, zero)], axis=0)

    m_ref[...] = jnp.full(m_ref.shape, -jnp.inf, F32)
    l_ref[...] = jnp.zeros(l_ref.shape, F32)
    acc_ref[...] = jnp.zeros(acc_ref.shape, F32)

    def step(j, masked):
        start = pl.multiple_of(j * tk, tk)
        k = k_ref[pl.ds(start, tk), :]
        st = lax.dot_general(k, q2x, (((1,), (1,)), ((), ())), preferred_element_type=F32)
        if masked:
            kpos = start + lax.broadcasted_iota(jnp.int32, st.shape, 0)
            c = lax.broadcasted_iota(jnp.int32, st.shape, 1)
            st = jnp.where(kpos <= qi * tq + jnp.where(c >= tq, c - tq, c), st, -jnp.inf)
        m_old = m_ref[...]
        m_new = jnp.maximum(m_old, jnp.max(st, axis=0, keepdims=True))
        alpha = jnp.exp(m_old - m_new)
        p = jnp.exp(st - m_new)
        l_ref[...] = alpha * l_ref[...] + jnp.sum(p, axis=0, keepdims=True)
        acc_ref[...] = alpha * acc_ref[...] + jnp.dot(vt_ref[j], p.astype(BF16), preferred_element_type=F32)
        m_ref[...] = m_new

    def body(j, carry):
        step(j, False)
        return carry

    n_full = (qi * tq) // tk
    lax.fori_loop(0, n_full, body, 0)
    step(n_full, True)

    o_t = acc_ref[...] * (1.0 / l_ref[...])
    o = (o_t[:, :tq] - lam_ref[0] * o_t[:, tq:]).T
    o_ref[...] = (_rms(o, 1e-5) * (sw_ref[...] * out_scale)).astype(o_ref.dtype)


def _diff_attention(proj3, lam, subln_w, lam_init, tq, tk):
    b, s, _ = proj3.shape
    hw = 2 * ATT_DH
    kern = functools.partial(_attn_kernel, tq=tq, tk=tk, out_scale=1.0 - lam_init)
    return pl.pallas_call(
        kern,
        grid=(b, ATT_HEADS, s // tq),
        in_specs=[pl.BlockSpec(memory_space=pltpu.SMEM),
                  pl.BlockSpec((None, tq, hw), lambda bb, h, i: (bb, i, COL_Q // hw + h)),
                  pl.BlockSpec((None, s, hw), lambda bb, h, i: (bb, 0, COL_K // hw + h)),
                  pl.BlockSpec((None, s, hw), lambda bb, h, i: (bb, 0, COL_V // hw + h)),
                  pl.BlockSpec((1, hw), lambda bb, h, i: (0, 0))],
        out_specs=pl.BlockSpec((None, tq, hw), lambda bb, h, i: (bb, i, h)),
        out_shape=jax.ShapeDtypeStruct((b, s, ATT_WIDTH), BF16),
        scratch_shapes=[pltpu.VMEM((s // tk, hw, tk), BF16), pltpu.VMEM((1, 2 * tq), F32),
                        pltpu.VMEM((1, 2 * tq), F32), pltpu.VMEM((hw, 2 * tq), F32)],
        compiler_params=_params("parallel", "parallel", "arbitrary"),
        name="diff_attention",
    )(lam.reshape(1).astype(F32), proj3, proj3, proj3, subln_w.reshape(1, hw).astype(F32))


CONV_HALO = 32
CONV_ROWS = 64


def _cconv_kernel(a_ref, g_ref, w_ref, b_ref, lnw_ref, lnb_ref, o_ref, buf_ref, *, ts):
    i = pl.program_id(1)

    @pl.when(i == 0)
    def _():
        buf_ref[0:CONV_HALO, :] = jnp.zeros((CONV_HALO, CONV_CH), F32)

    @pl.when(i > 0)
    def _():
        buf_ref[0:CONV_HALO, :] = buf_ref[ts:ts + CONV_HALO, :]

    buf_ref[CONV_HALO:CONV_HALO + ts, :] = a_ref[...].astype(F32) * _sigmoid(g_ref[...].astype(F32))

    first = CONV_HALO - (CONV_K - 1)
    for c in range(ts // CONV_ROWS):
        base = c * CONV_ROWS
        acc = jnp.zeros((CONV_ROWS, CONV_CH), F32) + b_ref[...]
        for k in range(CONV_K):
            acc = acc + w_ref[k:k + 1, :] * buf_ref[base + first + k:base + first + k + CONV_ROWS, :]
        mu = jnp.mean(acc, axis=-1, keepdims=True)
        xc = acc - mu
        var = jnp.mean(xc * xc, axis=-1, keepdims=True)
        y = xc * lax.rsqrt(var + 1e-5) * lnw_ref[...] + lnb_ref[...]
        o_ref[base:base + CONV_ROWS, :] = (y * _sigmoid(y)).astype(o_ref.dtype)


def _conformer_conv(proj3, dw_w, dw_b, ln_w, ln_b, ts):
    b, s, _ = proj3.shape
    c = CONV_CH
    row = lambda a: a.reshape(1, c).astype(F32)
    return pl.pallas_call(
        functools.partial(_cconv_kernel, ts=ts),
        grid=(b, s // ts),
        in_specs=[pl.BlockSpec((None, ts, c), lambda bb, i: (bb, i, COL_U // c)),
                  pl.BlockSpec((None, ts, c), lambda bb, i: (bb, i, COL_U // c + 1)),
                  pl.BlockSpec((CONV_K, c), lambda bb, i: (0, 0)),
                  pl.BlockSpec((1, c), lambda bb, i: (0, 0)),
                  pl.BlockSpec((1, c), lambda bb, i: (0, 0)),
                  pl.BlockSpec((1, c), lambda bb, i: (0, 0))],
        out_specs=pl.BlockSpec((None, ts, c), lambda bb, i: (bb, i, 0)),
        out_shape=jax.ShapeDtypeStruct((b, s, c), BF16),
        scratch_shapes=[pltpu.VMEM((CONV_HALO + ts, c), F32)],
        compiler_params=_params("parallel", "arbitrary"),
        name="conformer_conv",
    )(proj3, proj3, dw_w.astype(F32), row(dw_b), row(ln_w), row(ln_b))


SSD_HALO = 8


def _ssd_kernel(z_ref, xbc_ref, dt_ref, cw_ref, cb_ref, dtb_ref, a_ref, dsk_ref, nw_ref, e_ref,
                o_ref, buf_ref, h_ref, *, chunk):
    L = chunk
    ci = pl.program_id(1)
    gw = SSM_D_INNER // SSM_GROUPS
    hpg = SSM_HEADS // SSM_GROUPS

    @pl.when(ci == 0)
    def _():
        buf_ref[0:SSD_HALO, :] = jnp.zeros((SSD_HALO, SSM_CONV_DIM), F32)
        h_ref[...] = jnp.zeros(h_ref.shape, F32)

    @pl.when(ci > 0)
    def _():
        buf_ref[0:SSD_HALO, :] = buf_ref[L:L + SSD_HALO, :]

    buf_ref[SSD_HALO:SSD_HALO + L, :] = xbc_ref[...].astype(F32)
    first = SSD_HALO - (SSM_CONV_K - 1)
    xc = jnp.zeros((L, SSM_CONV_DIM), F32) + cb_ref[...]
    for k in range(SSM_CONV_K):
        xc = xc + cw_ref[k:k + 1, :] * buf_ref[first + k:first + k + L, :]
    xc = xc * _sigmoid(xc)
    xs = xc[:, :SSM_D_INNER]
    bm = xc[:, SSM_D_INNER:SSM_D_INNER + SSM_GROUPS * SSM_STATE]
    cm = xc[:, SSM_D_INNER + SSM_GROUPS * SSM_STATE:]

    dtr = dt_ref[...] + dtb_ref[...]
    dt = jnp.maximum(dtr, 0.0) + jnp.log(1.0 + jnp.exp(-jnp.abs(dtr)))
    da = dt * a_ref[...]
    row = lax.broadcasted_iota(jnp.int32, (L, L), 0)
    col = lax.broadcasted_iota(jnp.int32, (L, L), 1)
    causal = col <= row
    a_cum = jnp.dot(causal.astype(F32), da, precision=HIGHEST, preferred_element_type=F32)
    a_cum_t = jnp.dot(da.T, (row <= col).astype(F32), precision=HIGHEST, preferred_element_type=F32)

    expand = e_ref[...]
    dt_full = jnp.dot(dt, expand, precision=HIGHEST, preferred_element_type=F32)
    ea_full = jnp.dot(jnp.exp(a_cum), expand, precision=HIGHEST, preferred_element_type=F32)
    dend_full = jnp.dot(jnp.exp(a_cum[L - 1:L, :] - a_cum), expand, precision=HIGHEST,
                        preferred_element_type=F32)
    cdec_full = ea_full[L - 1:L, :]

    xdt = xs * dt_full
    xdt_b = xdt.astype(BF16)
    xw_b = (xdt * dend_full).astype(BF16)

    y_parts = []
    for g in range(SSM_GROUPS):
        cg = cm[:, g * SSM_STATE:(g + 1) * SSM_STATE].astype(BF16)
        bg_t = bm[:, g * SSM_STATE:(g + 1) * SSM_STATE].T.astype(BF16)
        cbm = jnp.dot(cg, bg_t, preferred_element_type=F32)
        h_in = h_ref[:, g * gw:(g + 1) * gw]
        y_off = jnp.dot(cg, h_in.astype(BF16), preferred_element_type=F32)
        states = jnp.dot(bg_t, xw_b[:, g * gw:(g + 1) * gw], preferred_element_type=F32)
        h_ref[:, g * gw:(g + 1) * gw] = h_in * cdec_full[:, g * gw:(g + 1) * gw] + states
        yd = []
        for r in range(hpg):
            hh = g * hpg + r
            seg = a_cum[:, hh:hh + 1] - a_cum_t[hh:hh + 1, :]
            dec = jnp.exp(jnp.where(causal, seg, -jnp.inf))
            yd.append(jnp.dot((cbm * dec).astype(BF16), xdt_b[:, hh * SSM_HEADDIM:(hh + 1) * SSM_HEADDIM],
                              preferred_element_type=F32))
        y_g = (jnp.concatenate(yd, axis=-1) + y_off * ea_full[:, g * gw:(g + 1) * gw]
               + xs[:, g * gw:(g + 1) * gw] * dsk_ref[:, g * gw:(g + 1) * gw])
        zg = z_ref[:, g * gw:(g + 1) * gw].astype(F32)
        y_parts.append(_rms(y_g * (zg * _sigmoid(zg)), 1e-5))
    o_ref[...] = (jnp.concatenate(y_parts, axis=-1) * nw_ref[...]).astype(o_ref.dtype)


def _mamba2_ssd(proj3, dt3, conv_w, conv_b, dt_bias, a_log, d_skip, norm_w):
    b, s, _ = proj3.shape
    L = SSM_CHUNK
    di, cd = SSM_D_INNER, SSM_CONV_DIM
    pad_h = lambda a: jnp.pad(a.astype(F32), (0, LANES - SSM_HEADS)).reshape(1, LANES)
    head_of_channel = jnp.arange(di, dtype=jnp.int32) // SSM_HEADDIM
    expand = (jnp.arange(LANES, dtype=jnp.int32)[:, None] == head_of_channel[None, :]).astype(F32)
    const = lambda shape: pl.BlockSpec(shape, lambda bb, c: (0, 0))
    return pl.pallas_call(
        functools.partial(_ssd_kernel, chunk=L),
        grid=(b, s // L),
        in_specs=[pl.BlockSpec((None, L, di), lambda bb, c: (bb, c, COL_Z // di)),
                  pl.BlockSpec((None, L, cd), lambda bb, c: (bb, c, COL_XBC // cd)),
                  pl.BlockSpec((None, L, LANES), lambda bb, c: (bb, c, 0)),
                  const((SSM_CONV_K, cd)), const((1, cd)), const((1, LANES)), const((1, LANES)),
                  const((1, di)), const((1, di)), const((LANES, di))],
        out_specs=pl.BlockSpec((None, L, di), lambda bb, c: (bb, c, 0)),
        out_shape=jax.ShapeDtypeStruct((b, s, di), BF16),
        scratch_shapes=[pltpu.VMEM((SSD_HALO + L, cd), F32), pltpu.VMEM((SSM_STATE, di), F32)],
        compiler_params=_params("parallel", "arbitrary"),
        name="mamba2_ssd",
    )(proj3, proj3, dt3, conv_w.astype(F32), conv_b.reshape(1, cd).astype(F32), pad_h(dt_bias),
      pad_h(-jnp.exp(a_log.astype(F32))), jnp.repeat(d_skip.astype(F32), SSM_HEADDIM).reshape(1, di),
      norm_w.reshape(1, di).astype(F32), expand)


def _merge_kernel(x_ref, ya_ref, yc_ref, ys_ref, ga_ref, gc_ref, gs_ref, wa_ref, wc_ref, ws_ref, wm_ref, o_ref):
    def branch(y_ref, g_ref, w_ref):
        return _sigmoid(g_ref[...].astype(F32)) * jnp.dot(y_ref[...], w_ref[...], preferred_element_type=F32)

    merged = branch(ya_ref, ga_ref, wa_ref) + branch(yc_ref, gc_ref, wc_ref) + branch(ys_ref, gs_ref, ws_ref)
    o_ref[...] = x_ref[...] + jnp.dot(merged.astype(BF16), wm_ref[...], preferred_element_type=F32)


def _merge(x, ya, yc, ys, proj, wa, wc, ws, wm, tm):
    t, d = x.shape
    w = ya.shape[1]
    rows = lambda width, cb: pl.BlockSpec((tm, width), lambda i: (i, cb))
    const = lambda shape: pl.BlockSpec(shape, lambda i: (0, 0))
    return pl.pallas_call(
        _merge_kernel,
        grid=(t // tm,),
        in_specs=[rows(d, 0), rows(w, 0), rows(w, 0), rows(w, 0),
                  rows(d, COL_G // d), rows(d, COL_G // d + 1), rows(d, COL_G // d + 2),
                  const((w, d)), const((w, d)), const((w, d)), const((d, d))],
        out_specs=rows(d, 0),
        out_shape=jax.ShapeDtypeStruct((t, d), F32),
        compiler_params=_params("parallel"),
        name="gated_merge",
    )(x, ya, yc, ys, proj, proj, proj, wa, wc, ws, wm)


def _ffn_kernel(x_ref, g_ref, wg_ref, wu_ref, wd_ref, o_ref, h_ref, acc_ref):
    j = pl.program_id(1)

    @pl.when(j == 0)
    def _():
        h_ref[...] = (_rms(x_ref[...], NORM_EPS) * g_ref[...]).astype(h_ref.dtype)
        acc_ref[...] = jnp.zeros(acc_ref.shape, F32)

    h = h_ref[...]
    a = jnp.dot(h, wg_ref[...], preferred_element_type=F32)
    u = jnp.dot(h, wu_ref[...], preferred_element_type=F32)
    acc_ref[...] += jnp.dot((a * _sigmoid(a) * u).astype(BF16), wd_ref[...], preferred_element_type=F32)

    @pl.when(j == pl.num_programs(1) - 1)
    def _():
        o_ref[...] = x_ref[...] + acc_ref[...]


def _ffn_dense(x, g, wg, wu, wd, tm, tf):
    t, d = x.shape
    f = wg.shape[1]
    return pl.pallas_call(
        _ffn_kernel,
        grid=(t // tm, f // tf),
        in_specs=[pl.BlockSpec((tm, d), lambda i, j: (i, 0)),
                  pl.BlockSpec((1, d), lambda i, j: (0, 0)),
                  pl.BlockSpec((d, tf), lambda i, j: (0, j)),
                  pl.BlockSpec((d, tf), lambda i, j: (0, j)),
                  pl.BlockSpec((tf, d), lambda i, j: (j, 0))],
        out_specs=pl.BlockSpec((tm, d), lambda i, j: (i, 0)),
        out_shape=jax.ShapeDtypeStruct((t, d), F32),
        scratch_shapes=[pltpu.VMEM((tm, d), BF16), pltpu.VMEM((tm, d), F32)],
        compiler_params=_params("parallel", "arbitrary"),
        name="ffn_dense",
    )(x, g.reshape(1, d), wg, wu, wd)


def _router_kernel(x_ref, g_ref, wr_ref, br_ref, h_ref, idx_ref, wt_ref):
    h = _rms(x_ref[...], NORM_EPS) * g_ref[...]
    h_ref[...] = h.astype(h_ref.dtype)
    logits = jnp.dot(h, wr_ref[...], precision=HIGHEST, preferred_element_type=F32) + br_ref[...]
    lane = lax.broadcasted_iota(jnp.int32, logits.shape, 1)
    logits = jnp.where(lane < N_EXPERTS, logits, -jnp.inf)
    m1 = jnp.max(logits, axis=-1, keepdims=True)
    i1 = jnp.min(jnp.where(logits == m1, lane, LANES), axis=-1, keepdims=True)
    rest = jnp.where(lane == i1, -jnp.inf, logits)
    m2 = jnp.max(rest, axis=-1, keepdims=True)
    i2 = jnp.min(jnp.where(rest == m2, lane, LANES), axis=-1, keepdims=True)
    e = jnp.exp(m2 - m1)
    w1 = 1.0 / (1.0 + e)
    idx_ref[...] = jnp.where(lane == 0, i1, i2)[:, :TOP_K]
    wt_ref[...] = jnp.where(lane == 0, w1, e * w1)[:, :TOP_K]


def _router(x, g, w_router, b_router, tm):
    t, d = x.shape
    wr = jnp.pad(w_router.astype(F32), ((0, 0), (0, LANES - N_EXPERTS)))
    br = jnp.pad(b_router.astype(F32), (0, LANES - N_EXPERTS)).reshape(1, LANES)
    return pl.pallas_call(
        _router_kernel,
        grid=(t // tm,),
        in_specs=[pl.BlockSpec((tm, d), lambda i: (i, 0)),
                  pl.BlockSpec((1, d), lambda i: (0, 0)),
                  pl.BlockSpec((d, LANES), lambda i: (0, 0)),
                  pl.BlockSpec((1, LANES), lambda i: (0, 0))],
        out_specs=[pl.BlockSpec((tm, d), lambda i: (i, 0)),
                   pl.BlockSpec((tm, TOP_K), lambda i: (i, 0)),
                   pl.BlockSpec((tm, TOP_K), lambda i: (i, 0))],
        out_shape=[jax.ShapeDtypeStruct((t, d), BF16),
                   jax.ShapeDtypeStruct((t, TOP_K), jnp.int32),
                   jax.ShapeDtypeStruct((t, TOP_K), F32)],
        compiler_params=_params("parallel"),
        name="moe_router",
    )(x, g.reshape(1, d), wr, br)


def _expert_kernel(be_ref, nu_ref, x_ref, rw_ref, wg_ref, wu_ref, wd_ref, o_ref, acc_ref):
    i = pl.program_id(0)
    j = pl.program_id(1)
    last = pl.num_programs(1) - 1
    used = i < nu_ref[0]

    @pl.when(jnp.logical_and(used, j == 0))
    def _():
        acc_ref[...] = jnp.zeros(acc_ref.shape, F32)

    @pl.when(used)
    def _():
        x = x_ref[...]
        a = jnp.dot(x, wg_ref[...], preferred_element_type=F32)
        u = jnp.dot(x, wu_ref[...], preferred_element_type=F32)
        acc_ref[...] += jnp.dot((a * _sigmoid(a) * u).astype(BF16), wd_ref[...], preferred_element_type=F32)

    @pl.when(jnp.logical_and(used, j == last))
    def _():
        o_ref[...] = acc_ref[...] * rw_ref[...]

    @pl.when(jnp.logical_and(jnp.logical_not(used), j == last))
    def _():
        o_ref[...] = jnp.zeros(o_ref.shape, o_ref.dtype)


def _expert_ffn(xs, row_w, block_exp, n_used, wg, wu, wd, bm, tf):
    n_rows, d = xs.shape
    f = wg.shape[2]
    nj = f // tf

    def jj(i, j, nu):
        return jnp.where(i < nu[0], j, nj - 1)

    grid_spec = pltpu.PrefetchScalarGridSpec(
        num_scalar_prefetch=2,
        grid=(n_rows // bm, nj),
        in_specs=[pl.BlockSpec((bm, d), lambda i, j, be, nu: (i, 0)),
                  pl.BlockSpec((bm, 1), lambda i, j, be, nu: (i, 0)),
                  pl.BlockSpec((None, d, tf), lambda i, j, be, nu: (be[i], 0, jj(i, j, nu))),
                  pl.BlockSpec((None, d, tf), lambda i, j, be, nu: (be[i], 0, jj(i, j, nu))),
                  pl.BlockSpec((None, tf, d), lambda i, j, be, nu: (be[i], jj(i, j, nu), 0))],
        out_specs=pl.BlockSpec((bm, d), lambda i, j, be, nu: (i, 0)),
        scratch_shapes=[pltpu.VMEM((bm, d), F32)],
    )
    return pl.pallas_call(
        _expert_kernel,
        grid_spec=grid_spec,
        out_shape=jax.ShapeDtypeStruct((n_rows, d), F32),
        compiler_params=_params("parallel", "arbitrary"),
        name="moe_experts",
    )(block_exp, n_used, xs, row_w.reshape(n_rows, 1), wg, wu, wd)


def _route(top_idx, top_w, bm):
    t = top_idx.shape[0]
    tk = t * TOP_K
    flat_e = top_idx.reshape(tk)
    flat_tok = jnp.repeat(jnp.arange(t, dtype=jnp.int32), TOP_K)
    order = jnp.argsort(flat_e)
    se = flat_e[order]
    counts = jnp.bincount(flat_e, length=N_EXPERTS).astype(jnp.int32)
    padded = (counts + bm - 1) // bm * bm
    start_sorted = jnp.cumsum(counts) - counts
    end_padded = jnp.cumsum(padded)
    start_padded = end_padded - padded
    dest_sorted = start_padded[se] + (jnp.arange(tk, dtype=jnp.int32) - start_sorted[se])
    n_blocks = tk // bm + N_EXPERTS
    n_rows = n_blocks * bm
    row_tok = jnp.zeros((n_rows,), jnp.int32).at[dest_sorted].set(flat_tok[order])
    row_w = jnp.zeros((n_rows,), F32).at[dest_sorted].set(top_w.reshape(tk)[order])
    dest = jnp.zeros((tk,), jnp.int32).at[order].set(dest_sorted).reshape(t, TOP_K)
    block_start = jnp.arange(n_blocks, dtype=jnp.int32) * bm
    block_exp = jnp.minimum(jnp.sum(block_start[:, None] >= end_padded[None, :], axis=1), N_EXPERTS - 1)
    n_used = (end_padded[-1] // bm).reshape(1)
    return row_tok, row_w, dest, block_exp.astype(jnp.int32), n_used.astype(jnp.int32)


def _add3_kernel(x_ref, a_ref, b_ref, o_ref):
    o_ref[...] = x_ref[...] + a_ref[...] + b_ref[...]


def _add3(x, a, b, tm):
    t, d = x.shape
    spec = pl.BlockSpec((tm, d), lambda i: (i, 0))
    return pl.pallas_call(
        _add3_kernel, grid=(t // tm,), in_specs=[spec, spec, spec], out_specs=spec,
        out_shape=jax.ShapeDtypeStruct((t, d), F32), compiler_params=_params("parallel"),
        name="moe_combine",
    )(x, a, b)


def _moe(x, g, w_router, b_router, wg, wu, wd, tm, bm, tf):
    h, top_idx, top_w = _router(x, g, w_router, b_router, tm)
    row_tok, row_w, dest, block_exp, n_used = _route(top_idx, top_w, bm)
    xs = jnp.take(h, row_tok, axis=0)
    ys = _expert_ffn(xs, row_w, block_exp, n_used, wg, wu, wd, bm, tf)
    return _add3(x, jnp.take(ys, dest[:, 0], axis=0), jnp.take(ys, dest[:, 1], axis=0), tm)


def _final_norm_kernel(x_ref, g_ref, o_ref):
    o_ref[...] = _rms(x_ref[...], NORM_EPS) * g_ref[...]


def _final_norm(x, g, tm):
    t, d = x.shape
    return pl.pallas_call(
        _final_norm_kernel, grid=(t // tm,),
        in_specs=[pl.BlockSpec((tm, d), lambda i: (i, 0)), pl.BlockSpec((1, d), lambda i: (0, 0))],
        out_specs=pl.BlockSpec((tm, d), lambda i: (i, 0)),
        out_shape=jax.ShapeDtypeStruct((t, d), F32), compiler_params=_params("parallel"),
        name="final_norm",
    )(x, g.reshape(1, d))


def _tiles(t, s):
    pick = lambda n, pref: pref if n % pref == 0 else n
    return dict(tm_proj=pick(t, 1024), tn_proj=1024, tq=pick(s, 256), tk=pick(s, 512), ts_conv=pick(s, 512),
                tm_merge=pick(t, 512), tm_ffn=pick(t, 512), tm_row=pick(t, 1024), bm_moe=pick(t, 1024))


def kernel(x, ln_mix_w, w_in, attn_lq1, attn_lk1, attn_lq2, attn_lk2, attn_subln_w, attn_w_o, conv_dw_w, conv_dw_b, conv_ln_w, conv_ln_b, conv_w_pw, ssm_conv_w, ssm_conv_b, ssm_dt_bias, ssm_A_log, ssm_D, ssm_norm_w, ssm_w_out, w_mix_out, ln_ffn_w, ffn_w_gate, ffn_w_up, ffn_w_down, moe_w_router, moe_b_router, moe_w_gate, moe_w_up, moe_w_down, ln_final_w):
    b, s, d = x.shape
    t = b * s
    depth = w_in.shape[0]
    tl = _tiles(t, s)
    xf = x.reshape(t, d)
    for l in range(depth):
        w_main = jnp.concatenate([w_in[l][:, :COL_DT], w_in[l][:, COL_DT + SSM_HEADS:]], axis=1).astype(BF16)
        w_dt = jnp.pad(w_in[l][:, COL_DT:COL_DT + SSM_HEADS], ((0, 0), (0, LANES - SSM_HEADS))).astype(BF16)
        proj = _norm_matmul(xf, ln_mix_w[l], w_main, BF16, tl["tm_proj"], tl["tn_proj"])
        dt = _norm_matmul(xf, ln_mix_w[l], w_dt, F32, tl["tm_proj"], LANES)
        proj3 = proj.reshape(b, s, PROJ_COLS)

        lam_init = 0.8 - 0.6 * math.exp(-0.3 * l)
        lam = (jnp.exp(jnp.sum(attn_lq1[l].astype(F32) * attn_lk1[l].astype(F32)))
               - jnp.exp(jnp.sum(attn_lq2[l].astype(F32) * attn_lk2[l].astype(F32))) + lam_init)
        y_att = _diff_attention(proj3, lam, attn_subln_w[l], lam_init, tl["tq"], tl["tk"])
        y_conv = _conformer_conv(proj3, conv_dw_w[l], conv_dw_b[l], conv_ln_w[l], conv_ln_b[l], tl["ts_conv"])
        y_ssm = _mamba2_ssd(proj3, dt.reshape(b, s, LANES), ssm_conv_w[l], ssm_conv_b[l], ssm_dt_bias[l],
                            ssm_A_log[l], ssm_D[l], ssm_norm_w[l])
        xf = _merge(xf, y_att.reshape(t, ATT_WIDTH), y_conv.reshape(t, CONV_CH), y_ssm.reshape(t, SSM_D_INNER),
                    proj, attn_w_o[l].astype(BF16), conv_w_pw[l].astype(BF16), ssm_w_out[l].astype(BF16),
                    w_mix_out[l].astype(BF16), tl["tm_merge"])

        i = l // 2
        if l % 2 == 0:
            f = ffn_w_gate.shape[2]
            tf = f // 2 if (f // 2) % LANES == 0 else f
            xf = _ffn_dense(xf, ln_ffn_w[l], ffn_w_gate[i].astype(BF16), ffn_w_up[i].astype(BF16),
                            ffn_w_down[i].astype(BF16), tl["tm_ffn"], tf)
        else:
            f = moe_w_gate.shape[3]
            tf = 512 if f % 512 == 0 else f
            xf = _moe(xf, ln_ffn_w[l], moe_w_router[i], moe_b_router[i], moe_w_gate[i].astype(BF16),
                      moe_w_up[i].astype(BF16), moe_w_down[i].astype(BF16), tl["tm_row"], tl["bm_moe"], tf)
    return _final_norm(xf, ln_final_w, tl["tm_row"]).reshape(b, s, d)
```

```python
import functools
import math

import jax
import jax.numpy as jnp
from jax import lax
from jax.experimental import pallas as pl
from jax.experimental.pallas import tpu as pltpu

F32 = jnp.float32
BF16 = jnp.bfloat16
HIGHEST = lax.Precision.HIGHEST

D_MODEL = 1024
ATT_HEADS = 4
ATT_DH = 64
ATT_WIDTH = ATT_HEADS * 2 * ATT_DH
CONV_CH = 512
CONV_K = 31
SSM_D_INNER = 512
SSM_HEADDIM = 64
SSM_HEADS = SSM_D_INNER // SSM_HEADDIM
SSM_GROUPS = 2
SSM_STATE = 128
SSM_CONV_K = 4
SSM_CONV_DIM = SSM_D_INNER + 2 * SSM_GROUPS * SSM_STATE
SSM_CHUNK = 128
N_EXPERTS = 8
TOP_K = 2
NORM_EPS = 1e-6

LANES = 128
SUBLANES = 8
VMEM_LIMIT_BYTES = 48 * 1024 * 1024

COL_Q = 0
COL_K = COL_Q + ATT_WIDTH
COL_V = COL_K + ATT_WIDTH
COL_U = COL_V + ATT_WIDTH
COL_Z = COL_U + 2 * CONV_CH
COL_XBC = COL_Z + SSM_D_INNER
COL_DT = COL_XBC + SSM_CONV_DIM
COL_G = COL_XBC + SSM_CONV_DIM
PROJ_COLS = COL_G + 3 * D_MODEL


def _params(*semantics):
    return pltpu.CompilerParams(dimension_semantics=semantics, vmem_limit_bytes=VMEM_LIMIT_BYTES)


def _sigmoid(x):
    return 1.0 / (1.0 + jnp.exp(-x))


def _rms(x, eps):
    return x * lax.rsqrt(jnp.mean(x * x, axis=-1, keepdims=True) + eps)


def _norm_matmul_kernel(x_ref, g_ref, w_ref, o_ref, h_ref):
    @pl.when(pl.program_id(1) == 0)
    def _():
        h_ref[...] = (_rms(x_ref[...], NORM_EPS) * g_ref[...]).astype(h_ref.dtype)

    o_ref[...] = jnp.dot(h_ref[...], w_ref[...], preferred_element_type=F32).astype(o_ref.dtype)


def _norm_matmul(x, g, w, out_dtype, tm, tn):
    t, d = x.shape
    n = w.shape[1]
    return pl.pallas_call(
        _norm_matmul_kernel,
        grid=(t // tm, n // tn),
        in_specs=[pl.BlockSpec((tm, d), lambda i, j: (i, 0)),
                  pl.BlockSpec((1, d), lambda i, j: (0, 0)),
                  pl.BlockSpec((d, tn), lambda i, j: (0, j))],
        out_specs=pl.BlockSpec((tm, tn), lambda i, j: (i, j)),
        out_shape=jax.ShapeDtypeStruct((t, n), out_dtype),
        scratch_shapes=[pltpu.VMEM((tm, d), BF16)],
        compiler_params=_params("parallel", "arbitrary"),
        name="norm_matmul",
    )(x, g.reshape(1, d), w)


ATT_ONES = 16
LOG2E = 1.4426950408889634


def _attn_kernel(lam_ref, q_ref, k_ref, v_ref, sw_ref, o_ref, vt_ref, m_ref, acc_ref, sa_ref, ca_ref, sb_ref, cb_ref,
                 *, tq, tk, out_scale):
    qi = pl.program_id(2)
    hw = 2 * ATT_DH

    @pl.when(qi == 0)
    def _():
        for c in range(vt_ref.shape[0]):
            vt_ref[c, 0:hw, :] = v_ref[c * tk:(c + 1) * tk, :].astype(F32).T.astype(BF16)
            vt_ref[c, hw:hw + ATT_ONES, :] = jnp.ones((ATT_ONES, tk), BF16)

    q = (q_ref[...].astype(F32) * (ATT_DH ** -0.5 * LOG2E)).astype(BF16)
    lane = lax.broadcasted_iota(jnp.int32, q.shape, 1)
    zero = jnp.zeros_like(q)
    q2x = jnp.concatenate([jnp.where(lane < ATT_DH, q, zero), jnp.where(lane >= ATT_DH, q, zero)], axis=0)

    m_ref[...] = jnp.full(m_ref.shape, -jnp.inf, F32)
    acc_ref[...] = jnp.zeros(acc_ref.shape, F32)

    def scores(j):
        k = k_ref[pl.ds(pl.multiple_of(j * tk, tk), tk), :]
        return lax.dot_general(k, q2x, (((1,), (1,)), ((), ())), preferred_element_type=F32)

    def masked_scores(j):
        st = scores(j)
        kpos = j * tk + lax.broadcasted_iota(jnp.int32, st.shape, 0)
        c = lax.broadcasted_iota(jnp.int32, st.shape, 1)
        return jnp.where(kpos <= qi * tq + jnp.where(c >= tq, c - tq, c), st, -jnp.inf)

    def fill(buf, j, masked=False):
        s_ref, cmax_ref = buf
        st = masked_scores(j) if masked else scores(j)
        s_ref[...] = st
        cmax_ref[...] = jnp.max(st, axis=0, keepdims=True)

    def update(buf, j):
        s_ref, cmax_ref = buf
        m_old = m_ref[...]
        m_new = jnp.maximum(m_old, cmax_ref[...])
        p = jnp.exp2(s_ref[...] - m_new).astype(BF16)
        acc_ref[...] = jnp.exp2(m_old - m_new) * acc_ref[...] + jnp.dot(vt_ref[j], p, preferred_element_type=F32)
        m_ref[...] = m_new

    buf_a = (sa_ref, ca_ref)
    buf_b = (sb_ref, cb_ref)
    n_full = (qi * tq) // tk

    @pl.when(n_full == 0)
    def _():
        fill(buf_a, 0, masked=True)
        update(buf_a, 0)

    @pl.when(n_full > 0)
    def _():
        fill(buf_a, 0)
        n_pairs = (n_full - 1) // 2

        def body(i, carry):
            j = 2 * i + 1
            fill(buf_b, j)
            update(buf_a, j - 1)
            fill(buf_a, j + 1)
            update(buf_b, j)
            return carry

        lax.fori_loop(0, n_pairs, body, 0)

        def tail(cur, other):
            fill(other, n_full, masked=True)
            update(cur, n_full - 1)
            update(other, n_full)

        @pl.when(n_full - 1 == 2 * n_pairs)
        def _():
            tail(buf_a, buf_b)

        @pl.when(n_full - 1 != 2 * n_pairs)
        def _():
            fill(buf_b, n_full - 1)
            update(buf_a, n_full - 2)
            tail(buf_b, buf_a)

    acc = acc_ref[...]
    o_t = acc[:hw] * (1.0 / acc[hw:hw + 1])
    o = (o_t[:, :tq] - lam_ref[0] * o_t[:, tq:]).T
    o_ref[...] = (_rms(o, 1e-5) * (sw_ref[...] * out_scale)).astype(o_ref.dtype)


def _diff_attention(proj3, lam, subln_w, lam_init, tq, tk):
    b, s, _ = proj3.shape
    hw = 2 * ATT_DH
    kern = functools.partial(_attn_kernel, tq=tq, tk=tk, out_scale=1.0 - lam_init)
    return pl.pallas_call(
        kern,
        grid=(b, ATT_HEADS, s // tq),
        in_specs=[pl.BlockSpec(memory_space=pltpu.SMEM),
                  pl.BlockSpec((None, tq, hw), lambda bb, h, i: (bb, i, COL_Q // hw + h)),
                  pl.BlockSpec((None, s, hw), lambda bb, h, i: (bb, 0, COL_K // hw + h)),
                  pl.BlockSpec((None, s, hw), lambda bb, h, i: (bb, 0, COL_V // hw + h)),
                  pl.BlockSpec((1, hw), lambda bb, h, i: (0, 0))],
        out_specs=pl.BlockSpec((None, tq, hw), lambda bb, h, i: (bb, i, h)),
        out_shape=jax.ShapeDtypeStruct((b, s, ATT_WIDTH), BF16),
        scratch_shapes=[pltpu.VMEM((s // tk, hw + ATT_ONES, tk), BF16), pltpu.VMEM((1, 2 * tq), F32),
                        pltpu.VMEM((hw + ATT_ONES, 2 * tq), F32),
                        pltpu.VMEM((tk, 2 * tq), F32), pltpu.VMEM((1, 2 * tq), F32),
                        pltpu.VMEM((tk, 2 * tq), F32), pltpu.VMEM((1, 2 * tq), F32)],
        compiler_params=_params("parallel", "parallel", "arbitrary"),
        name="diff_attention",
    )(lam.reshape(1).astype(F32), proj3, proj3, proj3, subln_w.reshape(1, hw).astype(F32))


CONV_HALO = 32
CONV_ROWS = 64


def _cconv_kernel(a_ref, g_ref, w_ref, b_ref, lnw_ref, lnb_ref, o_ref, buf_ref, *, ts):
    i = pl.program_id(1)

    @pl.when(i == 0)
    def _():
        buf_ref[0:CONV_HALO, :] = jnp.zeros((CONV_HALO, CONV_CH), F32)

    @pl.when(i > 0)
    def _():
        buf_ref[0:CONV_HALO, :] = buf_ref[ts:ts + CONV_HALO, :]

    buf_ref[CONV_HALO:CONV_HALO + ts, :] = a_ref[...].astype(F32) * _sigmoid(g_ref[...].astype(F32))

    first = CONV_HALO - (CONV_K - 1)
    for c in range(ts // CONV_ROWS):
        base = c * CONV_ROWS
        acc = jnp.zeros((CONV_ROWS, CONV_CH), F32) + b_ref[...]
        for k in range(CONV_K):
            acc = acc + w_ref[k:k + 1, :] * buf_ref[base + first + k:base + first + k + CONV_ROWS, :]
        mu = jnp.mean(acc, axis=-1, keepdims=True)
        xc = acc - mu
        var = jnp.mean(xc * xc, axis=-1, keepdims=True)
        y = xc * lax.rsqrt(var + 1e-5) * lnw_ref[...] + lnb_ref[...]
        o_ref[base:base + CONV_ROWS, :] = (y * _sigmoid(y)).astype(o_ref.dtype)


def _conformer_conv(proj3, dw_w, dw_b, ln_w, ln_b, ts):
    b, s, _ = proj3.shape
    c = CONV_CH
    row = lambda a: a.reshape(1, c).astype(F32)
    return pl.pallas_call(
        functools.partial(_cconv_kernel, ts=ts),
        grid=(b, s // ts),
        in_specs=[pl.BlockSpec((None, ts, c), lambda bb, i: (bb, i, COL_U // c)),
                  pl.BlockSpec((None, ts, c), lambda bb, i: (bb, i, COL_U // c + 1)),
                  pl.BlockSpec((CONV_K, c), lambda bb, i: (0, 0)),
                  pl.BlockSpec((1, c), lambda bb, i: (0, 0)),
                  pl.BlockSpec((1, c), lambda bb, i: (0, 0)),
                  pl.BlockSpec((1, c), lambda bb, i: (0, 0))],
        out_specs=pl.BlockSpec((None, ts, c), lambda bb, i: (bb, i, 0)),
        out_shape=jax.ShapeDtypeStruct((b, s, c), BF16),
        scratch_shapes=[pltpu.VMEM((CONV_HALO + ts, c), F32)],
        compiler_params=_params("parallel", "arbitrary"),
        name="conformer_conv",
    )(proj3, proj3, dw_w.astype(F32), row(dw_b), row(ln_w), row(ln_b))


SSD_HALO = 8


def _ssd_kernel(z_ref, xbc_ref, dt_ref, cw_ref, cb_ref, dtb_ref, a_ref, dsk_ref, nw_ref, e_ref,
                o_ref, buf_ref, h_ref, *, chunk):
    L = chunk
    ci = pl.program_id(1)
    gw = SSM_D_INNER // SSM_GROUPS
    hpg = SSM_HEADS // SSM_GROUPS

    @pl.when(ci == 0)
    def _():
        buf_ref[0:SSD_HALO, :] = jnp.zeros((SSD_HALO, SSM_CONV_DIM), F32)
        h_ref[...] = jnp.zeros(h_ref.shape, F32)

    @pl.when(ci > 0)
    def _():
        buf_ref[0:SSD_HALO, :] = buf_ref[L:L + SSD_HALO, :]

    buf_ref[SSD_HALO:SSD_HALO + L, :] = xbc_ref[...].astype(F32)
    first = SSD_HALO - (SSM_CONV_K - 1)
    xc = jnp.zeros((L, SSM_CONV_DIM), F32) + cb_ref[...]
    for k in range(SSM_CONV_K):
        xc = xc + cw_ref[k:k + 1, :] * buf_ref[first + k:first + k + L, :]
    xc = xc * _sigmoid(xc)
    xs = xc[:, :SSM_D_INNER]
    bm = xc[:, SSM_D_INNER:SSM_D_INNER + SSM_GROUPS * SSM_STATE]
    cm = xc[:, SSM_D_INNER + SSM_GROUPS * SSM_STATE:]

    dtr = dt_ref[...] + dtb_ref[...]
    dt = jnp.maximum(dtr, 0.0) + jnp.log(1.0 + jnp.exp(-jnp.abs(dtr)))
    da = dt * a_ref[...]
    row = lax.broadcasted_iota(jnp.int32, (L, L), 0)
    col = lax.broadcasted_iota(jnp.int32, (L, L), 1)
    causal = col <= row
    a_cum = jnp.dot(causal.astype(F32), da, precision=HIGHEST, preferred_element_type=F32)
    a_cum_t = jnp.dot(da.T, (row <= col).astype(F32), precision=HIGHEST, preferred_element_type=F32)

    expand = e_ref[...]
    dt_full = jnp.dot(dt, expand, precision=HIGHEST, preferred_element_type=F32)
    ea_full = jnp.dot(jnp.exp(a_cum), expand, precision=HIGHEST, preferred_element_type=F32)
    dend_full = jnp.dot(jnp.exp(a_cum[L - 1:L, :] - a_cum), expand, precision=HIGHEST,
                        preferred_element_type=F32)
    cdec_full = ea_full[L - 1:L, :]

    xdt = xs * dt_full
    xdt_b = xdt.astype(BF16)
    xw_b = (xdt * dend_full).astype(BF16)

    y_parts = []
    for g in range(SSM_GROUPS):
        cg = cm[:, g * SSM_STATE:(g + 1) * SSM_STATE].astype(BF16)
        bg_t = bm[:, g * SSM_STATE:(g + 1) * SSM_STATE].T.astype(BF16)
        cbm = jnp.dot(cg, bg_t, preferred_element_type=F32)
        h_in = h_ref[:, g * gw:(g + 1) * gw]
        y_off = jnp.dot(cg, h_in.astype(BF16), preferred_element_type=F32)
        states = jnp.dot(bg_t, xw_b[:, g * gw:(g + 1) * gw], preferred_element_type=F32)
        h_ref[:, g * gw:(g + 1) * gw] = h_in * cdec_full[:, g * gw:(g + 1) * gw] + states
        yd = []
        for r in range(hpg):
            hh = g * hpg + r
            seg = a_cum[:, hh:hh + 1] - a_cum_t[hh:hh + 1, :]
            dec = jnp.exp(jnp.where(causal, seg, -jnp.inf))
            yd.append(jnp.dot((cbm * dec).astype(BF16), xdt_b[:, hh * SSM_HEADDIM:(hh + 1) * SSM_HEADDIM],
                              preferred_element_type=F32))
        y_g = (jnp.concatenate(yd, axis=-1) + y_off * ea_full[:, g * gw:(g + 1) * gw]
               + xs[:, g * gw:(g + 1) * gw] * dsk_ref[:, g * gw:(g + 1) * gw])
        zg = z_ref[:, g * gw:(g + 1) * gw].astype(F32)
        y_parts.append(_rms(y_g * (zg * _sigmoid(zg)), 1e-5))
    o_ref[...] = (jnp.concatenate(y_parts, axis=-1) * nw_ref[...]).astype(o_ref.dtype)


def _mamba2_ssd(proj3, dt3, conv_w, conv_b, dt_bias, a_log, d_skip, norm_w):
    b, s, _ = proj3.shape
    L = SSM_CHUNK
    di, cd = SSM_D_INNER, SSM_CONV_DIM
    pad_h = lambda a: jnp.pad(a.astype(F32), (0, LANES - SSM_HEADS)).reshape(1, LANES)
    head_of_channel = jnp.arange(di, dtype=jnp.int32) // SSM_HEADDIM
    expand = (jnp.arange(LANES, dtype=jnp.int32)[:, None] == head_of_channel[None, :]).astype(F32)
    const = lambda shape: pl.BlockSpec(shape, lambda bb, c: (0, 0))
    return pl.pallas_call(
        functools.partial(_ssd_kernel, chunk=L),
        grid=(b, s // L),
        in_specs=[pl.BlockSpec((None, L, di), lambda bb, c: (bb, c, COL_Z // di)),
                  pl.BlockSpec((None, L, cd), lambda bb, c: (bb, c, COL_XBC // cd)),
                  pl.BlockSpec((None, L, LANES), lambda bb, c: (bb, c, 0)),
                  const((SSM_CONV_K, cd)), const((1, cd)), const((1, LANES)), const((1, LANES)),
                  const((1, di)), const((1, di)), const((LANES, di))],
        out_specs=pl.BlockSpec((None, L, di), lambda bb, c: (bb, c, 0)),
        out_shape=jax.ShapeDtypeStruct((b, s, di), BF16),
        scratch_shapes=[pltpu.VMEM((SSD_HALO + L, cd), F32), pltpu.VMEM((SSM_STATE, di), F32)],
        compiler_params=_params("parallel", "arbitrary"),
        name="mamba2_ssd",
    )(proj3, proj3, dt3, conv_w.astype(F32), conv_b.reshape(1, cd).astype(F32), pad_h(dt_bias),
      pad_h(-jnp.exp(a_log.astype(F32))), jnp.repeat(d_skip.astype(F32), SSM_HEADDIM).reshape(1, di),
      norm_w.reshape(1, di).astype(F32), expand)


def _merge_kernel(x_ref, ya_ref, yc_ref, ys_ref, ga_ref, gc_ref, gs_ref, wa_ref, wc_ref, ws_ref, wm_ref, o_ref):
    def branch(y_ref, g_ref, w_ref):
        return _sigmoid(g_ref[...].astype(F32)) * jnp.dot(y_ref[...], w_ref[...], preferred_element_type=F32)

    merged = branch(ya_ref, ga_ref, wa_ref) + branch(yc_ref, gc_ref, wc_ref) + branch(ys_ref, gs_ref, ws_ref)
    o_ref[...] = x_ref[...] + jnp.dot(merged.astype(BF16), wm_ref[...], preferred_element_type=F32)


def _merge(x, ya, yc, ys, proj, wa, wc, ws, wm, tm):
    t, d = x.shape
    w = ya.shape[1]
    rows = lambda width, cb: pl.BlockSpec((tm, width), lambda i: (i, cb))
    const = lambda shape: pl.BlockSpec(shape, lambda i: (0, 0))
    return pl.pallas_call(
        _merge_kernel,
        grid=(t // tm,),
        in_specs=[rows(d, 0), rows(w, 0), rows(w, 0), rows(w, 0),
                  rows(d, COL_G // d), rows(d, COL_G // d + 1), rows(d, COL_G // d + 2),
                  const((w, d)), const((w, d)), const((w, d)), const((d, d))],
        out_specs=rows(d, 0),
        out_shape=jax.ShapeDtypeStruct((t, d), F32),
        compiler_params=_params("parallel"),
        name="gated_merge",
    )(x, ya, yc, ys, proj, proj, proj, wa, wc, ws, wm)


def _ffn_kernel(x_ref, g_ref, wg_ref, wu_ref, wd_ref, o_ref, h_ref, acc_ref):
    j = pl.program_id(1)

    @pl.when(j == 0)
    def _():
        h_ref[...] = (_rms(x_ref[...], NORM_EPS) * g_ref[...]).astype(h_ref.dtype)
        acc_ref[...] = jnp.zeros(acc_ref.shape, F32)

    h = h_ref[...]
    a = jnp.dot(h, wg_ref[...], preferred_element_type=F32)
    u = jnp.dot(h, wu_ref[...], preferred_element_type=F32)
    acc_ref[...] += jnp.dot((a * _sigmoid(a) * u).astype(BF16), wd_ref[...], preferred_element_type=F32)

    @pl.when(j == pl.num_programs(1) - 1)
    def _():
        o_ref[...] = x_ref[...] + acc_ref[...]


def _ffn_dense(x, g, wg, wu, wd, tm, tf):
    t, d = x.shape
    f = wg.shape[1]
    return pl.pallas_call(
        _ffn_kernel,
        grid=(t // tm, f // tf),
        in_specs=[pl.BlockSpec((tm, d), lambda i, j: (i, 0)),
                  pl.BlockSpec((1, d), lambda i, j: (0, 0)),
                  pl.BlockSpec((d, tf), lambda i, j: (0, j)),
                  pl.BlockSpec((d, tf), lambda i, j: (0, j)),
                  pl.BlockSpec((tf, d), lambda i, j: (j, 0))],
        out_specs=pl.BlockSpec((tm, d), lambda i, j: (i, 0)),
        out_shape=jax.ShapeDtypeStruct((t, d), F32),
        scratch_shapes=[pltpu.VMEM((tm, d), BF16), pltpu.VMEM((tm, d), F32)],
        compiler_params=_params("parallel", "arbitrary"),
        name="ffn_dense",
    )(x, g.reshape(1, d), wg, wu, wd)


def _router_kernel(x_ref, g_ref, wr_ref, br_ref, h_ref, idx_ref, wt_ref):
    h = _rms(x_ref[...], NORM_EPS) * g_ref[...]
    h_ref[...] = h.astype(h_ref.dtype)
    logits = jnp.dot(h, wr_ref[...], precision=HIGHEST, preferred_element_type=F32) + br_ref[...]
    lane = lax.broadcasted_iota(jnp.int32, logits.shape, 1)
    logits = jnp.where(lane < N_EXPERTS, logits, -jnp.inf)
    m1 = jnp.max(logits, axis=-1, keepdims=True)
    i1 = jnp.min(jnp.where(logits == m1, lane, LANES), axis=-1, keepdims=True)
    rest = jnp.where(lane == i1, -jnp.inf, logits)
    m2 = jnp.max(rest, axis=-1, keepdims=True)
    i2 = jnp.min(jnp.where(rest == m2, lane, LANES), axis=-1, keepdims=True)
    e = jnp.exp(m2 - m1)
    w1 = 1.0 / (1.0 + e)
    idx_ref[...] = jnp.where(lane == 0, i1, i2)[:, :TOP_K]
    wt_ref[...] = jnp.where(lane == 0, w1, e * w1)[:, :TOP_K]


def _router(x, g, w_router, b_router, tm):
    t, d = x.shape
    wr = jnp.pad(w_router.astype(F32), ((0, 0), (0, LANES - N_EXPERTS)))
    br = jnp.pad(b_router.astype(F32), (0, LANES - N_EXPERTS)).reshape(1, LANES)
    return pl.pallas_call(
        _router_kernel,
        grid=(t // tm,),
        in_specs=[pl.BlockSpec((tm, d), lambda i: (i, 0)),
                  pl.BlockSpec((1, d), lambda i: (0, 0)),
                  pl.BlockSpec((d, LANES), lambda i: (0, 0)),
                  pl.BlockSpec((1, LANES), lambda i: (0, 0))],
        out_specs=[pl.BlockSpec((tm, d), lambda i: (i, 0)),
                   pl.BlockSpec((tm, TOP_K), lambda i: (i, 0)),
                   pl.BlockSpec((tm, TOP_K), lambda i: (i, 0))],
        out_shape=[jax.ShapeDtypeStruct((t, d), BF16),
                   jax.ShapeDtypeStruct((t, TOP_K), jnp.int32),
                   jax.ShapeDtypeStruct((t, TOP_K), F32)],
        compiler_params=_params("parallel"),
        name="moe_router",
    )(x, g.reshape(1, d), wr, br)


def _expert_kernel(be_ref, nu_ref, x_ref, rw_ref, wg_ref, wu_ref, wd_ref, o_ref, acc_ref):
    i = pl.program_id(0)
    j = pl.program_id(1)
    last = pl.num_programs(1) - 1
    used = i < nu_ref[0]

    @pl.when(jnp.logical_and(used, j == 0))
    def _():
        acc_ref[...] = jnp.zeros(acc_ref.shape, F32)

    @pl.when(used)
    def _():
        x = x_ref[...]
        a = jnp.dot(x, wg_ref[...], preferred_element_type=F32)
        u = jnp.dot(x, wu_ref[...], preferred_element_type=F32)
        acc_ref[...] += jnp.dot((a * _sigmoid(a) * u).astype(BF16), wd_ref[...], preferred_element_type=F32)

    @pl.when(jnp.logical_and(used, j == last))
    def _():
        o_ref[...] = acc_ref[...] * rw_ref[...]

    @pl.when(jnp.logical_and(jnp.logical_not(used), j == last))
    def _():
        o_ref[...] = jnp.zeros(o_ref.shape, o_ref.dtype)


def _expert_ffn(xs, row_w, block_exp, n_used, wg, wu, wd, bm, tf):
    n_rows, d = xs.shape
    f = wg.shape[2]
    nj = f // tf

    def jj(i, j, nu):
        return jnp.where(i < nu[0], j, nj - 1)

    grid_spec = pltpu.PrefetchScalarGridSpec(
        num_scalar_prefetch=2,
        grid=(n_rows // bm, nj),
        in_specs=[pl.BlockSpec((bm, d), lambda i, j, be, nu: (i, 0)),
                  pl.BlockSpec((bm, 1), lambda i, j, be, nu: (i, 0)),
                  pl.BlockSpec((None, d, tf), lambda i, j, be, nu: (be[i], 0, jj(i, j, nu))),
                  pl.BlockSpec((None, d, tf), lambda i, j, be, nu: (be[i], 0, jj(i, j, nu))),
                  pl.BlockSpec((None, tf, d), lambda i, j, be, nu: (be[i], jj(i, j, nu), 0))],
        out_specs=pl.BlockSpec((bm, d), lambda i, j, be, nu: (i, 0)),
        scratch_shapes=[pltpu.VMEM((bm, d), F32)],
    )
    return pl.pallas_call(
        _expert_kernel,
        grid_spec=grid_spec,
        out_shape=jax.ShapeDtypeStruct((n_rows, d), F32),
        compiler_params=_params("parallel", "arbitrary"),
        name="moe_experts",
    )(block_exp, n_used, xs, row_w.reshape(n_rows, 1), wg, wu, wd)


def _route(top_idx, top_w, bm):
    t = top_idx.shape[0]
    tk = t * TOP_K
    flat_e = top_idx.reshape(tk)
    flat_tok = jnp.repeat(jnp.arange(t, dtype=jnp.int32), TOP_K)
    order = jnp.argsort(flat_e)
    se = flat_e[order]
    counts = jnp.bincount(flat_e, length=N_EXPERTS).astype(jnp.int32)
    padded = (counts + bm - 1) // bm * bm
    start_sorted = jnp.cumsum(counts) - counts
    end_padded = jnp.cumsum(padded)
    start_padded = end_padded - padded
    dest_sorted = start_padded[se] + (jnp.arange(tk, dtype=jnp.int32) - start_sorted[se])
    n_blocks = tk // bm + N_EXPERTS
    n_rows = n_blocks * bm
    row_tok = jnp.zeros((n_rows,), jnp.int32).at[dest_sorted].set(flat_tok[order])
    row_w = jnp.zeros((n_rows,), F32).at[dest_sorted].set(top_w.reshape(tk)[order])
    dest = jnp.zeros((tk,), jnp.int32).at[order].set(dest_sorted).reshape(t, TOP_K)
    block_start = jnp.arange(n_blocks, dtype=jnp.int32) * bm
    block_exp = jnp.minimum(jnp.sum(block_start[:, None] >= end_padded[None, :], axis=1), N_EXPERTS - 1)
    n_used = (end_padded[-1] // bm).reshape(1)
    return row_tok, row_w, dest, block_exp.astype(jnp.int32), n_used.astype(jnp.int32)


def _add3_kernel(x_ref, a_ref, b_ref, o_ref):
    o_ref[...] = x_ref[...] + a_ref[...] + b_ref[...]


def _add3(x, a, b, tm):
    t, d = x.shape
    spec = pl.BlockSpec((tm, d), lambda i: (i, 0))
    return pl.pallas_call(
        _add3_kernel, grid=(t // tm,), in_specs=[spec, spec, spec], out_specs=spec,
        out_shape=jax.ShapeDtypeStruct((t, d), F32), compiler_params=_params("parallel"),
        name="moe_combine",
    )(x, a, b)


def _moe(x, g, w_router, b_router, wg, wu, wd, tm, bm, tf):
    h, top_idx, top_w = _router(x, g, w_router, b_router, tm)
    row_tok, row_w, dest, block_exp, n_used = _route(top_idx, top_w, bm)
    xs = jnp.take(h, row_tok, axis=0)
    ys = _expert_ffn(xs, row_w, block_exp, n_used, wg, wu, wd, bm, tf)
    return _add3(x, jnp.take(ys, dest[:, 0], axis=0), jnp.take(ys, dest[:, 1], axis=0), tm)


def _final_norm_kernel(x_ref, g_ref, o_ref):
    o_ref[...] = _rms(x_ref[...], NORM_EPS) * g_ref[...]


def _final_norm(x, g, tm):
    t, d = x.shape
    return pl.pallas_call(
        _final_norm_kernel, grid=(t // tm,),
        in_specs=[pl.BlockSpec((tm, d), lambda i: (i, 0)), pl.BlockSpec((1, d), lambda i: (0, 0))],
        out_specs=pl.BlockSpec((tm, d), lambda i: (i, 0)),
        out_shape=jax.ShapeDtypeStruct((t, d), F32), compiler_params=_params("parallel"),
        name="final_norm",
    )(x, g.reshape(1, d))


def _tiles(t, s):
    pick = lambda n, pref: pref if n % pref == 0 else n
    return dict(tm_proj=pick(t, 1024), tn_proj=1024, tq=pick(s, 256), tk=pick(s, 512), ts_conv=pick(s, 512),
                tm_merge=pick(t, 512), tm_ffn=pick(t, 512), tm_row=pick(t, 1024), bm_moe=pick(t, 1024))


def kernel(x, ln_mix_w, w_in, attn_lq1, attn_lk1, attn_lq2, attn_lk2, attn_subln_w, attn_w_o, conv_dw_w, conv_dw_b, conv_ln_w, conv_ln_b, conv_w_pw, ssm_conv_w, ssm_conv_b, ssm_dt_bias, ssm_A_log, ssm_D, ssm_norm_w, ssm_w_out, w_mix_out, ln_ffn_w, ffn_w_gate, ffn_w_up, ffn_w_down, moe_w_router, moe_b_router, moe_w_gate, moe_w_up, moe_w_down, ln_final_w):
    b, s, d = x.shape
    t = b * s
    depth = w_in.shape[0]
    tl = _tiles(t, s)
    xf = x.reshape(t, d)
    for l in range(depth):
        w_main = jnp.concatenate([w_in[l][:, :COL_DT], w_in[l][:, COL_DT + SSM_HEADS:]], axis=1).astype(BF16)
        w_dt = jnp.pad(w_in[l][:, COL_DT:COL_DT + SSM_HEADS], ((0, 0), (0, LANES - SSM_HEADS))).astype(BF16)
        proj = _norm_matmul(xf, ln_mix_w[l], w_main, BF16, tl["tm_proj"], tl["tn_proj"])
        dt = _norm_matmul(xf, ln_mix_w[l], w_dt, F32, tl["tm_proj"], LANES)
        proj3 = proj.reshape(b, s, PROJ_COLS)

        lam_init = 0.8 - 0.6 * math.exp(-0.3 * l)
        lam = (jnp.exp(jnp.sum(attn_lq1[l].astype(F32) * attn_lk1[l].astype(F32)))
               - jnp.exp(jnp.sum(attn_lq2[l].astype(F32) * attn_lk2[l].astype(F32))) + lam_init)
        y_att = _diff_attention(proj3, lam, attn_subln_w[l], lam_init, tl["tq"], tl["tk"])
        y_conv = _conformer_conv(proj3, conv_dw_w[l], conv_dw_b[l], conv_ln_w[l], conv_ln_b[l], tl["ts_conv"])
        y_ssm = _mamba2_ssd(proj3, dt.reshape(b, s, LANES), ssm_conv_w[l], ssm_conv_b[l], ssm_dt_bias[l],
                            ssm_A_log[l], ssm_D[l], ssm_norm_w[l])
        xf = _merge(xf, y_att.reshape(t, ATT_WIDTH), y_conv.reshape(t, CONV_CH), y_ssm.reshape(t, SSM_D_INNER),
                    proj, attn_w_o[l].astype(BF16), conv_w_pw[l].astype(BF16), ssm_w_out[l].astype(BF16),
                    w_mix_out[l].astype(BF16), tl["tm_merge"])

        i = l // 2
        if l % 2 == 0:
            f = ffn_w_gate.shape[2]
            tf = f // 2 if (f // 2) % LANES == 0 else f
            xf = _ffn_dense(xf, ln_ffn_w[l], ffn_w_gate[i].astype(BF16), ffn_w_up[i].astype(BF16),
                            ffn_w_down[i].astype(BF16), tl["tm_ffn"], tf)
        else:
            f = moe_w_gate.shape[3]
            tf = 512 if f % 512 == 0 else f
            xf = _moe(xf, ln_ffn_w[l], moe_w_router[i], moe_b_router[i], moe_w_gate[i].astype(BF16),
                      moe_w_up[i].astype(BF16), moe_w_down[i].astype(BF16), tl["tm_row"], tl["bm_moe"], tf)
    return _final_norm(xf, ln_final_w, tl["tm_row"]).reshape(b, s, d)
```

```python
import functools
import math

import jax
import jax.numpy as jnp
from jax import lax
from jax.experimental import pallas as pl
from jax.experimental.pallas import tpu as pltpu

F32 = jnp.float32
BF16 = jnp.bfloat16
HIGHEST = lax.Precision.HIGHEST

D_MODEL = 1024
ATT_HEADS = 4
ATT_DH = 64
ATT_WIDTH = ATT_HEADS * 2 * ATT_DH
CONV_CH = 512
CONV_K = 31
SSM_D_INNER = 512
SSM_HEADDIM = 64
SSM_HEADS = SSM_D_INNER // SSM_HEADDIM
SSM_GROUPS = 2
SSM_STATE = 128
SSM_CONV_K = 4
SSM_CONV_DIM = SSM_D_INNER + 2 * SSM_GROUPS * SSM_STATE
SSM_CHUNK = 128
N_EXPERTS = 8
TOP_K = 2
NORM_EPS = 1e-6

LANES = 128
SUBLANES = 8
VMEM_LIMIT_BYTES = 48 * 1024 * 1024

COL_Q = 0
COL_K = COL_Q + ATT_WIDTH
COL_V = COL_K + ATT_WIDTH
COL_U = COL_V + ATT_WIDTH
COL_Z = COL_U + 2 * CONV_CH
COL_XBC = COL_Z + SSM_D_INNER
COL_DT = COL_XBC + SSM_CONV_DIM
COL_G = COL_XBC + SSM_CONV_DIM
PROJ_COLS = COL_G + 3 * D_MODEL


def _params(*semantics):
    return pltpu.CompilerParams(dimension_semantics=semantics, vmem_limit_bytes=VMEM_LIMIT_BYTES)


def _sigmoid(x):
    return 1.0 / (1.0 + jnp.exp(-x))


def _rms(x, eps):
    return x * lax.rsqrt(jnp.mean(x * x, axis=-1, keepdims=True) + eps)


def _norm_matmul_kernel(x_ref, g_ref, w_ref, o_ref, h_ref):
    @pl.when(pl.program_id(1) == 0)
    def _():
        h_ref[...] = (_rms(x_ref[...], NORM_EPS) * g_ref[...]).astype(h_ref.dtype)

    o_ref[...] = jnp.dot(h_ref[...], w_ref[...], preferred_element_type=F32).astype(o_ref.dtype)


def _norm_matmul(x, g, w, out_dtype, tm, tn):
    t, d = x.shape
    n = w.shape[1]
    return pl.pallas_call(
        _norm_matmul_kernel,
        grid=(t // tm, n // tn),
        in_specs=[pl.BlockSpec((tm, d), lambda i, j: (i, 0)),
                  pl.BlockSpec((1, d), lambda i, j: (0, 0)),
                  pl.BlockSpec((d, tn), lambda i, j: (0, j))],
        out_specs=pl.BlockSpec((tm, tn), lambda i, j: (i, j)),
        out_shape=jax.ShapeDtypeStruct((t, n), out_dtype),
        scratch_shapes=[pltpu.VMEM((tm, d), BF16)],
        compiler_params=_params("parallel", "arbitrary"),
        name="norm_matmul",
    )(x, g.reshape(1, d), w)


ATT_ONES = 16
LOG2E = 1.4426950408889634


def _attn_kernel(items_ref, lam_ref, q_ref, k_ref, v_ref, sw_ref, o_ref,
                 vt_ref, q2_ref, m_ref, acc_ref, sa_ref, ca_ref, sb_ref, cb_ref, *, tb, n_off, n_diag, out_scale):
    hw = 2 * ATT_DH
    nb = vt_ref.shape[0]

    lane = lax.broadcasted_iota(jnp.int32, (tb, hw), 1)
    for c in range(nb):
        rows = slice(c * tb, (c + 1) * tb)
        vt_ref[c, 0:hw, :] = v_ref[rows, :].astype(F32).T.astype(BF16)
        vt_ref[c, hw:hw + ATT_ONES, :] = jnp.ones((ATT_ONES, tb), BF16)
        q = (q_ref[rows, :].astype(F32) * (ATT_DH ** -0.5 * LOG2E)).astype(BF16)
        zero = jnp.zeros_like(q)
        q2_ref[c, 0:tb, :] = jnp.where(lane < ATT_DH, q, zero)
        q2_ref[c, tb:2 * tb, :] = jnp.where(lane >= ATT_DH, q, zero)
        m_ref[c] = jnp.full(m_ref.shape[1:], -jnp.inf, F32)
        acc_ref[c] = jnp.zeros(acc_ref.shape[1:], F32)

    def fill(buf, t, masked):
        s_ref, cmax_ref = buf
        qi, j = items_ref[0, t], items_ref[1, t]
        k = k_ref[pl.ds(pl.multiple_of(j * tb, tb), tb), :]
        st = lax.dot_general(k, q2_ref[qi], (((1,), (1,)), ((), ())), preferred_element_type=F32)
        if masked:
            r = lax.broadcasted_iota(jnp.int32, st.shape, 0)
            c = lax.broadcasted_iota(jnp.int32, st.shape, 1)
            st = jnp.where(r <= jnp.where(c >= tb, c - tb, c), st, -jnp.inf)
        s_ref[...] = st
        cmax_ref[...] = jnp.max(st, axis=0, keepdims=True)

    def update(buf, t, last):
        s_ref, cmax_ref = buf
        qi, j = items_ref[0, t], items_ref[1, t]
        m_old = m_ref[qi]
        m_new = jnp.maximum(m_old, cmax_ref[...])
        p = jnp.exp2(s_ref[...] - m_new).astype(BF16)
        acc = jnp.exp2(m_old - m_new) * acc_ref[qi] + jnp.dot(vt_ref[j], p, preferred_element_type=F32)
        if not last:
            acc_ref[qi] = acc
            m_ref[qi] = m_new
        else:
            o_t = acc[:hw] * (1.0 / acc[hw:hw + 1])
            o = (o_t[:, :tb] - lam_ref[0] * o_t[:, tb:]).T
            o_ref[pl.ds(pl.multiple_of(qi * tb, tb), tb), :] = (
                _rms(o, 1e-5) * (sw_ref[...] * out_scale)).astype(o_ref.dtype)

    def run(first, count, diag, cur, other):
        trips = (count - 1) // 2

        def body(i, carry):
            t = first + 2 * i
            fill(other, t + 1, diag)
            update(cur, t, diag)
            fill(cur, t + 2, diag)
            update(other, t + 1, diag)
            return carry

        lax.fori_loop(0, trips, body, 0)
        if count - 1 > 2 * trips:
            t = first + 2 * trips
            fill(other, t + 1, diag)
            update(cur, t, diag)
            cur, other = other, cur
        return cur, other

    cur, other = (sa_ref, ca_ref), (sb_ref, cb_ref)
    if n_off:
        fill(cur, 0, False)
        cur, other = run(0, n_off, False, cur, other)
        fill(other, n_off, True)
        update(cur, n_off - 1, False)
        cur, other = other, cur
    else:
        fill(cur, 0, True)
    cur, other = run(n_off, n_diag, True, cur, other)
    update(cur, n_off + n_diag - 1, True)


def _diff_attention(proj3, lam, subln_w, lam_init, tb):
    b, s, _ = proj3.shape
    hw = 2 * ATT_DH
    nb = s // tb
    off = [(qi, j) for qi in range(nb) for j in range(qi)]
    items = jnp.asarray(list(zip(*(off + [(qi, qi) for qi in range(nb)]))), jnp.int32)
    kern = functools.partial(_attn_kernel, tb=tb, n_off=len(off), n_diag=nb, out_scale=1.0 - lam_init)
    seq = lambda col: pl.BlockSpec((None, s, hw), lambda bb, h: (bb, 0, col // hw + h))
    return pl.pallas_call(
        kern,
        grid=(b, ATT_HEADS),
        in_specs=[pl.BlockSpec(memory_space=pltpu.SMEM), pl.BlockSpec(memory_space=pltpu.SMEM),
                  seq(COL_Q), seq(COL_K), seq(COL_V), pl.BlockSpec((1, hw), lambda bb, h: (0, 0))],
        out_specs=pl.BlockSpec((None, s, hw), lambda bb, h: (bb, 0, h)),
        out_shape=jax.ShapeDtypeStruct((b, s, ATT_WIDTH), BF16),
        scratch_shapes=[pltpu.VMEM((nb, hw + ATT_ONES, tb), BF16), pltpu.VMEM((nb, 2 * tb, hw), BF16),
                        pltpu.VMEM((nb, 1, 2 * tb), F32), pltpu.VMEM((nb, hw + ATT_ONES, 2 * tb), F32),
                        pltpu.VMEM((tb, 2 * tb), F32), pltpu.VMEM((1, 2 * tb), F32),
                        pltpu.VMEM((tb, 2 * tb), F32), pltpu.VMEM((1, 2 * tb), F32)],
        compiler_params=_params("parallel", "parallel"),
        name="diff_attention",
    )(items, lam.reshape(1).astype(F32), proj3, proj3, proj3, subln_w.reshape(1, hw).astype(F32))


CONV_HALO = 32
CONV_ROWS = 64


def _cconv_kernel(a_ref, g_ref, w_ref, b_ref, lnw_ref, lnb_ref, o_ref, buf_ref, *, ts):
    i = pl.program_id(1)

    @pl.when(i == 0)
    def _():
        buf_ref[0:CONV_HALO, :] = jnp.zeros((CONV_HALO, CONV_CH), F32)

    @pl.when(i > 0)
    def _():
        buf_ref[0:CONV_HALO, :] = buf_ref[ts:ts + CONV_HALO, :]

    buf_ref[CONV_HALO:CONV_HALO + ts, :] = a_ref[...].astype(F32) * _sigmoid(g_ref[...].astype(F32))

    first = CONV_HALO - (CONV_K - 1)
    for c in range(ts // CONV_ROWS):
        base = c * CONV_ROWS
        acc = jnp.zeros((CONV_ROWS, CONV_CH), F32) + b_ref[...]
        for k in range(CONV_K):
            acc = acc + w_ref[k:k + 1, :] * buf_ref[base + first + k:base + first + k + CONV_ROWS, :]
        mu = jnp.mean(acc, axis=-1, keepdims=True)
        xc = acc - mu
        var = jnp.mean(xc * xc, axis=-1, keepdims=True)
        y = xc * lax.rsqrt(var + 1e-5) * lnw_ref[...] + lnb_ref[...]
        o_ref[base:base + CONV_ROWS, :] = (y * _sigmoid(y)).astype(o_ref.dtype)


def _conformer_conv(proj3, dw_w, dw_b, ln_w, ln_b, ts):
    b, s, _ = proj3.shape
    c = CONV_CH
    row = lambda a: a.reshape(1, c).astype(F32)
    return pl.pallas_call(
        functools.partial(_cconv_kernel, ts=ts),
        grid=(b, s // ts),
        in_specs=[pl.BlockSpec((None, ts, c), lambda bb, i: (bb, i, COL_U // c)),
                  pl.BlockSpec((None, ts, c), lambda bb, i: (bb, i, COL_U // c + 1)),
                  pl.BlockSpec((CONV_K, c), lambda bb, i: (0, 0)),
                  pl.BlockSpec((1, c), lambda bb, i: (0, 0)),
                  pl.BlockSpec((1, c), lambda bb, i: (0, 0)),
                  pl.BlockSpec((1, c), lambda bb, i: (0, 0))],
        out_specs=pl.BlockSpec((None, ts, c), lambda bb, i: (bb, i, 0)),
        out_shape=jax.ShapeDtypeStruct((b, s, c), BF16),
        scratch_shapes=[pltpu.VMEM((CONV_HALO + ts, c), F32)],
        compiler_params=_params("parallel", "arbitrary"),
        name="conformer_conv",
    )(proj3, proj3, dw_w.astype(F32), row(dw_b), row(ln_w), row(ln_b))


SSD_HALO = 8


def _ssd_kernel(z_ref, xbc_ref, dt_ref, cw_ref, cb_ref, dtb_ref, a_ref, dsk_ref, nw_ref, e_ref,
                o_ref, buf_ref, h_ref, *, chunk):
    L = chunk
    ci = pl.program_id(1)
    gw = SSM_D_INNER // SSM_GROUPS
    hpg = SSM_HEADS // SSM_GROUPS

    @pl.when(ci == 0)
    def _():
        buf_ref[0:SSD_HALO, :] = jnp.zeros((SSD_HALO, SSM_CONV_DIM), F32)
        h_ref[...] = jnp.zeros(h_ref.shape, F32)

    @pl.when(ci > 0)
    def _():
        buf_ref[0:SSD_HALO, :] = buf_ref[L:L + SSD_HALO, :]

    buf_ref[SSD_HALO:SSD_HALO + L, :] = xbc_ref[...].astype(F32)
    first = SSD_HALO - (SSM_CONV_K - 1)
    xc = jnp.zeros((L, SSM_CONV_DIM), F32) + cb_ref[...]
    for k in range(SSM_CONV_K):
        xc = xc + cw_ref[k:k + 1, :] * buf_ref[first + k:first + k + L, :]
    xc = xc * _sigmoid(xc)
    xs = xc[:, :SSM_D_INNER]
    bm = xc[:, SSM_D_INNER:SSM_D_INNER + SSM_GROUPS * SSM_STATE]
    cm = xc[:, SSM_D_INNER + SSM_GROUPS * SSM_STATE:]

    dtr = dt_ref[...] + dtb_ref[...]
    dt = jnp.maximum(dtr, 0.0) + jnp.log(1.0 + jnp.exp(-jnp.abs(dtr)))
    da = dt * a_ref[...]
    row = lax.broadcasted_iota(jnp.int32, (L, L), 0)
    col = lax.broadcasted_iota(jnp.int32, (L, L), 1)
    causal = col <= row
    a_cum = jnp.dot(causal.astype(F32), da, precision=HIGHEST, preferred_element_type=F32)
    a_cum_t = jnp.dot(da.T, (row <= col).astype(F32), precision=HIGHEST, preferred_element_type=F32)

    expand = e_ref[...]
    dt_full = jnp.dot(dt, expand, precision=HIGHEST, preferred_element_type=F32)
    ea_full = jnp.dot(jnp.exp(a_cum), expand, precision=HIGHEST, preferred_element_type=F32)
    dend_full = jnp.dot(jnp.exp(a_cum[L - 1:L, :] - a_cum), expand, precision=HIGHEST,
                        preferred_element_type=F32)
    cdec_full = ea_full[L - 1:L, :]

    xdt = xs * dt_full
    xdt_b = xdt.astype(BF16)
    xw_b = (xdt * dend_full).astype(BF16)

    y_parts = []
    for g in range(SSM_GROUPS):
        cg = cm[:, g * SSM_STATE:(g + 1) * SSM_STATE].astype(BF16)
        bg_t = bm[:, g * SSM_STATE:(g + 1) * SSM_STATE].T.astype(BF16)
        cbm = jnp.dot(cg, bg_t, preferred_element_type=F32)
        h_in = h_ref[:, g * gw:(g + 1) * gw]
        y_off = jnp.dot(cg, h_in.astype(BF16), preferred_element_type=F32)
        states = jnp.dot(bg_t, xw_b[:, g * gw:(g + 1) * gw], preferred_element_type=F32)
        h_ref[:, g * gw:(g + 1) * gw] = h_in * cdec_full[:, g * gw:(g + 1) * gw] + states
        yd = []
        for r in range(hpg):
            hh = g * hpg + r
            seg = a_cum[:, hh:hh + 1] - a_cum_t[hh:hh + 1, :]
            dec = jnp.exp(jnp.where(causal, seg, -jnp.inf))
            yd.append(jnp.dot((cbm * dec).astype(BF16), xdt_b[:, hh * SSM_HEADDIM:(hh + 1) * SSM_HEADDIM],
                              preferred_element_type=F32))
        y_g = (jnp.concatenate(yd, axis=-1) + y_off * ea_full[:, g * gw:(g + 1) * gw]
               + xs[:, g * gw:(g + 1) * gw] * dsk_ref[:, g * gw:(g + 1) * gw])
        zg = z_ref[:, g * gw:(g + 1) * gw].astype(F32)
        y_parts.append(_rms(y_g * (zg * _sigmoid(zg)), 1e-5))
    o_ref[...] = (jnp.concatenate(y_parts, axis=-1) * nw_ref[...]).astype(o_ref.dtype)


def _mamba2_ssd(proj3, dt3, conv_w, conv_b, dt_bias, a_log, d_skip, norm_w):
    b, s, _ = proj3.shape
    L = SSM_CHUNK
    di, cd = SSM_D_INNER, SSM_CONV_DIM
    pad_h = lambda a: jnp.pad(a.astype(F32), (0, LANES - SSM_HEADS)).reshape(1, LANES)
    head_of_channel = jnp.arange(di, dtype=jnp.int32) // SSM_HEADDIM
    expand = (jnp.arange(LANES, dtype=jnp.int32)[:, None] == head_of_channel[None, :]).astype(F32)
    const = lambda shape: pl.BlockSpec(shape, lambda bb, c: (0, 0))
    return pl.pallas_call(
        functools.partial(_ssd_kernel, chunk=L),
        grid=(b, s // L),
        in_specs=[pl.BlockSpec((None, L, di), lambda bb, c: (bb, c, COL_Z // di)),
                  pl.BlockSpec((None, L, cd), lambda bb, c: (bb, c, COL_XBC // cd)),
                  pl.BlockSpec((None, L, LANES), lambda bb, c: (bb, c, 0)),
                  const((SSM_CONV_K, cd)), const((1, cd)), const((1, LANES)), const((1, LANES)),
                  const((1, di)), const((1, di)), const((LANES, di))],
        out_specs=pl.BlockSpec((None, L, di), lambda bb, c: (bb, c, 0)),
        out_shape=jax.ShapeDtypeStruct((b, s, di), BF16),
        scratch_shapes=[pltpu.VMEM((SSD_HALO + L, cd), F32), pltpu.VMEM((SSM_STATE, di), F32)],
        compiler_params=_params("parallel", "arbitrary"),
        name="mamba2_ssd",
    )(proj3, proj3, dt3, conv_w.astype(F32), conv_b.reshape(1, cd).astype(F32), pad_h(dt_bias),
      pad_h(-jnp.exp(a_log.astype(F32))), jnp.repeat(d_skip.astype(F32), SSM_HEADDIM).reshape(1, di),
      norm_w.reshape(1, di).astype(F32), expand)


def _merge_kernel(x_ref, ya_ref, yc_ref, ys_ref, ga_ref, gc_ref, gs_ref, wa_ref, wc_ref, ws_ref, wm_ref, o_ref):
    def branch(y_ref, g_ref, w_ref):
        return _sigmoid(g_ref[...].astype(F32)) * jnp.dot(y_ref[...], w_ref[...], preferred_element_type=F32)

    merged = branch(ya_ref, ga_ref, wa_ref) + branch(yc_ref, gc_ref, wc_ref) + branch(ys_ref, gs_ref, ws_ref)
    o_ref[...] = x_ref[...] + jnp.dot(merged.astype(BF16), wm_ref[...], preferred_element_type=F32)


def _merge(x, ya, yc, ys, proj, wa, wc, ws, wm, tm):
    t, d = x.shape
    w = ya.shape[1]
    rows = lambda width, cb: pl.BlockSpec((tm, width), lambda i: (i, cb))
    const = lambda shape: pl.BlockSpec(shape, lambda i: (0, 0))
    return pl.pallas_call(
        _merge_kernel,
        grid=(t // tm,),
        in_specs=[rows(d, 0), rows(w, 0), rows(w, 0), rows(w, 0),
                  rows(d, COL_G // d), rows(d, COL_G // d + 1), rows(d, COL_G // d + 2),
                  const((w, d)), const((w, d)), const((w, d)), const((d, d))],
        out_specs=rows(d, 0),
        out_shape=jax.ShapeDtypeStruct((t, d), F32),
        compiler_params=_params("parallel"),
        name="gated_merge",
    )(x, ya, yc, ys, proj, proj, proj, wa, wc, ws, wm)


def _ffn_kernel(x_ref, g_ref, wg_ref, wu_ref, wd_ref, o_ref, h_ref, acc_ref):
    j = pl.program_id(1)

    @pl.when(j == 0)
    def _():
        h_ref[...] = (_rms(x_ref[...], NORM_EPS) * g_ref[...]).astype(h_ref.dtype)
        acc_ref[...] = jnp.zeros(acc_ref.shape, F32)

    h = h_ref[...]
    a = jnp.dot(h, wg_ref[...], preferred_element_type=F32)
    u = jnp.dot(h, wu_ref[...], preferred_element_type=F32)
    acc_ref[...] += jnp.dot((a * _sigmoid(a) * u).astype(BF16), wd_ref[...], preferred_element_type=F32)

    @pl.when(j == pl.num_programs(1) - 1)
    def _():
        o_ref[...] = x_ref[...] + acc_ref[...]


def _ffn_dense(x, g, wg, wu, wd, tm, tf):
    t, d = x.shape
    f = wg.shape[1]
    return pl.pallas_call(
        _ffn_kernel,
        grid=(t // tm, f // tf),
        in_specs=[pl.BlockSpec((tm, d), lambda i, j: (i, 0)),
                  pl.BlockSpec((1, d), lambda i, j: (0, 0)),
                  pl.BlockSpec((d, tf), lambda i, j: (0, j)),
                  pl.BlockSpec((d, tf), lambda i, j: (0, j)),
                  pl.BlockSpec((tf, d), lambda i, j: (j, 0))],
        out_specs=pl.BlockSpec((tm, d), lambda i, j: (i, 0)),
        out_shape=jax.ShapeDtypeStruct((t, d), F32),
        scratch_shapes=[pltpu.VMEM((tm, d), BF16), pltpu.VMEM((tm, d), F32)],
        compiler_params=_params("parallel", "arbitrary"),
        name="ffn_dense",
    )(x, g.reshape(1, d), wg, wu, wd)


def _router_kernel(x_ref, g_ref, wr_ref, br_ref, h_ref, idx_ref, wt_ref):
    h = _rms(x_ref[...], NORM_EPS) * g_ref[...]
    h_ref[...] = h.astype(h_ref.dtype)
    logits = jnp.dot(h, wr_ref[...], precision=HIGHEST, preferred_element_type=F32) + br_ref[...]
    lane = lax.broadcasted_iota(jnp.int32, logits.shape, 1)
    logits = jnp.where(lane < N_EXPERTS, logits, -jnp.inf)
    m1 = jnp.max(logits, axis=-1, keepdims=True)
    i1 = jnp.min(jnp.where(logits == m1, lane, LANES), axis=-1, keepdims=True)
    rest = jnp.where(lane == i1, -jnp.inf, logits)
    m2 = jnp.max(rest, axis=-1, keepdims=True)
    i2 = jnp.min(jnp.where(rest == m2, lane, LANES), axis=-1, keepdims=True)
    e = jnp.exp(m2 - m1)
    w1 = 1.0 / (1.0 + e)
    idx_ref[...] = jnp.where(lane == 0, i1, i2)[:, :TOP_K]
    wt_ref[...] = jnp.where(lane == 0, w1, e * w1)[:, :TOP_K]


def _router(x, g, w_router, b_router, tm):
    t, d = x.shape
    wr = jnp.pad(w_router.astype(F32), ((0, 0), (0, LANES - N_EXPERTS)))
    br = jnp.pad(b_router.astype(F32), (0, LANES - N_EXPERTS)).reshape(1, LANES)
    return pl.pallas_call(
        _router_kernel,
        grid=(t // tm,),
        in_specs=[pl.BlockSpec((tm, d), lambda i: (i, 0)),
                  pl.BlockSpec((1, d), lambda i: (0, 0)),
                  pl.BlockSpec((d, LANES), lambda i: (0, 0)),
                  pl.BlockSpec((1, LANES), lambda i: (0, 0))],
        out_specs=[pl.BlockSpec((tm, d), lambda i: (i, 0)),
                   pl.BlockSpec((tm, TOP_K), lambda i: (i, 0)),
                   pl.BlockSpec((tm, TOP_K), lambda i: (i, 0))],
        out_shape=[jax.ShapeDtypeStruct((t, d), BF16),
                   jax.ShapeDtypeStruct((t, TOP_K), jnp.int32),
                   jax.ShapeDtypeStruct((t, TOP_K), F32)],
        compiler_params=_params("parallel"),
        name="moe_router",
    )(x, g.reshape(1, d), wr, br)


def _expert_kernel(be_ref, nu_ref, x_ref, rw_ref, wg_ref, wu_ref, wd_ref, o_ref, acc_ref):
    i = pl.program_id(0)
    j = pl.program_id(1)
    last = pl.num_programs(1) - 1
    used = i < nu_ref[0]

    @pl.when(jnp.logical_and(used, j == 0))
    def _():
        acc_ref[...] = jnp.zeros(acc_ref.shape, F32)

    @pl.when(used)
    def _():
        x = x_ref[...]
        a = jnp.dot(x, wg_ref[...], preferred_element_type=F32)
        u = jnp.dot(x, wu_ref[...], preferred_element_type=F32)
        acc_ref[...] += jnp.dot((a * _sigmoid(a) * u).astype(BF16), wd_ref[...], preferred_element_type=F32)

    @pl.when(jnp.logical_and(used, j == last))
    def _():
        o_ref[...] = acc_ref[...] * rw_ref[...]

    @pl.when(jnp.logical_and(jnp.logical_not(used), j == last))
    def _():
        o_ref[...] = jnp.zeros(o_ref.shape, o_ref.dtype)


def _expert_ffn(xs, row_w, block_exp, n_used, wg, wu, wd, bm, tf):
    n_rows, d = xs.shape
    f = wg.shape[2]
    nj = f // tf

    def jj(i, j, nu):
        return jnp.where(i < nu[0], j, nj - 1)

    grid_spec = pltpu.PrefetchScalarGridSpec(
        num_scalar_prefetch=2,
        grid=(n_rows // bm, nj),
        in_specs=[pl.BlockSpec((bm, d), lambda i, j, be, nu: (i, 0)),
                  pl.BlockSpec((bm, 1), lambda i, j, be, nu: (i, 0)),
                  pl.BlockSpec((None, d, tf), lambda i, j, be, nu: (be[i], 0, jj(i, j, nu))),
                  pl.BlockSpec((None, d, tf), lambda i, j, be, nu: (be[i], 0, jj(i, j, nu))),
                  pl.BlockSpec((None, tf, d), lambda i, j, be, nu: (be[i], jj(i, j, nu), 0))],
        out_specs=pl.BlockSpec((bm, d), lambda i, j, be, nu: (i, 0)),
        scratch_shapes=[pltpu.VMEM((bm, d), F32)],
    )
    return pl.pallas_call(
        _expert_kernel,
        grid_spec=grid_spec,
        out_shape=jax.ShapeDtypeStruct((n_rows, d), F32),
        compiler_params=_params("parallel", "arbitrary"),
        name="moe_experts",
    )(block_exp, n_used, xs, row_w.reshape(n_rows, 1), wg, wu, wd)


def _route(top_idx, top_w, bm):
    t = top_idx.shape[0]
    tk = t * TOP_K
    flat_e = top_idx.reshape(tk)
    flat_tok = jnp.repeat(jnp.arange(t, dtype=jnp.int32), TOP_K)
    order = jnp.argsort(flat_e)
    se = flat_e[order]
    counts = jnp.bincount(flat_e, length=N_EXPERTS).astype(jnp.int32)
    padded = (counts + bm - 1) // bm * bm
    start_sorted = jnp.cumsum(counts) - counts
    end_padded = jnp.cumsum(padded)
    start_padded = end_padded - padded
    dest_sorted = start_padded[se] + (jnp.arange(tk, dtype=jnp.int32) - start_sorted[se])
    n_blocks = tk // bm + N_EXPERTS
    n_rows = n_blocks * bm
    row_tok = jnp.zeros((n_rows,), jnp.int32).at[dest_sorted].set(flat_tok[order])
    row_w = jnp.zeros((n_rows,), F32).at[dest_sorted].set(top_w.reshape(tk)[order])
    dest = jnp.zeros((tk,), jnp.int32).at[order].set(dest_sorted).reshape(t, TOP_K)
    block_start = jnp.arange(n_blocks, dtype=jnp.int32) * bm
    block_exp = jnp.minimum(jnp.sum(block_start[:, None] >= end_padded[None, :], axis=1), N_EXPERTS - 1)
    n_used = (end_padded[-1] // bm).reshape(1)
    return row_tok, row_w, dest, block_exp.astype(jnp.int32), n_used.astype(jnp.int32)


def _add3_kernel(x_ref, a_ref, b_ref, o_ref):
    o_ref[...] = x_ref[...] + a_ref[...] + b_ref[...]


def _add3(x, a, b, tm):
    t, d = x.shape
    spec = pl.BlockSpec((tm, d), lambda i: (i, 0))
    return pl.pallas_call(
        _add3_kernel, grid=(t // tm,), in_specs=[spec, spec, spec], out_specs=spec,
        out_shape=jax.ShapeDtypeStruct((t, d), F32), compiler_params=_params("parallel"),
        name="moe_combine",
    )(x, a, b)


def _moe(x, g, w_router, b_router, wg, wu, wd, tm, bm, tf):
    h, top_idx, top_w = _router(x, g, w_router, b_router, tm)
    row_tok, row_w, dest, block_exp, n_used = _route(top_idx, top_w, bm)
    xs = jnp.take(h, row_tok, axis=0)
    ys = _expert_ffn(xs, row_w, block_exp, n_used, wg, wu, wd, bm, tf)
    return _add3(x, jnp.take(ys, dest[:, 0], axis=0), jnp.take(ys, dest[:, 1], axis=0), tm)


def _final_norm_kernel(x_ref, g_ref, o_ref):
    o_ref[...] = _rms(x_ref[...], NORM_EPS) * g_ref[...]


def _final_norm(x, g, tm):
    t, d = x.shape
    return pl.pallas_call(
        _final_norm_kernel, grid=(t // tm,),
        in_specs=[pl.BlockSpec((tm, d), lambda i: (i, 0)), pl.BlockSpec((1, d), lambda i: (0, 0))],
        out_specs=pl.BlockSpec((tm, d), lambda i: (i, 0)),
        out_shape=jax.ShapeDtypeStruct((t, d), F32), compiler_params=_params("parallel"),
        name="final_norm",
    )(x, g.reshape(1, d))


def _tiles(t, s):
    pick = lambda n, pref: pref if n % pref == 0 else n
    return dict(tm_proj=pick(t, 1024), tn_proj=1024, tb_attn=pick(s, 512), ts_conv=pick(s, 512),
                tm_merge=pick(t, 512), tm_ffn=pick(t, 512), tm_row=pick(t, 1024), bm_moe=pick(t, 1024))


def kernel(x, ln_mix_w, w_in, attn_lq1, attn_lk1, attn_lq2, attn_lk2, attn_subln_w, attn_w_o, conv_dw_w, conv_dw_b, conv_ln_w, conv_ln_b, conv_w_pw, ssm_conv_w, ssm_conv_b, ssm_dt_bias, ssm_A_log, ssm_D, ssm_norm_w, ssm_w_out, w_mix_out, ln_ffn_w, ffn_w_gate, ffn_w_up, ffn_w_down, moe_w_router, moe_b_router, moe_w_gate, moe_w_up, moe_w_down, ln_final_w):
    b, s, d = x.shape
    t = b * s
    depth = w_in.shape[0]
    tl = _tiles(t, s)
    xf = x.reshape(t, d)
    for l in range(depth):
        w_main = jnp.concatenate([w_in[l][:, :COL_DT], w_in[l][:, COL_DT + SSM_HEADS:]], axis=1).astype(BF16)
        w_dt = jnp.pad(w_in[l][:, COL_DT:COL_DT + SSM_HEADS], ((0, 0), (0, LANES - SSM_HEADS))).astype(BF16)
        proj = _norm_matmul(xf, ln_mix_w[l], w_main, BF16, tl["tm_proj"], tl["tn_proj"])
        dt = _norm_matmul(xf, ln_mix_w[l], w_dt, F32, tl["tm_proj"], LANES)
        proj3 = proj.reshape(b, s, PROJ_COLS)

        lam_init = 0.8 - 0.6 * math.exp(-0.3 * l)
        lam = (jnp.exp(jnp.sum(attn_lq1[l].astype(F32) * attn_lk1[l].astype(F32)))
               - jnp.exp(jnp.sum(attn_lq2[l].astype(F32) * attn_lk2[l].astype(F32))) + lam_init)
        y_att = _diff_attention(proj3, lam, attn_subln_w[l], lam_init, tl["tb_attn"])
        y_conv = _conformer_conv(proj3, conv_dw_w[l], conv_dw_b[l], conv_ln_w[l], conv_ln_b[l], tl["ts_conv"])
        y_ssm = _mamba2_ssd(proj3, dt.reshape(b, s, LANES), ssm_conv_w[l], ssm_conv_b[l], ssm_dt_bias[l],
                            ssm_A_log[l], ssm_D[l], ssm_norm_w[l])
        xf = _merge(xf, y_att.reshape(t, ATT_WIDTH), y_conv.reshape(t, CONV_CH), y_ssm.reshape(t, SSM_D_INNER),
                    proj, attn_w_o[l].astype(BF16), conv_w_pw[l].astype(BF16), ssm_w_out[l].astype(BF16),
                    w_mix_out[l].astype(BF16), tl["tm_merge"])

        i = l // 2
        if l % 2 == 0:
            f = ffn_w_gate.shape[2]
            tf = f // 2 if (f // 2) % LANES == 0 else f
            xf = _ffn_dense(xf, ln_ffn_w[l], ffn_w_gate[i].astype(BF16), ffn_w_up[i].astype(BF16),
                            ffn_w_down[i].astype(BF16), tl["tm_ffn"], tf)
        else:
            f = moe_w_gate.shape[3]
            tf = 512 if f % 512 == 0 else f
            xf = _moe(xf, ln_ffn_w[l], moe_w_router[i], moe_b_router[i], moe_w_gate[i].astype(BF16),
                      moe_w_up[i].astype(BF16), moe_w_down[i].astype(BF16), tl["tm_row"], tl["bm_moe"], tf)
    return _final_norm(xf, ln_final_w, tl["tm_row"]).reshape(b, s, d)
```

```python
import functools
import math

import jax
import jax.numpy as jnp
from jax import lax
from jax.experimental import pallas as pl
from jax.experimental.pallas import tpu as pltpu

F32 = jnp.float32
BF16 = jnp.bfloat16
HIGHEST = lax.Precision.HIGHEST

D_MODEL = 1024
ATT_HEADS = 4
ATT_DH = 64
ATT_WIDTH = ATT_HEADS * 2 * ATT_DH
CONV_CH = 512
CONV_K = 31
SSM_D_INNER = 512
SSM_HEADDIM = 64
SSM_HEADS = SSM_D_INNER // SSM_HEADDIM
SSM_GROUPS = 2
SSM_STATE = 128
SSM_CONV_K = 4
SSM_CONV_DIM = SSM_D_INNER + 2 * SSM_GROUPS * SSM_STATE
SSM_CHUNK = 128
N_EXPERTS = 8
TOP_K = 2
NORM_EPS = 1e-6

LANES = 128
SUBLANES = 8
VMEM_LIMIT_BYTES = 48 * 1024 * 1024

COL_Q = 0
COL_K = COL_Q + ATT_WIDTH
COL_V = COL_K + ATT_WIDTH
COL_U = COL_V + ATT_WIDTH
COL_Z = COL_U + 2 * CONV_CH
COL_XBC = COL_Z + SSM_D_INNER
COL_DT = COL_XBC + SSM_CONV_DIM
COL_G = COL_XBC + SSM_CONV_DIM
PROJ_COLS = COL_G + 3 * D_MODEL


def _params(*semantics):
    return pltpu.CompilerParams(dimension_semantics=semantics, vmem_limit_bytes=VMEM_LIMIT_BYTES)


def _sigmoid(x):
    return 1.0 / (1.0 + jnp.exp(-x))


def _rms(x, eps):
    return x * lax.rsqrt(jnp.mean(x * x, axis=-1, keepdims=True) + eps)


def _norm_matmul_kernel(x_ref, g_ref, w_ref, o_ref, h_ref):
    @pl.when(pl.program_id(1) == 0)
    def _():
        h_ref[...] = (_rms(x_ref[...], NORM_EPS) * g_ref[...]).astype(h_ref.dtype)

    o_ref[...] = jnp.dot(h_ref[...], w_ref[...], preferred_element_type=F32).astype(o_ref.dtype)


def _norm_matmul(x, g, w, out_dtype, tm, tn):
    t, d = x.shape
    n = w.shape[1]
    return pl.pallas_call(
        _norm_matmul_kernel,
        grid=(t // tm, n // tn),
        in_specs=[pl.BlockSpec((tm, d), lambda i, j: (i, 0)),
                  pl.BlockSpec((1, d), lambda i, j: (0, 0)),
                  pl.BlockSpec((d, tn), lambda i, j: (0, j))],
        out_specs=pl.BlockSpec((tm, tn), lambda i, j: (i, j)),
        out_shape=jax.ShapeDtypeStruct((t, n), out_dtype),
        scratch_shapes=[pltpu.VMEM((tm, d), BF16)],
        compiler_params=_params("parallel", "arbitrary"),
        name="norm_matmul",
    )(x, g.reshape(1, d), w)


ATT_ONES = 16
LOG2E = 1.4426950408889634


def _attn_kernel(items_ref, lam_ref, q_ref, k_ref, v_ref, sw_ref, o_ref,
                 vt_ref, q2_ref, m_ref, acc_ref, sa_ref, ca_ref, sb_ref, cb_ref, *, tb, n_off, n_diag, out_scale):
    hw = 2 * ATT_DH
    nb = vt_ref.shape[0]

    lane = lax.broadcasted_iota(jnp.int32, (tb, hw), 1)
    for c in range(nb):
        rows = slice(c * tb, (c + 1) * tb)
        vt_ref[c, 0:hw, :] = v_ref[rows, :].astype(F32).T.astype(BF16)
        vt_ref[c, hw:hw + ATT_ONES, :] = jnp.ones((ATT_ONES, tb), BF16)
        q = (q_ref[rows, :].astype(F32) * (ATT_DH ** -0.5 * LOG2E)).astype(BF16)
        zero = jnp.zeros_like(q)
        q2_ref[c, 0:tb, :] = jnp.where(lane < ATT_DH, q, zero)
        q2_ref[c, tb:2 * tb, :] = jnp.where(lane >= ATT_DH, q, zero)
        m_ref[c] = jnp.full(m_ref.shape[1:], -jnp.inf, F32)
        acc_ref[c] = jnp.zeros(acc_ref.shape[1:], F32)

    def fill(buf, t, masked):
        s_ref, cmax_ref = buf
        qi, j = items_ref[0, t], items_ref[1, t]
        k = k_ref[pl.ds(pl.multiple_of(j * tb, tb), tb), :]
        st = lax.dot_general(k, q2_ref[qi], (((1,), (1,)), ((), ())), preferred_element_type=F32)
        if masked:
            r = lax.broadcasted_iota(jnp.int32, st.shape, 0)
            c = lax.broadcasted_iota(jnp.int32, st.shape, 1)
            st = jnp.where(r <= jnp.where(c >= tb, c - tb, c), st, -jnp.inf)
        s_ref[...] = st
        cmax_ref[...] = jnp.max(st, axis=0, keepdims=True)

    def update(buf, t, last):
        s_ref, cmax_ref = buf
        qi, j = items_ref[0, t], items_ref[1, t]
        m_old = m_ref[qi]
        m_new = jnp.maximum(m_old, cmax_ref[...])
        p = jnp.exp2(s_ref[...] - m_new).astype(BF16)
        acc = jnp.exp2(m_old - m_new) * acc_ref[qi] + jnp.dot(vt_ref[j], p, preferred_element_type=F32)
        if not last:
            acc_ref[qi] = acc
            m_ref[qi] = m_new
        else:
            o_t = acc[:hw] * (1.0 / acc[hw:hw + 1])
            o = (o_t[:, :tb] - lam_ref[0] * o_t[:, tb:]).T
            o_ref[pl.ds(pl.multiple_of(qi * tb, tb), tb), :] = (
                _rms(o, 1e-5) * (sw_ref[...] * out_scale)).astype(o_ref.dtype)

    def run(first, count, diag, cur, other):
        trips = (count - 1) // 2

        def body(i, carry):
            t = first + 2 * i
            fill(other, t + 1, diag)
            update(cur, t, diag)
            fill(cur, t + 2, diag)
            update(other, t + 1, diag)
            return carry

        lax.fori_loop(0, trips, body, 0)
        if count - 1 > 2 * trips:
            t = first + 2 * trips
            fill(other, t + 1, diag)
            update(cur, t, diag)
            cur, other = other, cur
        return cur, other

    cur, other = (sa_ref, ca_ref), (sb_ref, cb_ref)
    if n_off:
        fill(cur, 0, False)
        cur, other = run(0, n_off, False, cur, other)
        fill(other, n_off, True)
        update(cur, n_off - 1, False)
        cur, other = other, cur
    else:
        fill(cur, 0, True)
    cur, other = run(n_off, n_diag, True, cur, other)
    update(cur, n_off + n_diag - 1, True)


def _diff_attention(proj3, lam, subln_w, lam_init, tb):
    b, s, _ = proj3.shape
    hw = 2 * ATT_DH
    nb = s // tb
    off = [(qi, j) for qi in range(nb) for j in range(qi)]
    items = jnp.asarray(list(zip(*(off + [(qi, qi) for qi in range(nb)]))), jnp.int32)
    kern = functools.partial(_attn_kernel, tb=tb, n_off=len(off), n_diag=nb, out_scale=1.0 - lam_init)
    seq = lambda col: pl.BlockSpec((None, s, hw), lambda bb, h: (bb, 0, col // hw + h))
    return pl.pallas_call(
        kern,
        grid=(b, ATT_HEADS),
        in_specs=[pl.BlockSpec(memory_space=pltpu.SMEM), pl.BlockSpec(memory_space=pltpu.SMEM),
                  seq(COL_Q), seq(COL_K), seq(COL_V), pl.BlockSpec((1, hw), lambda bb, h: (0, 0))],
        out_specs=pl.BlockSpec((None, s, hw), lambda bb, h: (bb, 0, h)),
        out_shape=jax.ShapeDtypeStruct((b, s, ATT_WIDTH), BF16),
        scratch_shapes=[pltpu.VMEM((nb, hw + ATT_ONES, tb), BF16), pltpu.VMEM((nb, 2 * tb, hw), BF16),
                        pltpu.VMEM((nb, 1, 2 * tb), F32), pltpu.VMEM((nb, hw + ATT_ONES, 2 * tb), F32),
                        pltpu.VMEM((tb, 2 * tb), F32), pltpu.VMEM((1, 2 * tb), F32),
                        pltpu.VMEM((tb, 2 * tb), F32), pltpu.VMEM((1, 2 * tb), F32)],
        compiler_params=_params("parallel", "parallel"),
        name="diff_attention",
    )(items, lam.reshape(1).astype(F32), proj3, proj3, proj3, subln_w.reshape(1, hw).astype(F32))


CONV_HALO = 32
CONV_ROWS = 64


def _cconv_kernel(a_ref, g_ref, w_ref, b_ref, lnw_ref, lnb_ref, o_ref, buf_ref, *, ts):
    i = pl.program_id(1)

    @pl.when(i == 0)
    def _():
        buf_ref[0:CONV_HALO, :] = jnp.zeros((CONV_HALO, CONV_CH), F32)

    @pl.when(i > 0)
    def _():
        buf_ref[0:CONV_HALO, :] = buf_ref[ts:ts + CONV_HALO, :]

    buf_ref[CONV_HALO:CONV_HALO + ts, :] = a_ref[...].astype(F32) * _sigmoid(g_ref[...].astype(F32))

    first = CONV_HALO - (CONV_K - 1)
    for c in range(ts // CONV_ROWS):
        base = c * CONV_ROWS
        acc = jnp.zeros((CONV_ROWS, CONV_CH), F32) + b_ref[...]
        for k in range(CONV_K):
            acc = acc + w_ref[k:k + 1, :] * buf_ref[base + first + k:base + first + k + CONV_ROWS, :]
        mu = jnp.mean(acc, axis=-1, keepdims=True)
        xc = acc - mu
        var = jnp.mean(xc * xc, axis=-1, keepdims=True)
        y = xc * lax.rsqrt(var + 1e-5) * lnw_ref[...] + lnb_ref[...]
        o_ref[base:base + CONV_ROWS, :] = (y * _sigmoid(y)).astype(o_ref.dtype)


def _conformer_conv(proj3, dw_w, dw_b, ln_w, ln_b, ts):
    b, s, _ = proj3.shape
    c = CONV_CH
    row = lambda a: a.reshape(1, c).astype(F32)
    return pl.pallas_call(
        functools.partial(_cconv_kernel, ts=ts),
        grid=(b, s // ts),
        in_specs=[pl.BlockSpec((None, ts, c), lambda bb, i: (bb, i, COL_U // c)),
                  pl.BlockSpec((None, ts, c), lambda bb, i: (bb, i, COL_U // c + 1)),
                  pl.BlockSpec((CONV_K, c), lambda bb, i: (0, 0)),
                  pl.BlockSpec((1, c), lambda bb, i: (0, 0)),
                  pl.BlockSpec((1, c), lambda bb, i: (0, 0)),
                  pl.BlockSpec((1, c), lambda bb, i: (0, 0))],
        out_specs=pl.BlockSpec((None, ts, c), lambda bb, i: (bb, i, 0)),
        out_shape=jax.ShapeDtypeStruct((b, s, c), BF16),
        scratch_shapes=[pltpu.VMEM((CONV_HALO + ts, c), F32)],
        compiler_params=_params("parallel", "arbitrary"),
        name="conformer_conv",
    )(proj3, proj3, dw_w.astype(F32), row(dw_b), row(ln_w), row(ln_b))


SSD_HALO = 8


def _ssd_kernel(z_ref, xbc_ref, dt_ref, cw_ref, cb_ref, dtb_ref, a_ref, dsk_ref, nw_ref, e_ref,
                o_ref, buf_ref, h_ref, *, chunk):
    L = chunk
    ci = pl.program_id(1)
    gw = SSM_D_INNER // SSM_GROUPS
    hpg = SSM_HEADS // SSM_GROUPS

    @pl.when(ci == 0)
    def _():
        buf_ref[0:SSD_HALO, :] = jnp.zeros((SSD_HALO, SSM_CONV_DIM), F32)
        h_ref[...] = jnp.zeros(h_ref.shape, F32)

    @pl.when(ci > 0)
    def _():
        buf_ref[0:SSD_HALO, :] = buf_ref[L:L + SSD_HALO, :]

    buf_ref[SSD_HALO:SSD_HALO + L, :] = xbc_ref[...].astype(F32)
    first = SSD_HALO - (SSM_CONV_K - 1)
    xc = jnp.zeros((L, SSM_CONV_DIM), F32) + cb_ref[...]
    for k in range(SSM_CONV_K):
        xc = xc + cw_ref[k:k + 1, :] * buf_ref[first + k:first + k + L, :]
    xc = xc * _sigmoid(xc)
    xs = xc[:, :SSM_D_INNER]
    bm = xc[:, SSM_D_INNER:SSM_D_INNER + SSM_GROUPS * SSM_STATE]
    cm = xc[:, SSM_D_INNER + SSM_GROUPS * SSM_STATE:]

    dtr = dt_ref[...] + dtb_ref[...]
    dt = jnp.maximum(dtr, 0.0) + jnp.log(1.0 + jnp.exp(-jnp.abs(dtr)))
    da = dt * a_ref[...]
    row = lax.broadcasted_iota(jnp.int32, (L, L), 0)
    col = lax.broadcasted_iota(jnp.int32, (L, L), 1)
    causal = col <= row
    a_cum = jnp.dot(causal.astype(F32), da, precision=HIGHEST, preferred_element_type=F32)
    a_cum_t = jnp.dot(da.T, (row <= col).astype(F32), precision=HIGHEST, preferred_element_type=F32)

    expand = e_ref[...]
    dt_full = jnp.dot(dt, expand, precision=HIGHEST, preferred_element_type=F32)
    ea_full = jnp.dot(jnp.exp(a_cum), expand, precision=HIGHEST, preferred_element_type=F32)
    dend_full = jnp.dot(jnp.exp(a_cum[L - 1:L, :] - a_cum), expand, precision=HIGHEST,
                        preferred_element_type=F32)
    cdec_full = ea_full[L - 1:L, :]

    xdt = xs * dt_full
    xdt_b = xdt.astype(BF16)
    xw_b = (xdt * dend_full).astype(BF16)

    y_parts = []
    for g in range(SSM_GROUPS):
        cg = cm[:, g * SSM_STATE:(g + 1) * SSM_STATE].astype(BF16)
        bg_t = bm[:, g * SSM_STATE:(g + 1) * SSM_STATE].T.astype(BF16)
        cbm = jnp.dot(cg, bg_t, preferred_element_type=F32)
        h_in = h_ref[:, g * gw:(g + 1) * gw]
        y_off = jnp.dot(cg, h_in.astype(BF16), preferred_element_type=F32)
        states = jnp.dot(bg_t, xw_b[:, g * gw:(g + 1) * gw], preferred_element_type=F32)
        h_ref[:, g * gw:(g + 1) * gw] = h_in * cdec_full[:, g * gw:(g + 1) * gw] + states
        yd = []
        for r in range(hpg):
            hh = g * hpg + r
            seg = a_cum[:, hh:hh + 1] - a_cum_t[hh:hh + 1, :]
            dec = jnp.exp(jnp.where(causal, seg, -jnp.inf))
            yd.append(jnp.dot((cbm * dec).astype(BF16), xdt_b[:, hh * SSM_HEADDIM:(hh + 1) * SSM_HEADDIM],
                              preferred_element_type=F32))
        y_g = (jnp.concatenate(yd, axis=-1) + y_off * ea_full[:, g * gw:(g + 1) * gw]
               + xs[:, g * gw:(g + 1) * gw] * dsk_ref[:, g * gw:(g + 1) * gw])
        zg = z_ref[:, g * gw:(g + 1) * gw].astype(F32)
        y_parts.append(_rms(y_g * (zg * _sigmoid(zg)), 1e-5))
    o_ref[...] = (jnp.concatenate(y_parts, axis=-1) * nw_ref[...]).astype(o_ref.dtype)


def _mamba2_ssd(proj3, dt3, conv_w, conv_b, dt_bias, a_log, d_skip, norm_w):
    b, s, _ = proj3.shape
    L = SSM_CHUNK
    di, cd = SSM_D_INNER, SSM_CONV_DIM
    pad_h = lambda a: jnp.pad(a.astype(F32), (0, LANES - SSM_HEADS)).reshape(1, LANES)
    head_of_channel = jnp.arange(di, dtype=jnp.int32) // SSM_HEADDIM
    expand = (jnp.arange(LANES, dtype=jnp.int32)[:, None] == head_of_channel[None, :]).astype(F32)
    const = lambda shape: pl.BlockSpec(shape, lambda bb, c: (0, 0))
    return pl.pallas_call(
        functools.partial(_ssd_kernel, chunk=L),
        grid=(b, s // L),
        in_specs=[pl.BlockSpec((None, L, di), lambda bb, c: (bb, c, COL_Z // di)),
                  pl.BlockSpec((None, L, cd), lambda bb, c: (bb, c, COL_XBC // cd)),
                  pl.BlockSpec((None, L, LANES), lambda bb, c: (bb, c, 0)),
                  const((SSM_CONV_K, cd)), const((1, cd)), const((1, LANES)), const((1, LANES)),
                  const((1, di)), const((1, di)), const((LANES, di))],
        out_specs=pl.BlockSpec((None, L, di), lambda bb, c: (bb, c, 0)),
        out_shape=jax.ShapeDtypeStruct((b, s, di), BF16),
        scratch_shapes=[pltpu.VMEM((SSD_HALO + L, cd), F32), pltpu.VMEM((SSM_STATE, di), F32)],
        compiler_params=_params("parallel", "arbitrary"),
        name="mamba2_ssd",
    )(proj3, proj3, dt3, conv_w.astype(F32), conv_b.reshape(1, cd).astype(F32), pad_h(dt_bias),
      pad_h(-jnp.exp(a_log.astype(F32))), jnp.repeat(d_skip.astype(F32), SSM_HEADDIM).reshape(1, di),
      norm_w.reshape(1, di).astype(F32), expand)


def _merge_kernel(x_ref, ya_ref, yc_ref, ys_ref, ga_ref, gc_ref, gs_ref, wa_ref, wc_ref, ws_ref, wm_ref, o_ref):
    def branch(y_ref, g_ref, w_ref):
        return _sigmoid(g_ref[...].astype(F32)) * jnp.dot(y_ref[...], w_ref[...], preferred_element_type=F32)

    merged = branch(ya_ref, ga_ref, wa_ref) + branch(yc_ref, gc_ref, wc_ref) + branch(ys_ref, gs_ref, ws_ref)
    o_ref[...] = x_ref[...] + jnp.dot(merged.astype(BF16), wm_ref[...], preferred_element_type=F32)


def _merge(x, ya, yc, ys, proj, wa, wc, ws, wm, tm):
    t, d = x.shape
    w = ya.shape[1]
    rows = lambda width, cb: pl.BlockSpec((tm, width), lambda i: (i, cb))
    const = lambda shape: pl.BlockSpec(shape, lambda i: (0, 0))
    return pl.pallas_call(
        _merge_kernel,
        grid=(t // tm,),
        in_specs=[rows(d, 0), rows(w, 0), rows(w, 0), rows(w, 0),
                  rows(d, COL_G // d), rows(d, COL_G // d + 1), rows(d, COL_G // d + 2),
                  const((w, d)), const((w, d)), const((w, d)), const((d, d))],
        out_specs=rows(d, 0),
        out_shape=jax.ShapeDtypeStruct((t, d), F32),
        compiler_params=_params("parallel"),
        name="gated_merge",
    )(x, ya, yc, ys, proj, proj, proj, wa, wc, ws, wm)


def _ffn_kernel(x_ref, g_ref, wg_ref, wu_ref, wd_ref, o_ref, h_ref, acc_ref):
    j = pl.program_id(1)

    @pl.when(j == 0)
    def _():
        h_ref[...] = (_rms(x_ref[...], NORM_EPS) * g_ref[...]).astype(h_ref.dtype)
        acc_ref[...] = jnp.zeros(acc_ref.shape, F32)

    h = h_ref[...]
    a = jnp.dot(h, wg_ref[...], preferred_element_type=F32)
    u = jnp.dot(h, wu_ref[...], preferred_element_type=F32)
    acc_ref[...] += jnp.dot((a * _sigmoid(a) * u).astype(BF16), wd_ref[...], preferred_element_type=F32)

    @pl.when(j == pl.num_programs(1) - 1)
    def _():
        o_ref[...] = x_ref[...] + acc_ref[...]


def _ffn_dense(x, g, wg, wu, wd, tm, tf):
    t, d = x.shape
    f = wg.shape[1]
    return pl.pallas_call(
        _ffn_kernel,
        grid=(t // tm, f // tf),
        in_specs=[pl.BlockSpec((tm, d), lambda i, j: (i, 0)),
                  pl.BlockSpec((1, d), lambda i, j: (0, 0)),
                  pl.BlockSpec((d, tf), lambda i, j: (0, j)),
                  pl.BlockSpec((d, tf), lambda i, j: (0, j)),
                  pl.BlockSpec((tf, d), lambda i, j: (j, 0))],
        out_specs=pl.BlockSpec((tm, d), lambda i, j: (i, 0)),
        out_shape=jax.ShapeDtypeStruct((t, d), F32),
        scratch_shapes=[pltpu.VMEM((tm, d), BF16), pltpu.VMEM((tm, d), F32)],
        compiler_params=_params("parallel", "arbitrary"),
        name="ffn_dense",
    )(x, g.reshape(1, d), wg, wu, wd)


def _router_kernel(x_ref, g_ref, wr_ref, br_ref, h_ref, idx_ref, wt_ref, rank_ref, cnt_ref, tri_ref, base_ref):
    i = pl.program_id(0)
    tm = x_ref.shape[0]

    @pl.when(i == 0)
    def _():
        r = lax.broadcasted_iota(jnp.int32, (tm, tm), 0)
        c = lax.broadcasted_iota(jnp.int32, (tm, tm), 1)
        tri_ref[...] = jnp.where(c < r, 1.0, 0.0).astype(BF16)
        base_ref[...] = jnp.zeros(base_ref.shape, F32)

    h = _rms(x_ref[...], NORM_EPS) * g_ref[...]
    h_ref[...] = h.astype(h_ref.dtype)
    logits = jnp.dot(h, wr_ref[...], precision=HIGHEST, preferred_element_type=F32) + br_ref[...]
    lane = lax.broadcasted_iota(jnp.int32, logits.shape, 1)
    logits = jnp.where(lane < N_EXPERTS, logits, -jnp.inf)
    m1 = jnp.max(logits, axis=-1, keepdims=True)
    i1 = jnp.min(jnp.where(logits == m1, lane, LANES), axis=-1, keepdims=True)
    rest = jnp.where(lane == i1, -jnp.inf, logits)
    m2 = jnp.max(rest, axis=-1, keepdims=True)
    i2 = jnp.min(jnp.where(rest == m2, lane, LANES), axis=-1, keepdims=True)
    e = jnp.exp(m2 - m1)
    w1 = 1.0 / (1.0 + e)
    idx_ref[...] = jnp.where(lane == 0, i1, i2)[:, :TOP_K]
    wt_ref[...] = jnp.where(lane == 0, w1, e * w1)[:, :TOP_K]

    chosen = jnp.logical_or(lane == i1, lane == i2)
    before = jnp.dot(tri_ref[...], jnp.where(chosen, 1.0, 0.0).astype(BF16), preferred_element_type=F32)
    before = before + base_ref[...]
    r1 = jnp.sum(jnp.where(lane == i1, before, 0.0), axis=-1, keepdims=True)
    r2 = jnp.sum(jnp.where(lane == i2, before, 0.0), axis=-1, keepdims=True)
    rank_ref[...] = jnp.where(lane == 0, r1, r2)[:, :TOP_K].astype(jnp.int32)
    base_ref[...] += jnp.sum(jnp.where(chosen, 1.0, 0.0), axis=0, keepdims=True)
    cnt_ref[...] = base_ref[...].astype(jnp.int32)


def _router(x, g, w_router, b_router, tm):
    t, d = x.shape
    wr = jnp.pad(w_router.astype(F32), ((0, 0), (0, LANES - N_EXPERTS)))
    br = jnp.pad(b_router.astype(F32), (0, LANES - N_EXPERTS)).reshape(1, LANES)
    pair = pl.BlockSpec((tm, TOP_K), lambda i: (i, 0))
    return pl.pallas_call(
        _router_kernel,
        grid=(t // tm,),
        in_specs=[pl.BlockSpec((tm, d), lambda i: (i, 0)),
                  pl.BlockSpec((1, d), lambda i: (0, 0)),
                  pl.BlockSpec((d, LANES), lambda i: (0, 0)),
                  pl.BlockSpec((1, LANES), lambda i: (0, 0))],
        out_specs=[pl.BlockSpec((tm, d), lambda i: (i, 0)), pair, pair, pair,
                   pl.BlockSpec((1, LANES), lambda i: (0, 0))],
        out_shape=[jax.ShapeDtypeStruct((t, d), BF16),
                   jax.ShapeDtypeStruct((t, TOP_K), jnp.int32),
                   jax.ShapeDtypeStruct((t, TOP_K), F32),
                   jax.ShapeDtypeStruct((t, TOP_K), jnp.int32),
                   jax.ShapeDtypeStruct((1, LANES), jnp.int32)],
        scratch_shapes=[pltpu.VMEM((tm, tm), BF16), pltpu.VMEM((1, LANES), F32)],
        compiler_params=_params("arbitrary"),
        name="moe_router",
    )(x, g.reshape(1, d), wr, br)


def _expert_kernel(be_ref, nu_ref, x_ref, wg_ref, wu_ref, wd_ref, o_ref, acc_ref):
    i = pl.program_id(0)
    j = pl.program_id(1)
    last = pl.num_programs(1) - 1
    used = i < nu_ref[0]

    @pl.when(jnp.logical_and(used, j == 0))
    def _():
        acc_ref[...] = jnp.zeros(acc_ref.shape, F32)

    @pl.when(used)
    def _():
        x = x_ref[...]
        a = jnp.dot(x, wg_ref[...], preferred_element_type=F32)
        u = jnp.dot(x, wu_ref[...], preferred_element_type=F32)
        acc_ref[...] += jnp.dot((a * _sigmoid(a) * u).astype(BF16), wd_ref[...], preferred_element_type=F32)

    @pl.when(jnp.logical_and(used, j == last))
    def _():
        o_ref[...] = acc_ref[...].astype(o_ref.dtype)

    @pl.when(jnp.logical_and(jnp.logical_not(used), j == last))
    def _():
        o_ref[...] = jnp.zeros(o_ref.shape, o_ref.dtype)


def _expert_ffn(xs, block_exp, n_used, wg, wu, wd, bm, tf):
    n_rows, d = xs.shape
    f = wg.shape[2]
    nj = f // tf

    def jj(i, j, nu):
        return jnp.where(i < nu[0], j, nj - 1)

    grid_spec = pltpu.PrefetchScalarGridSpec(
        num_scalar_prefetch=2,
        grid=(n_rows // bm, nj),
        in_specs=[pl.BlockSpec((bm, d), lambda i, j, be, nu: (i, 0)),
                  pl.BlockSpec((None, d, tf), lambda i, j, be, nu: (be[i], 0, jj(i, j, nu))),
                  pl.BlockSpec((None, d, tf), lambda i, j, be, nu: (be[i], 0, jj(i, j, nu))),
                  pl.BlockSpec((None, tf, d), lambda i, j, be, nu: (be[i], jj(i, j, nu), 0))],
        out_specs=pl.BlockSpec((bm, d), lambda i, j, be, nu: (i, 0)),
        scratch_shapes=[pltpu.VMEM((bm, d), F32)],
    )
    return pl.pallas_call(
        _expert_kernel,
        grid_spec=grid_spec,
        out_shape=jax.ShapeDtypeStruct((n_rows, d), BF16),
        compiler_params=_params("parallel", "arbitrary"),
        name="moe_experts",
    )(block_exp, n_used, xs, wg, wu, wd)


def _route(top_idx, rank, counts, bm):
    t = top_idx.shape[0]
    tk = t * TOP_K
    padded = (counts + bm - 1) // bm * bm
    end_padded = jnp.cumsum(padded)
    start_padded = end_padded - padded
    experts = jnp.arange(N_EXPERTS, dtype=jnp.int32)
    dest = rank + jnp.sum(jnp.where(top_idx[..., None] == experts, start_padded, 0), axis=-1)
    n_blocks = tk // bm + N_EXPERTS
    flat_tok = jnp.repeat(jnp.arange(t, dtype=jnp.int32), TOP_K)
    row_tok = jnp.zeros((n_blocks * bm,), jnp.int32).at[dest.reshape(tk)].set(flat_tok, unique_indices=True)
    block_start = jnp.arange(n_blocks, dtype=jnp.int32) * bm
    block_exp = jnp.minimum(jnp.sum(block_start[:, None] >= end_padded[None, :], axis=1), N_EXPERTS - 1)
    n_used = (end_padded[-1] // bm).reshape(1)
    return row_tok, dest, block_exp.astype(jnp.int32), n_used.astype(jnp.int32)


def _combine_kernel(x_ref, a_ref, b_ref, w_ref, g_ref, o_ref, *, final_norm):
    w = w_ref[...]
    y = x_ref[...] + w[:, 0:1] * a_ref[...].astype(F32) + w[:, 1:2] * b_ref[...].astype(F32)
    o_ref[...] = _rms(y, NORM_EPS) * g_ref[...] if final_norm else y


def _combine(x, a, b, w, g, tm):
    t, d = x.shape
    spec = pl.BlockSpec((tm, d), lambda i: (i, 0))
    final_norm = g is not None
    gain = g.reshape(1, d) if final_norm else jnp.ones((1, d), F32)
    return pl.pallas_call(
        functools.partial(_combine_kernel, final_norm=final_norm), grid=(t // tm,),
        in_specs=[spec, spec, spec, pl.BlockSpec((tm, TOP_K), lambda i: (i, 0)),
                  pl.BlockSpec((1, d), lambda i: (0, 0))],
        out_specs=spec,
        out_shape=jax.ShapeDtypeStruct((t, d), F32), compiler_params=_params("parallel"),
        name="moe_combine",
    )(x, a, b, w, gain)


def _moe(x, g, w_router, b_router, wg, wu, wd, final_g, tm, bm, tf):
    h, top_idx, top_w, rank, counts = _router(x, g, w_router, b_router, tm)
    row_tok, dest, block_exp, n_used = _route(top_idx, rank, counts[0, :N_EXPERTS], bm)
    xs = jnp.take(h, row_tok, axis=0)
    ys = _expert_ffn(xs, block_exp, n_used, wg, wu, wd, bm, tf)
    return _combine(x, jnp.take(ys, dest[:, 0], axis=0), jnp.take(ys, dest[:, 1], axis=0), top_w, final_g, tm)


def _final_norm_kernel(x_ref, g_ref, o_ref):
    o_ref[...] = _rms(x_ref[...], NORM_EPS) * g_ref[...]


def _final_norm(x, g, tm):
    t, d = x.shape
    return pl.pallas_call(
        _final_norm_kernel, grid=(t // tm,),
        in_specs=[pl.BlockSpec((tm, d), lambda i: (i, 0)), pl.BlockSpec((1, d), lambda i: (0, 0))],
        out_specs=pl.BlockSpec((tm, d), lambda i: (i, 0)),
        out_shape=jax.ShapeDtypeStruct((t, d), F32), compiler_params=_params("parallel"),
        name="final_norm",
    )(x, g.reshape(1, d))


def _tiles(t, s):
    pick = lambda n, pref: pref if n % pref == 0 else n
    return dict(tm_proj=pick(t, 1024), tn_proj=1024, tb_attn=pick(s, 512), ts_conv=pick(s, 512),
                tm_merge=pick(t, 512), tm_ffn=pick(t, 512), tm_row=pick(t, 1024), bm_moe=pick(t, 1024))


def kernel(x, ln_mix_w, w_in, attn_lq1, attn_lk1, attn_lq2, attn_lk2, attn_subln_w, attn_w_o, conv_dw_w, conv_dw_b, conv_ln_w, conv_ln_b, conv_w_pw, ssm_conv_w, ssm_conv_b, ssm_dt_bias, ssm_A_log, ssm_D, ssm_norm_w, ssm_w_out, w_mix_out, ln_ffn_w, ffn_w_gate, ffn_w_up, ffn_w_down, moe_w_router, moe_b_router, moe_w_gate, moe_w_up, moe_w_down, ln_final_w):
    b, s, d = x.shape
    t = b * s
    depth = w_in.shape[0]
    tl = _tiles(t, s)
    xf = x.reshape(t, d)
    for l in range(depth):
        w_main = jnp.concatenate([w_in[l][:, :COL_DT], w_in[l][:, COL_DT + SSM_HEADS:]], axis=1).astype(BF16)
        w_dt = jnp.pad(w_in[l][:, COL_DT:COL_DT + SSM_HEADS], ((0, 0), (0, LANES - SSM_HEADS))).astype(BF16)
        proj = _norm_matmul(xf, ln_mix_w[l], w_main, BF16, tl["tm_proj"], tl["tn_proj"])
        dt = _norm_matmul(xf, ln_mix_w[l], w_dt, F32, tl["tm_proj"], LANES)
        proj3 = proj.reshape(b, s, PROJ_COLS)

        lam_init = 0.8 - 0.6 * math.exp(-0.3 * l)
        lam = (jnp.exp(jnp.sum(attn_lq1[l].astype(F32) * attn_lk1[l].astype(F32)))
               - jnp.exp(jnp.sum(attn_lq2[l].astype(F32) * attn_lk2[l].astype(F32))) + lam_init)
        y_att = _diff_attention(proj3, lam, attn_subln_w[l], lam_init, tl["tb_attn"])
        y_conv = _conformer_conv(proj3, conv_dw_w[l], conv_dw_b[l], conv_ln_w[l], conv_ln_b[l], tl["ts_conv"])
        y_ssm = _mamba2_ssd(proj3, dt.reshape(b, s, LANES), ssm_conv_w[l], ssm_conv_b[l], ssm_dt_bias[l],
                            ssm_A_log[l], ssm_D[l], ssm_norm_w[l])
        xf = _merge(xf, y_att.reshape(t, ATT_WIDTH), y_conv.reshape(t, CONV_CH), y_ssm.reshape(t, SSM_D_INNER),
                    proj, attn_w_o[l].astype(BF16), conv_w_pw[l].astype(BF16), ssm_w_out[l].astype(BF16),
                    w_mix_out[l].astype(BF16), tl["tm_merge"])

        i = l // 2
        if l % 2 == 0:
            f = ffn_w_gate.shape[2]
            tf = f // 2 if (f // 2) % LANES == 0 else f
            xf = _ffn_dense(xf, ln_ffn_w[l], ffn_w_gate[i].astype(BF16), ffn_w_up[i].astype(BF16),
                            ffn_w_down[i].astype(BF16), tl["tm_ffn"], tf)
        else:
            f = moe_w_gate.shape[3]
            tf = 512 if f % 512 == 0 else f
            final_g = ln_final_w if l == depth - 1 else None
            xf = _moe(xf, ln_ffn_w[l], moe_w_router[i], moe_b_router[i], moe_w_gate[i].astype(BF16),
                      moe_w_up[i].astype(BF16), moe_w_down[i].astype(BF16), final_g, tl["tm_row"], tl["bm_moe"], tf)
    if depth % 2 == 1:
        xf = _final_norm(xf, ln_final_w, tl["tm_row"])
    return xf.reshape(b, s, d)
```

```python
import functools
import math

import jax
import jax.numpy as jnp
from jax import lax
from jax.experimental import pallas as pl
from jax.experimental.pallas import tpu as pltpu

F32 = jnp.float32
BF16 = jnp.bfloat16
HIGHEST = lax.Precision.HIGHEST

D_MODEL = 1024
ATT_HEADS = 4
ATT_DH = 64
ATT_WIDTH = ATT_HEADS * 2 * ATT_DH
CONV_CH = 512
CONV_K = 31
SSM_D_INNER = 512
SSM_HEADDIM = 64
SSM_HEADS = SSM_D_INNER // SSM_HEADDIM
SSM_GROUPS = 2
SSM_STATE = 128
SSM_CONV_K = 4
SSM_CONV_DIM = SSM_D_INNER + 2 * SSM_GROUPS * SSM_STATE
SSM_CHUNK = 128
N_EXPERTS = 8
TOP_K = 2
NORM_EPS = 1e-6

LANES = 128
SUBLANES = 8
VMEM_LIMIT_BYTES = 48 * 1024 * 1024

COL_Q = 0
COL_K = COL_Q + ATT_WIDTH
COL_V = COL_K + ATT_WIDTH
COL_U = COL_V + ATT_WIDTH
COL_Z = COL_U + 2 * CONV_CH
COL_XBC = COL_Z + SSM_D_INNER
COL_DT = COL_XBC + SSM_CONV_DIM
COL_G = COL_XBC + SSM_CONV_DIM
PROJ_COLS = COL_G + 3 * D_MODEL


def _params(*semantics):
    return pltpu.CompilerParams(dimension_semantics=semantics, vmem_limit_bytes=VMEM_LIMIT_BYTES)


def _sigmoid(x):
    return 1.0 / (1.0 + jnp.exp(-x))


def _rms(x, eps):
    return x * lax.rsqrt(jnp.mean(x * x, axis=-1, keepdims=True) + eps)


def _split3(x):
    hi = x.astype(BF16)
    rest = x - hi.astype(F32)
    mid = rest.astype(BF16)
    return hi, mid, (rest - mid.astype(F32)).astype(BF16)


def _norm_matmul_kernel(x_ref, g_ref, w_ref, o_ref, h_ref):
    @pl.when(pl.program_id(1) == 0)
    def _():
        h_ref[...] = (_rms(x_ref[...], NORM_EPS) * g_ref[...]).astype(h_ref.dtype)

    o_ref[...] = jnp.dot(h_ref[...], w_ref[...], preferred_element_type=F32).astype(o_ref.dtype)


def _norm_matmul(x, g, w, out_dtype, tm, tn):
    t, d = x.shape
    n = w.shape[1]
    return pl.pallas_call(
        _norm_matmul_kernel,
        grid=(t // tm, n // tn),
        in_specs=[pl.BlockSpec((tm, d), lambda i, j: (i, 0)),
                  pl.BlockSpec((1, d), lambda i, j: (0, 0)),
                  pl.BlockSpec((d, tn), lambda i, j: (0, j))],
        out_specs=pl.BlockSpec((tm, tn), lambda i, j: (i, j)),
        out_shape=jax.ShapeDtypeStruct((t, n), out_dtype),
        scratch_shapes=[pltpu.VMEM((tm, d), BF16)],
        compiler_params=_params("parallel", "arbitrary"),
        name="norm_matmul",
    )(x, g.reshape(1, d), w)


ATT_ONES = 16
LOG2E = 1.4426950408889634


def _attn_kernel(items_ref, lam_ref, q_ref, k_ref, v_ref, sw_ref, o_ref,
                 vt_ref, q2_ref, m_ref, acc_ref, sa_ref, ca_ref, sb_ref, cb_ref, *, tb, n_off, n_diag, out_scale):
    hw = 2 * ATT_DH
    nb = vt_ref.shape[0]

    lane = lax.broadcasted_iota(jnp.int32, (tb, hw), 1)
    for c in range(nb):
        rows = slice(c * tb, (c + 1) * tb)
        vt_ref[c, 0:hw, :] = v_ref[rows, :].astype(F32).T.astype(BF16)
        vt_ref[c, hw:hw + ATT_ONES, :] = jnp.ones((ATT_ONES, tb), BF16)
        q = (q_ref[rows, :].astype(F32) * (ATT_DH ** -0.5 * LOG2E)).astype(BF16)
        zero = jnp.zeros_like(q)
        q2_ref[c, 0:tb, :] = jnp.where(lane < ATT_DH, q, zero)
        q2_ref[c, tb:2 * tb, :] = jnp.where(lane >= ATT_DH, q, zero)
        m_ref[c] = jnp.full(m_ref.shape[1:], -jnp.inf, F32)
        acc_ref[c] = jnp.zeros(acc_ref.shape[1:], F32)

    def fill(buf, t, masked):
        s_ref, cmax_ref = buf
        qi, j = items_ref[0, t], items_ref[1, t]
        k = k_ref[pl.ds(pl.multiple_of(j * tb, tb), tb), :]
        st = lax.dot_general(k, q2_ref[qi], (((1,), (1,)), ((), ())), preferred_element_type=F32)
        if masked:
            r = lax.broadcasted_iota(jnp.int32, st.shape, 0)
            c = lax.broadcasted_iota(jnp.int32, st.shape, 1)
            st = jnp.where(r <= jnp.where(c >= tb, c - tb, c), st, -jnp.inf)
        s_ref[...] = st
        cmax_ref[...] = jnp.max(st, axis=0, keepdims=True)

    def update(buf, t, last):
        s_ref, cmax_ref = buf
        qi, j = items_ref[0, t], items_ref[1, t]
        m_old = m_ref[qi]
        m_new = jnp.maximum(m_old, cmax_ref[...])
        p = jnp.exp2(s_ref[...] - m_new).astype(BF16)
        acc = jnp.exp2(m_old - m_new) * acc_ref[qi] + jnp.dot(vt_ref[j], p, preferred_element_type=F32)
        if not last:
            acc_ref[qi] = acc
            m_ref[qi] = m_new
        else:
            o_t = acc[:hw] * (1.0 / acc[hw:hw + 1])
            o = (o_t[:, :tb] - lam_ref[0] * o_t[:, tb:]).T
            o_ref[pl.ds(pl.multiple_of(qi * tb, tb), tb), :] = (
                _rms(o, 1e-5) * (sw_ref[...] * out_scale)).astype(o_ref.dtype)

    def run(first, count, diag, cur, other):
        trips = (count - 1) // 2

        def body(i, carry):
            t = first + 2 * i
            fill(other, t + 1, diag)
            update(cur, t, diag)
            fill(cur, t + 2, diag)
            update(other, t + 1, diag)
            return carry

        lax.fori_loop(0, trips, body, 0)
        if count - 1 > 2 * trips:
            t = first + 2 * trips
            fill(other, t + 1, diag)
            update(cur, t, diag)
            cur, other = other, cur
        return cur, other

    cur, other = (sa_ref, ca_ref), (sb_ref, cb_ref)
    if n_off:
        fill(cur, 0, False)
        cur, other = run(0, n_off, False, cur, other)
        fill(other, n_off, True)
        update(cur, n_off - 1, False)
        cur, other = other, cur
    else:
        fill(cur, 0, True)
    cur, other = run(n_off, n_diag, True, cur, other)
    update(cur, n_off + n_diag - 1, True)


def _diff_attention(proj3, lam, subln_w, lam_init, tb):
    b, s, _ = proj3.shape
    hw = 2 * ATT_DH
    nb = s // tb
    off = [(qi, j) for qi in range(nb) for j in range(qi)]
    items = jnp.asarray(list(zip(*(off + [(qi, qi) for qi in range(nb)]))), jnp.int32)
    kern = functools.partial(_attn_kernel, tb=tb, n_off=len(off), n_diag=nb, out_scale=1.0 - lam_init)
    seq = lambda col: pl.BlockSpec((None, s, hw), lambda bb, h: (bb, 0, col // hw + h))
    return pl.pallas_call(
        kern,
        grid=(b, ATT_HEADS),
        in_specs=[pl.BlockSpec(memory_space=pltpu.SMEM), pl.BlockSpec(memory_space=pltpu.SMEM),
                  seq(COL_Q), seq(COL_K), seq(COL_V), pl.BlockSpec((1, hw), lambda bb, h: (0, 0))],
        out_specs=pl.BlockSpec((None, s, hw), lambda bb, h: (bb, 0, h)),
        out_shape=jax.ShapeDtypeStruct((b, s, ATT_WIDTH), BF16),
        scratch_shapes=[pltpu.VMEM((nb, hw + ATT_ONES, tb), BF16), pltpu.VMEM((nb, 2 * tb, hw), BF16),
                        pltpu.VMEM((nb, 1, 2 * tb), F32), pltpu.VMEM((nb, hw + ATT_ONES, 2 * tb), F32),
                        pltpu.VMEM((tb, 2 * tb), F32), pltpu.VMEM((1, 2 * tb), F32),
                        pltpu.VMEM((tb, 2 * tb), F32), pltpu.VMEM((1, 2 * tb), F32)],
        compiler_params=_params("parallel", "parallel"),
        name="diff_attention",
    )(items, lam.reshape(1).astype(F32), proj3, proj3, proj3, subln_w.reshape(1, hw).astype(F32))


CONV_HALO = 32
CONV_ROWS = 64


def _cconv_kernel(a_ref, g_ref, w_ref, b_ref, lnw_ref, lnb_ref, o_ref, buf_ref, *, ts):
    i = pl.program_id(1)

    @pl.when(i == 0)
    def _():
        buf_ref[0:CONV_HALO, :] = jnp.zeros((CONV_HALO, CONV_CH), F32)

    @pl.when(i > 0)
    def _():
        buf_ref[0:CONV_HALO, :] = buf_ref[ts:ts + CONV_HALO, :]

    buf_ref[CONV_HALO:CONV_HALO + ts, :] = a_ref[...].astype(F32) * _sigmoid(g_ref[...].astype(F32))

    first = CONV_HALO - (CONV_K - 1)
    for c in range(ts // CONV_ROWS):
        base = c * CONV_ROWS
        acc = jnp.zeros((CONV_ROWS, CONV_CH), F32) + b_ref[...]
        for res in range(SUBLANES):
            taps = [k for k in range(CONV_K) if (first + k) % SUBLANES == res]
            rows = CONV_ROWS + (SUBLANES if res else 0)
            part = None
            for k in taps:
                lo = base + first + k - res
                term = w_ref[k:k + 1, :] * buf_ref[lo:lo + rows, :]
                part = term if part is None else part + term
            acc = acc + part[res:res + CONV_ROWS]
        mu = jnp.mean(acc, axis=-1, keepdims=True)
        xc = acc - mu
        var = jnp.mean(xc * xc, axis=-1, keepdims=True)
        y = xc * lax.rsqrt(var + 1e-5) * lnw_ref[...] + lnb_ref[...]
        o_ref[base:base + CONV_ROWS, :] = (y * _sigmoid(y)).astype(o_ref.dtype)


def _conformer_conv(proj3, dw_w, dw_b, ln_w, ln_b, ts):
    b, s, _ = proj3.shape
    c = CONV_CH
    row = lambda a: a.reshape(1, c).astype(F32)
    return pl.pallas_call(
        functools.partial(_cconv_kernel, ts=ts),
        grid=(b, s // ts),
        in_specs=[pl.BlockSpec((None, ts, c), lambda bb, i: (bb, i, COL_U // c)),
                  pl.BlockSpec((None, ts, c), lambda bb, i: (bb, i, COL_U // c + 1)),
                  pl.BlockSpec((CONV_K, c), lambda bb, i: (0, 0)),
                  pl.BlockSpec((1, c), lambda bb, i: (0, 0)),
                  pl.BlockSpec((1, c), lambda bb, i: (0, 0)),
                  pl.BlockSpec((1, c), lambda bb, i: (0, 0))],
        out_specs=pl.BlockSpec((None, ts, c), lambda bb, i: (bb, i, 0)),
        out_shape=jax.ShapeDtypeStruct((b, s, c), BF16),
        scratch_shapes=[pltpu.VMEM((CONV_HALO + ts, c), F32)],
        compiler_params=_params("parallel", "arbitrary"),
        name="conformer_conv",
    )(proj3, proj3, dw_w.astype(F32), row(dw_b), row(ln_w), row(ln_b))


SSD_HALO = 16


def _ssd_kernel(z_ref, xbc_ref, dt_ref, cw_ref, cb_ref, dtb_ref, a_ref, dsk_ref, nw_ref, e_ref,
                o_ref, buf_ref, h_ref, *, chunk):
    L = chunk
    ci = pl.program_id(1)
    gw = SSM_D_INNER // SSM_GROUPS
    hpg = SSM_HEADS // SSM_GROUPS

    @pl.when(ci == 0)
    def _():
        buf_ref[0:SSD_HALO, :] = jnp.zeros((SSD_HALO, SSM_CONV_DIM), BF16)
        h_ref[...] = jnp.zeros(h_ref.shape, F32)

    @pl.when(ci > 0)
    def _():
        buf_ref[0:SSD_HALO, :] = buf_ref[L:L + SSD_HALO, :]

    buf_ref[SSD_HALO:SSD_HALO + L, :] = xbc_ref[...]
    xraw = buf_ref[...]
    r = lax.broadcasted_iota(jnp.int32, (L, SSD_HALO + L), 0)
    c = lax.broadcasted_iota(jnp.int32, (L, SSD_HALO + L), 1)
    xc = jnp.zeros((L, SSM_CONV_DIM), F32) + cb_ref[...]
    for k in range(SSM_CONV_K):
        shift = jnp.where(c == r + (SSD_HALO - (SSM_CONV_K - 1) + k), 1.0, 0.0).astype(BF16)
        xc = xc + cw_ref[k:k + 1, :] * jnp.dot(shift, xraw, preferred_element_type=F32)
    xc = xc * _sigmoid(xc)
    xs = xc[:, :SSM_D_INNER]
    bm = xc[:, SSM_D_INNER:SSM_D_INNER + SSM_GROUPS * SSM_STATE]
    cm = xc[:, SSM_D_INNER + SSM_GROUPS * SSM_STATE:]

    dtr = dt_ref[...] + dtb_ref[...]
    dt = jnp.maximum(dtr, 0.0) + jnp.log(1.0 + jnp.exp(-jnp.abs(dtr)))
    da = dt * a_ref[...]
    row = lax.broadcasted_iota(jnp.int32, (L, L), 0)
    col = lax.broadcasted_iota(jnp.int32, (L, L), 1)
    causal = col <= row
    a_cum = sum(jnp.dot(jnp.where(causal, 1.0, 0.0).astype(BF16), part, preferred_element_type=F32)
                for part in _split3(da))
    a_cum_t = a_cum.T

    expand = e_ref[...]
    per_head = jnp.concatenate([dt, jnp.exp(a_cum), jnp.exp(a_cum[L - 1:L, :] - a_cum)], axis=0)
    full = sum(jnp.dot(part, expand, preferred_element_type=F32) for part in _split3(per_head))
    dt_full, ea_full, dend_full = full[:L], full[L:2 * L], full[2 * L:]
    cdec_full = ea_full[L - 1:L, :]

    xdt = xs * dt_full
    xdt_b = xdt.astype(BF16)
    xw_b = (xdt * dend_full).astype(BF16)

    y_parts = []
    for g in range(SSM_GROUPS):
        cg = cm[:, g * SSM_STATE:(g + 1) * SSM_STATE].astype(BF16)
        bg_t = bm[:, g * SSM_STATE:(g + 1) * SSM_STATE].T.astype(BF16)
        cbm = jnp.dot(cg, bg_t, preferred_element_type=F32)
        h_in = h_ref[:, g * gw:(g + 1) * gw]
        y_off = jnp.dot(cg, h_in.astype(BF16), preferred_element_type=F32)
        states = jnp.dot(bg_t, xw_b[:, g * gw:(g + 1) * gw], preferred_element_type=F32)
        h_ref[:, g * gw:(g + 1) * gw] = h_in * cdec_full[:, g * gw:(g + 1) * gw] + states
        yd = []
        for r in range(hpg):
            hh = g * hpg + r
            seg = a_cum[:, hh:hh + 1] - a_cum_t[hh:hh + 1, :]
            dec = jnp.exp(jnp.where(causal, seg, -jnp.inf))
            yd.append(jnp.dot((cbm * dec).astype(BF16), xdt_b[:, hh * SSM_HEADDIM:(hh + 1) * SSM_HEADDIM],
                              preferred_element_type=F32))
        y_g = (jnp.concatenate(yd, axis=-1) + y_off * ea_full[:, g * gw:(g + 1) * gw]
               + xs[:, g * gw:(g + 1) * gw] * dsk_ref[:, g * gw:(g + 1) * gw])
        zg = z_ref[:, g * gw:(g + 1) * gw].astype(F32)
        y_parts.append(_rms(y_g * (zg * _sigmoid(zg)), 1e-5))
    o_ref[...] = (jnp.concatenate(y_parts, axis=-1) * nw_ref[...]).astype(o_ref.dtype)


def _mamba2_ssd(proj3, dt3, conv_w, conv_b, dt_bias, a_log, d_skip, norm_w):
    b, s, _ = proj3.shape
    L = SSM_CHUNK
    di, cd = SSM_D_INNER, SSM_CONV_DIM
    pad_h = lambda a: jnp.pad(a.astype(F32), (0, LANES - SSM_HEADS)).reshape(1, LANES)
    head_of_channel = jnp.arange(di, dtype=jnp.int32) // SSM_HEADDIM
    expand = (jnp.arange(LANES, dtype=jnp.int32)[:, None] == head_of_channel[None, :]).astype(BF16)
    const = lambda shape: pl.BlockSpec(shape, lambda bb, c: (0, 0))
    return pl.pallas_call(
        functools.partial(_ssd_kernel, chunk=L),
        grid=(b, s // L),
        in_specs=[pl.BlockSpec((None, L, di), lambda bb, c: (bb, c, COL_Z // di)),
                  pl.BlockSpec((None, L, cd), lambda bb, c: (bb, c, COL_XBC // cd)),
                  pl.BlockSpec((None, L, LANES), lambda bb, c: (bb, c, 0)),
                  const((SSM_CONV_K, cd)), const((1, cd)), const((1, LANES)), const((1, LANES)),
                  const((1, di)), const((1, di)), const((LANES, di))],
        out_specs=pl.BlockSpec((None, L, di), lambda bb, c: (bb, c, 0)),
        out_shape=jax.ShapeDtypeStruct((b, s, di), BF16),
        scratch_shapes=[pltpu.VMEM((SSD_HALO + L, cd), BF16), pltpu.VMEM((SSM_STATE, di), F32)],
        compiler_params=_params("parallel", "arbitrary"),
        name="mamba2_ssd",
    )(proj3, proj3, dt3, conv_w.astype(F32), conv_b.reshape(1, cd).astype(F32), pad_h(dt_bias),
      pad_h(-jnp.exp(a_log.astype(F32))), jnp.repeat(d_skip.astype(F32), SSM_HEADDIM).reshape(1, di),
      norm_w.reshape(1, di).astype(F32), expand)


def _merge_kernel(x_ref, ya_ref, yc_ref, ys_ref, ga_ref, gc_ref, gs_ref, wa_ref, wc_ref, ws_ref, wm_ref, o_ref):
    def branch(y_ref, g_ref, w_ref):
        return _sigmoid(g_ref[...].astype(F32)) * jnp.dot(y_ref[...], w_ref[...], preferred_element_type=F32)

    merged = branch(ya_ref, ga_ref, wa_ref) + branch(yc_ref, gc_ref, wc_ref) + branch(ys_ref, gs_ref, ws_ref)
    o_ref[...] = x_ref[...] + jnp.dot(merged.astype(BF16), wm_ref[...], preferred_element_type=F32)


def _merge(x, ya, yc, ys, proj, wa, wc, ws, wm, tm):
    t, d = x.shape
    w = ya.shape[1]
    rows = lambda width, cb: pl.BlockSpec((tm, width), lambda i: (i, cb))
    const = lambda shape: pl.BlockSpec(shape, lambda i: (0, 0))
    return pl.pallas_call(
        _merge_kernel,
        grid=(t // tm,),
        in_specs=[rows(d, 0), rows(w, 0), rows(w, 0), rows(w, 0),
                  rows(d, COL_G // d), rows(d, COL_G // d + 1), rows(d, COL_G // d + 2),
                  const((w, d)), const((w, d)), const((w, d)), const((d, d))],
        out_specs=rows(d, 0),
        out_shape=jax.ShapeDtypeStruct((t, d), F32),
        compiler_params=_params("parallel"),
        name="gated_merge",
    )(x, ya, yc, ys, proj, proj, proj, wa, wc, ws, wm)


def _ffn_kernel(x_ref, g_ref, wg_ref, wu_ref, wd_ref, o_ref, h_ref, acc_ref):
    j = pl.program_id(1)

    @pl.when(j == 0)
    def _():
        h_ref[...] = (_rms(x_ref[...], NORM_EPS) * g_ref[...]).astype(h_ref.dtype)
        acc_ref[...] = jnp.zeros(acc_ref.shape, F32)

    h = h_ref[...]
    a = jnp.dot(h, wg_ref[...], preferred_element_type=F32)
    u = jnp.dot(h, wu_ref[...], preferred_element_type=F32)
    acc_ref[...] += jnp.dot((a * _sigmoid(a) * u).astype(BF16), wd_ref[...], preferred_element_type=F32)

    @pl.when(j == pl.num_programs(1) - 1)
    def _():
        o_ref[...] = x_ref[...] + acc_ref[...]


def _ffn_dense(x, g, wg, wu, wd, tm, tf):
    t, d = x.shape
    f = wg.shape[1]
    return pl.pallas_call(
        _ffn_kernel,
        grid=(t // tm, f // tf),
        in_specs=[pl.BlockSpec((tm, d), lambda i, j: (i, 0)),
                  pl.BlockSpec((1, d), lambda i, j: (0, 0)),
                  pl.BlockSpec((d, tf), lambda i, j: (0, j)),
                  pl.BlockSpec((d, tf), lambda i, j: (0, j)),
                  pl.BlockSpec((tf, d), lambda i, j: (j, 0))],
        out_specs=pl.BlockSpec((tm, d), lambda i, j: (i, 0)),
        out_shape=jax.ShapeDtypeStruct((t, d), F32),
        scratch_shapes=[pltpu.VMEM((tm, d), BF16), pltpu.VMEM((tm, d), F32)],
        compiler_params=_params("parallel", "arbitrary"),
        name="ffn_dense",
    )(x, g.reshape(1, d), wg, wu, wd)


def _router_kernel(x_ref, g_ref, wr_ref, br_ref, h_ref, idx_ref, wt_ref, rank_ref, cnt_ref, tri_ref, base_ref):
    i = pl.program_id(0)
    tm = x_ref.shape[0]

    @pl.when(i == 0)
    def _():
        r = lax.broadcasted_iota(jnp.int32, (tm, tm), 0)
        c = lax.broadcasted_iota(jnp.int32, (tm, tm), 1)
        tri_ref[...] = jnp.where(c < r, 1.0, 0.0).astype(BF16)
        base_ref[...] = jnp.zeros(base_ref.shape, F32)

    h = _rms(x_ref[...], NORM_EPS) * g_ref[...]
    h_ref[...] = h.astype(h_ref.dtype)
    logits = jnp.dot(h, wr_ref[...], precision=HIGHEST, preferred_element_type=F32) + br_ref[...]
    lane = lax.broadcasted_iota(jnp.int32, logits.shape, 1)
    logits = jnp.where(lane < N_EXPERTS, logits, -jnp.inf)
    m1 = jnp.max(logits, axis=-1, keepdims=True)
    i1 = jnp.min(jnp.where(logits == m1, lane, LANES), axis=-1, keepdims=True)
    rest = jnp.where(lane == i1, -jnp.inf, logits)
    m2 = jnp.max(rest, axis=-1, keepdims=True)
    i2 = jnp.min(jnp.where(rest == m2, lane, LANES), axis=-1, keepdims=True)
    e = jnp.exp(m2 - m1)
    w1 = 1.0 / (1.0 + e)
    idx_ref[...] = jnp.where(lane == 0, i1, i2)[:, :TOP_K]
    wt_ref[...] = jnp.where(lane == 0, w1, e * w1)[:, :TOP_K]

    chosen = jnp.logical_or(lane == i1, lane == i2)
    before = jnp.dot(tri_ref[...], jnp.where(chosen, 1.0, 0.0).astype(BF16), preferred_element_type=F32)
    before = before + base_ref[...]
    r1 = jnp.sum(jnp.where(lane == i1, before, 0.0), axis=-1, keepdims=True)
    r2 = jnp.sum(jnp.where(lane == i2, before, 0.0), axis=-1, keepdims=True)
    rank_ref[...] = jnp.where(lane == 0, r1, r2)[:, :TOP_K].astype(jnp.int32)
    base_ref[...] += jnp.sum(jnp.where(chosen, 1.0, 0.0), axis=0, keepdims=True)
    cnt_ref[...] = base_ref[...].astype(jnp.int32)


def _router(x, g, w_router, b_router, tm):
    t, d = x.shape
    wr = jnp.pad(w_router.astype(F32), ((0, 0), (0, LANES - N_EXPERTS)))
    br = jnp.pad(b_router.astype(F32), (0, LANES - N_EXPERTS)).reshape(1, LANES)
    pair = pl.BlockSpec((tm, TOP_K), lambda i: (i, 0))
    return pl.pallas_call(
        _router_kernel,
        grid=(t // tm,),
        in_specs=[pl.BlockSpec((tm, d), lambda i: (i, 0)),
                  pl.BlockSpec((1, d), lambda i: (0, 0)),
                  pl.BlockSpec((d, LANES), lambda i: (0, 0)),
                  pl.BlockSpec((1, LANES), lambda i: (0, 0))],
        out_specs=[pl.BlockSpec((tm, d), lambda i: (i, 0)), pair, pair, pair,
                   pl.BlockSpec((1, LANES), lambda i: (0, 0))],
        out_shape=[jax.ShapeDtypeStruct((t, d), BF16),
                   jax.ShapeDtypeStruct((t, TOP_K), jnp.int32),
                   jax.ShapeDtypeStruct((t, TOP_K), F32),
                   jax.ShapeDtypeStruct((t, TOP_K), jnp.int32),
                   jax.ShapeDtypeStruct((1, LANES), jnp.int32)],
        scratch_shapes=[pltpu.VMEM((tm, tm), BF16), pltpu.VMEM((1, LANES), F32)],
        compiler_params=_params("arbitrary"),
        name="moe_router",
    )(x, g.reshape(1, d), wr, br)


def _expert_kernel(be_ref, nu_ref, x_ref, wg_ref, wu_ref, wd_ref, o_ref, acc_ref):
    i = pl.program_id(0)
    j = pl.program_id(1)
    last = pl.num_programs(1) - 1
    used = i < nu_ref[0]

    @pl.when(jnp.logical_and(used, j == 0))
    def _():
        acc_ref[...] = jnp.zeros(acc_ref.shape, F32)

    @pl.when(used)
    def _():
        x = x_ref[...]
        a = jnp.dot(x, wg_ref[...], preferred_element_type=F32)
        u = jnp.dot(x, wu_ref[...], preferred_element_type=F32)
        acc_ref[...] += jnp.dot((a * _sigmoid(a) * u).astype(BF16), wd_ref[...], preferred_element_type=F32)

    @pl.when(jnp.logical_and(used, j == last))
    def _():
        o_ref[...] = acc_ref[...].astype(o_ref.dtype)

    @pl.when(jnp.logical_and(jnp.logical_not(used), j == last))
    def _():
        o_ref[...] = jnp.zeros(o_ref.shape, o_ref.dtype)


def _expert_ffn(xs, block_exp, n_used, wg, wu, wd, bm, tf):
    n_rows, d = xs.shape
    f = wg.shape[2]
    nj = f // tf

    def jj(i, j, nu):
        return jnp.where(i < nu[0], j, nj - 1)

    grid_spec = pltpu.PrefetchScalarGridSpec(
        num_scalar_prefetch=2,
        grid=(n_rows // bm, nj),
        in_specs=[pl.BlockSpec((bm, d), lambda i, j, be, nu: (i, 0)),
                  pl.BlockSpec((None, d, tf), lambda i, j, be, nu: (be[i], 0, jj(i, j, nu))),
                  pl.BlockSpec((None, d, tf), lambda i, j, be, nu: (be[i], 0, jj(i, j, nu))),
                  pl.BlockSpec((None, tf, d), lambda i, j, be, nu: (be[i], jj(i, j, nu), 0))],
        out_specs=pl.BlockSpec((bm, d), lambda i, j, be, nu: (i, 0)),
        scratch_shapes=[pltpu.VMEM((bm, d), F32)],
    )
    return pl.pallas_call(
        _expert_kernel,
        grid_spec=grid_spec,
        out_shape=jax.ShapeDtypeStruct((n_rows, d), BF16),
        compiler_params=_params("parallel", "arbitrary"),
        name="moe_experts",
    )(block_exp, n_used, xs, wg, wu, wd)


def _route(top_idx, rank, counts, bm):
    t = top_idx.shape[0]
    tk = t * TOP_K
    padded = (counts + bm - 1) // bm * bm
    end_padded = jnp.cumsum(padded)
    start_padded = end_padded - padded
    experts = jnp.arange(N_EXPERTS, dtype=jnp.int32)
    dest = rank + jnp.sum(jnp.where(top_idx[..., None] == experts, start_padded, 0), axis=-1)
    n_blocks = tk // bm + N_EXPERTS
    flat_tok = jnp.repeat(jnp.arange(t, dtype=jnp.int32), TOP_K)
    row_tok = jnp.zeros((n_blocks * bm,), jnp.int32).at[dest.reshape(tk)].set(flat_tok, unique_indices=True)
    block_start = jnp.arange(n_blocks, dtype=jnp.int32) * bm
    block_exp = jnp.minimum(jnp.sum(block_start[:, None] >= end_padded[None, :], axis=1), N_EXPERTS - 1)
    n_used = (end_padded[-1] // bm).reshape(1)
    return row_tok, dest, block_exp.astype(jnp.int32), n_used.astype(jnp.int32)


def _combine_kernel(x_ref, a_ref, b_ref, w_ref, g_ref, o_ref, *, final_norm):
    w = w_ref[...]
    y = x_ref[...] + w[:, 0:1] * a_ref[...].astype(F32) + w[:, 1:2] * b_ref[...].astype(F32)
    o_ref[...] = _rms(y, NORM_EPS) * g_ref[...] if final_norm else y


def _combine(x, a, b, w, g, tm):
    t, d = x.shape
    spec = pl.BlockSpec((tm, d), lambda i: (i, 0))
    final_norm = g is not None
    gain = g.reshape(1, d) if final_norm else jnp.ones((1, d), F32)
    return pl.pallas_call(
        functools.partial(_combine_kernel, final_norm=final_norm), grid=(t // tm,),
        in_specs=[spec, spec, spec, pl.BlockSpec((tm, TOP_K), lambda i: (i, 0)),
                  pl.BlockSpec((1, d), lambda i: (0, 0))],
        out_specs=spec,
        out_shape=jax.ShapeDtypeStruct((t, d), F32), compiler_params=_params("parallel"),
        name="moe_combine",
    )(x, a, b, w, gain)


def _moe(x, g, w_router, b_router, wg, wu, wd, final_g, tm, bm, tf):
    h, top_idx, top_w, rank, counts = _router(x, g, w_router, b_router, tm)
    row_tok, dest, block_exp, n_used = _route(top_idx, rank, counts[0, :N_EXPERTS], bm)
    xs = jnp.take(h, row_tok, axis=0)
    ys = _expert_ffn(xs, block_exp, n_used, wg, wu, wd, bm, tf)
    return _combine(x, jnp.take(ys, dest[:, 0], axis=0), jnp.take(ys, dest[:, 1], axis=0), top_w, final_g, tm)


def _final_norm_kernel(x_ref, g_ref, o_ref):
    o_ref[...] = _rms(x_ref[...], NORM_EPS) * g_ref[...]


def _final_norm(x, g, tm):
    t, d = x.shape
    return pl.pallas_call(
        _final_norm_kernel, grid=(t // tm,),
        in_specs=[pl.BlockSpec((tm, d), lambda i: (i, 0)), pl.BlockSpec((1, d), lambda i: (0, 0))],
        out_specs=pl.BlockSpec((tm, d), lambda i: (i, 0)),
        out_shape=jax.ShapeDtypeStruct((t, d), F32), compiler_params=_params("parallel"),
        name="final_norm",
    )(x, g.reshape(1, d))


def _tiles(t, s):
    pick = lambda n, pref: pref if n % pref == 0 else n
    return dict(tm_proj=pick(t, 1024), tn_proj=1024, tb_attn=pick(s, 512), ts_conv=pick(s, 512),
                tm_merge=pick(t, 512), tm_ffn=pick(t, 512), tm_row=pick(t, 1024), bm_moe=pick(t, 1024))


def kernel(x, ln_mix_w, w_in, attn_lq1, attn_lk1, attn_lq2, attn_lk2, attn_subln_w, attn_w_o, conv_dw_w, conv_dw_b, conv_ln_w, conv_ln_b, conv_w_pw, ssm_conv_w, ssm_conv_b, ssm_dt_bias, ssm_A_log, ssm_D, ssm_norm_w, ssm_w_out, w_mix_out, ln_ffn_w, ffn_w_gate, ffn_w_up, ffn_w_down, moe_w_router, moe_b_router, moe_w_gate, moe_w_up, moe_w_down, ln_final_w):
    b, s, d = x.shape
    t = b * s
    depth = w_in.shape[0]
    tl = _tiles(t, s)
    xf = x.reshape(t, d)
    for l in range(depth):
        w_main = jnp.concatenate([w_in[l][:, :COL_DT], w_in[l][:, COL_DT + SSM_HEADS:]], axis=1).astype(BF16)
        w_dt = jnp.pad(w_in[l][:, COL_DT:COL_DT + SSM_HEADS], ((0, 0), (0, LANES - SSM_HEADS))).astype(BF16)
        proj = _norm_matmul(xf, ln_mix_w[l], w_main, BF16, tl["tm_proj"], tl["tn_proj"])
        dt = _norm_matmul(xf, ln_mix_w[l], w_dt, F32, tl["tm_proj"], LANES)
        proj3 = proj.reshape(b, s, PROJ_COLS)

        lam_init = 0.8 - 0.6 * math.exp(-0.3 * l)
        lam = (jnp.exp(jnp.sum(attn_lq1[l].astype(F32) * attn_lk1[l].astype(F32)))
               - jnp.exp(jnp.sum(attn_lq2[l].astype(F32) * attn_lk2[l].astype(F32))) + lam_init)
        y_att = _diff_attention(proj3, lam, attn_subln_w[l], lam_init, tl["tb_attn"])
        y_conv = _conformer_conv(proj3, conv_dw_w[l], conv_dw_b[l], conv_ln_w[l], conv_ln_b[l], tl["ts_conv"])
        y_ssm = _mamba2_ssd(proj3, dt.reshape(b, s, LANES), ssm_conv_w[l], ssm_conv_b[l], ssm_dt_bias[l],
                            ssm_A_log[l], ssm_D[l], ssm_norm_w[l])
        xf = _merge(xf, y_att.reshape(t, ATT_WIDTH), y_conv.reshape(t, CONV_CH), y_ssm.reshape(t, SSM_D_INNER),
                    proj, attn_w_o[l].astype(BF16), conv_w_pw[l].astype(BF16), ssm_w_out[l].astype(BF16),
                    w_mix_out[l].astype(BF16), tl["tm_merge"])

        i = l // 2
        if l % 2 == 0:
            f = ffn_w_gate.shape[2]
            tf = f // 2 if (f // 2) % LANES == 0 else f
            xf = _ffn_dense(xf, ln_ffn_w[l], ffn_w_gate[i].astype(BF16), ffn_w_up[i].astype(BF16),
                            ffn_w_down[i].astype(BF16), tl["tm_ffn"], tf)
        else:
            f = moe_w_gate.shape[3]
            tf = 512 if f % 512 == 0 else f
            final_g = ln_final_w if l == depth - 1 else None
            xf = _moe(xf, ln_ffn_w[l], moe_w_router[i], moe_b_router[i], moe_w_gate[i].astype(BF16),
                      moe_w_up[i].astype(BF16), moe_w_down[i].astype(BF16), final_g, tl["tm_row"], tl["bm_moe"], tf)
    if depth % 2 == 1:
        xf = _final_norm(xf, ln_final_w, tl["tm_row"])
    return xf.reshape(b, s, d)
```

```python
import functools
import math

import jax
import jax.numpy as jnp
from jax import lax
from jax.experimental import pallas as pl
from jax.experimental.pallas import tpu as pltpu
from jax.experimental.pallas import tpu_sc as plsc

F32 = jnp.float32
BF16 = jnp.bfloat16
HIGHEST = lax.Precision.HIGHEST

D_MODEL = 1024
ATT_HEADS = 4
ATT_DH = 64
ATT_WIDTH = ATT_HEADS * 2 * ATT_DH
CONV_CH = 512
CONV_K = 31
SSM_D_INNER = 512
SSM_HEADDIM = 64
SSM_HEADS = SSM_D_INNER // SSM_HEADDIM
SSM_GROUPS = 2
SSM_STATE = 128
SSM_CONV_K = 4
SSM_CONV_DIM = SSM_D_INNER + 2 * SSM_GROUPS * SSM_STATE
SSM_CHUNK = 128
N_EXPERTS = 8
TOP_K = 2
NORM_EPS = 1e-6

LANES = 128
SUBLANES = 8
VMEM_LIMIT_BYTES = 48 * 1024 * 1024

COL_Q = 0
COL_K = COL_Q + ATT_WIDTH
COL_V = COL_K + ATT_WIDTH
COL_U = COL_V + ATT_WIDTH
COL_Z = COL_U + 2 * CONV_CH
COL_XBC = COL_Z + SSM_D_INNER
COL_DT = COL_XBC + SSM_CONV_DIM
COL_G = COL_XBC + SSM_CONV_DIM
PROJ_COLS = COL_G + 3 * D_MODEL


def _params(*semantics):
    return pltpu.CompilerParams(dimension_semantics=semantics, vmem_limit_bytes=VMEM_LIMIT_BYTES)


def _sigmoid(x):
    return 1.0 / (1.0 + jnp.exp(-x))


def _rms(x, eps):
    return x * lax.rsqrt(jnp.mean(x * x, axis=-1, keepdims=True) + eps)


def _split3(x):
    hi = x.astype(BF16)
    rest = x - hi.astype(F32)
    mid = rest.astype(BF16)
    return hi, mid, (rest - mid.astype(F32)).astype(BF16)


def _pack_halves(x):
    n = x.shape[1] // 2
    bits = lambda v: lax.bitcast_convert_type(v.astype(BF16).astype(F32), jnp.uint32)
    return (bits(x[:, :n]) >> 16) | bits(x[:, n:])


def _unpack_halves(w):
    lo = lax.bitcast_convert_type(w << 16, F32)
    hi = lax.bitcast_convert_type(w & jnp.uint32(0xFFFF0000), F32)
    return jnp.concatenate([lo, hi], axis=1)


SC_GATHER_ROWS = 64


def _gather_rows(table, idx):
    n_idx = idx.shape[0]
    width = table.shape[1]
    info = plsc.get_sparse_core_info()
    n_workers = info.num_cores * info.num_subcores
    per_worker = n_idx // n_workers
    assert n_idx % n_workers == 0 and per_worker % SC_GATHER_ROWS == 0, (n_idx, n_workers)
    mesh = plsc.VectorSubcoreMesh(core_axis_name="c", subcore_axis_name="s")

    @functools.partial(
        pl.kernel, mesh=mesh,
        out_type=jax.ShapeDtypeStruct((n_idx, width), table.dtype),
        scratch_types=[pltpu.VMEM((SC_GATHER_ROWS,), jnp.int32), pltpu.VMEM((SC_GATHER_ROWS, width), table.dtype),
                       pltpu.SemaphoreType.DMA],
    )
    def gather(table_hbm, idx_hbm, out_hbm, idx_v, rows_v, sem):
        worker = lax.axis_index("s") * info.num_cores + lax.axis_index("c")

        @pl.loop(0, per_worker // SC_GATHER_ROWS)
        def _(c):
            off = pl.multiple_of(worker * per_worker + c * SC_GATHER_ROWS, SUBLANES)
            pltpu.sync_copy(idx_hbm.at[pl.ds(off, SC_GATHER_ROWS)], idx_v)
            pltpu.async_copy(table_hbm.at[idx_v], rows_v, sem).wait()
            pltpu.sync_copy(rows_v, out_hbm.at[pl.ds(off, SC_GATHER_ROWS)])

    return gather(table, idx)


def _norm_matmul_kernel(x_ref, g_ref, w_ref, o_ref, h_ref):
    @pl.when(pl.program_id(1) == 0)
    def _():
        h_ref[...] = (_rms(x_ref[...], NORM_EPS) * g_ref[...]).astype(h_ref.dtype)

    o_ref[...] = jnp.dot(h_ref[...], w_ref[...], preferred_element_type=F32).astype(o_ref.dtype)


def _norm_matmul(x, g, w, out_dtype, tm, tn):
    t, d = x.shape
    n = w.shape[1]
    return pl.pallas_call(
        _norm_matmul_kernel,
        grid=(t // tm, n // tn),
        in_specs=[pl.BlockSpec((tm, d), lambda i, j: (i, 0)),
                  pl.BlockSpec((1, d), lambda i, j: (0, 0)),
                  pl.BlockSpec((d, tn), lambda i, j: (0, j))],
        out_specs=pl.BlockSpec((tm, tn), lambda i, j: (i, j)),
        out_shape=jax.ShapeDtypeStruct((t, n), out_dtype),
        scratch_shapes=[pltpu.VMEM((tm, d), BF16)],
        compiler_params=_params("parallel", "arbitrary"),
        name="norm_matmul",
    )(x, g.reshape(1, d), w)


ATT_ONES = 16
LOG2E = 1.4426950408889634


def _attn_kernel(items_ref, lam_ref, q_ref, k_ref, v_ref, sw_ref, o_ref,
                 vt_ref, q2_ref, m_ref, acc_ref, sa_ref, ca_ref, sb_ref, cb_ref, *, tb, n_off, n_diag, out_scale):
    hw = 2 * ATT_DH
    nb = vt_ref.shape[0]

    lane = lax.broadcasted_iota(jnp.int32, (tb, hw), 1)
    for c in range(nb):
        rows = slice(c * tb, (c + 1) * tb)
        vt_ref[c, 0:hw, :] = v_ref[rows, :].astype(F32).T.astype(BF16)
        vt_ref[c, hw:hw + ATT_ONES, :] = jnp.ones((ATT_ONES, tb), BF16)
        q = (q_ref[rows, :].astype(F32) * (ATT_DH ** -0.5 * LOG2E)).astype(BF16)
        zero = jnp.zeros_like(q)
        q2_ref[c, 0:tb, :] = jnp.where(lane < ATT_DH, q, zero)
        q2_ref[c, tb:2 * tb, :] = jnp.where(lane >= ATT_DH, q, zero)
        m_ref[c] = jnp.full(m_ref.shape[1:], -jnp.inf, F32)
        acc_ref[c] = jnp.zeros(acc_ref.shape[1:], F32)

    def fill(buf, t, masked):
        s_ref, cmax_ref = buf
        qi, j = items_ref[0, t], items_ref[1, t]
        k = k_ref[pl.ds(pl.multiple_of(j * tb, tb), tb), :]
        st = lax.dot_general(k, q2_ref[qi], (((1,), (1,)), ((), ())), preferred_element_type=F32)
        if masked:
            r = lax.broadcasted_iota(jnp.int32, st.shape, 0)
            c = lax.broadcasted_iota(jnp.int32, st.shape, 1)
            st = jnp.where(r <= jnp.where(c >= tb, c - tb, c), st, -jnp.inf)
        s_ref[...] = st
        cmax_ref[...] = jnp.max(st, axis=0, keepdims=True)

    def update(buf, t, last):
        s_ref, cmax_ref = buf
        qi, j = items_ref[0, t], items_ref[1, t]
        m_old = m_ref[qi]
        m_new = jnp.maximum(m_old, cmax_ref[...])
        p = jnp.exp2(s_ref[...] - m_new).astype(BF16)
        acc = jnp.exp2(m_old - m_new) * acc_ref[qi] + jnp.dot(vt_ref[j], p, preferred_element_type=F32)
        if not last:
            acc_ref[qi] = acc
            m_ref[qi] = m_new
        else:
            o_t = acc[:hw] * (1.0 / acc[hw:hw + 1])
            o = (o_t[:, :tb] - lam_ref[0] * o_t[:, tb:]).T
            o_ref[pl.ds(pl.multiple_of(qi * tb, tb), tb), :] = (
                _rms(o, 1e-5) * (sw_ref[...] * out_scale)).astype(o_ref.dtype)

    def run(first, count, diag, cur, other):
        trips = (count - 1) // 2

        def body(i, carry):
            t = first + 2 * i
            fill(other, t + 1, diag)
            update(cur, t, diag)
            fill(cur, t + 2, diag)
            update(other, t + 1, diag)
            return carry

        lax.fori_loop(0, trips, body, 0)
        if count - 1 > 2 * trips:
            t = first + 2 * trips
            fill(other, t + 1, diag)
            update(cur, t, diag)
            cur, other = other, cur
        return cur, other

    cur, other = (sa_ref, ca_ref), (sb_ref, cb_ref)
    if n_off:
        fill(cur, 0, False)
        cur, other = run(0, n_off, False, cur, other)
        fill(other, n_off, True)
        update(cur, n_off - 1, False)
        cur, other = other, cur
    else:
        fill(cur, 0, True)
    cur, other = run(n_off, n_diag, True, cur, other)
    update(cur, n_off + n_diag - 1, True)


def _diff_attention(proj3, lam, subln_w, lam_init, tb):
    b, s, _ = proj3.shape
    hw = 2 * ATT_DH
    nb = s // tb
    off = [(qi, j) for qi in range(nb) for j in range(qi)]
    items = jnp.asarray(list(zip(*(off + [(qi, qi) for qi in range(nb)]))), jnp.int32)
    kern = functools.partial(_attn_kernel, tb=tb, n_off=len(off), n_diag=nb, out_scale=1.0 - lam_init)
    seq = lambda col: pl.BlockSpec((None, s, hw), lambda bb, h: (bb, 0, col // hw + h))
    return pl.pallas_call(
        kern,
        grid=(b, ATT_HEADS),
        in_specs=[pl.BlockSpec(memory_space=pltpu.SMEM), pl.BlockSpec(memory_space=pltpu.SMEM),
                  seq(COL_Q), seq(COL_K), seq(COL_V), pl.BlockSpec((1, hw), lambda bb, h: (0, 0))],
        out_specs=pl.BlockSpec((None, s, hw), lambda bb, h: (bb, 0, h)),
        out_shape=jax.ShapeDtypeStruct((b, s, ATT_WIDTH), BF16),
        scratch_shapes=[pltpu.VMEM((nb, hw + ATT_ONES, tb), BF16), pltpu.VMEM((nb, 2 * tb, hw), BF16),
                        pltpu.VMEM((nb, 1, 2 * tb), F32), pltpu.VMEM((nb, hw + ATT_ONES, 2 * tb), F32),
                        pltpu.VMEM((tb, 2 * tb), F32), pltpu.VMEM((1, 2 * tb), F32),
                        pltpu.VMEM((tb, 2 * tb), F32), pltpu.VMEM((1, 2 * tb), F32)],
        compiler_params=_params("parallel", "parallel"),
        name="diff_attention",
    )(items, lam.reshape(1).astype(F32), proj3, proj3, proj3, subln_w.reshape(1, hw).astype(F32))


CONV_HALO = 32
CONV_ROWS = 64


def _cconv_kernel(a_ref, g_ref, w_ref, b_ref, lnw_ref, lnb_ref, o_ref, buf_ref, *, ts):
    i = pl.program_id(1)

    @pl.when(i == 0)
    def _():
        buf_ref[0:CONV_HALO, :] = jnp.zeros((CONV_HALO, CONV_CH), F32)

    @pl.when(i > 0)
    def _():
        buf_ref[0:CONV_HALO, :] = buf_ref[ts:ts + CONV_HALO, :]

    buf_ref[CONV_HALO:CONV_HALO + ts, :] = a_ref[...].astype(F32) * _sigmoid(g_ref[...].astype(F32))

    first = CONV_HALO - (CONV_K - 1)
    for c in range(ts // CONV_ROWS):
        base = c * CONV_ROWS
        acc = jnp.zeros((CONV_ROWS, CONV_CH), F32) + b_ref[...]
        for res in range(SUBLANES):
            taps = [k for k in range(CONV_K) if (first + k) % SUBLANES == res]
            rows = CONV_ROWS + (SUBLANES if res else 0)
            part = None
            for k in taps:
                lo = base + first + k - res
                term = w_ref[k:k + 1, :] * buf_ref[lo:lo + rows, :]
                part = term if part is None else part + term
            acc = acc + part[res:res + CONV_ROWS]
        mu = jnp.mean(acc, axis=-1, keepdims=True)
        xc = acc - mu
        var = jnp.mean(xc * xc, axis=-1, keepdims=True)
        y = xc * lax.rsqrt(var + 1e-5) * lnw_ref[...] + lnb_ref[...]
        o_ref[base:base + CONV_ROWS, :] = (y * _sigmoid(y)).astype(o_ref.dtype)


def _conformer_conv(proj3, dw_w, dw_b, ln_w, ln_b, ts):
    b, s, _ = proj3.shape
    c = CONV_CH
    row = lambda a: a.reshape(1, c).astype(F32)
    return pl.pallas_call(
        functools.partial(_cconv_kernel, ts=ts),
        grid=(b, s // ts),
        in_specs=[pl.BlockSpec((None, ts, c), lambda bb, i: (bb, i, COL_U // c)),
                  pl.BlockSpec((None, ts, c), lambda bb, i: (bb, i, COL_U // c + 1)),
                  pl.BlockSpec((CONV_K, c), lambda bb, i: (0, 0)),
                  pl.BlockSpec((1, c), lambda bb, i: (0, 0)),
                  pl.BlockSpec((1, c), lambda bb, i: (0, 0)),
                  pl.BlockSpec((1, c), lambda bb, i: (0, 0))],
        out_specs=pl.BlockSpec((None, ts, c), lambda bb, i: (bb, i, 0)),
        out_shape=jax.ShapeDtypeStruct((b, s, c), BF16),
        scratch_shapes=[pltpu.VMEM((CONV_HALO + ts, c), F32)],
        compiler_params=_params("parallel", "arbitrary"),
        name="conformer_conv",
    )(proj3, proj3, dw_w.astype(F32), row(dw_b), row(ln_w), row(ln_b))


SSD_HALO = 16


def _ssd_kernel(z_ref, xbc_ref, dt_ref, cw_ref, cb_ref, dtb_ref, a_ref, dsk_ref, nw_ref, e_ref,
                o_ref, buf_ref, h_ref, *, chunk):
    L = chunk
    ci = pl.program_id(1)
    gw = SSM_D_INNER // SSM_GROUPS
    hpg = SSM_HEADS // SSM_GROUPS

    @pl.when(ci == 0)
    def _():
        buf_ref[0:SSD_HALO, :] = jnp.zeros((SSD_HALO, SSM_CONV_DIM), BF16)
        h_ref[...] = jnp.zeros(h_ref.shape, F32)

    @pl.when(ci > 0)
    def _():
        buf_ref[0:SSD_HALO, :] = buf_ref[L:L + SSD_HALO, :]

    buf_ref[SSD_HALO:SSD_HALO + L, :] = xbc_ref[...]
    xraw = buf_ref[...]
    r = lax.broadcasted_iota(jnp.int32, (L, SSD_HALO + L), 0)
    c = lax.broadcasted_iota(jnp.int32, (L, SSD_HALO + L), 1)
    xc = jnp.zeros((L, SSM_CONV_DIM), F32) + cb_ref[...]
    for k in range(SSM_CONV_K):
        shift = jnp.where(c == r + (SSD_HALO - (SSM_CONV_K - 1) + k), 1.0, 0.0).astype(BF16)
        xc = xc + cw_ref[k:k + 1, :] * jnp.dot(shift, xraw, preferred_element_type=F32)
    xc = xc * _sigmoid(xc)
    xs = xc[:, :SSM_D_INNER]
    bm = xc[:, SSM_D_INNER:SSM_D_INNER + SSM_GROUPS * SSM_STATE]
    cm = xc[:, SSM_D_INNER + SSM_GROUPS * SSM_STATE:]

    dtr = dt_ref[...] + dtb_ref[...]
    dt = jnp.maximum(dtr, 0.0) + jnp.log(1.0 + jnp.exp(-jnp.abs(dtr)))
    da = dt * a_ref[...]
    row = lax.broadcasted_iota(jnp.int32, (L, L), 0)
    col = lax.broadcasted_iota(jnp.int32, (L, L), 1)
    causal = col <= row
    a_cum = sum(jnp.dot(jnp.where(causal, 1.0, 0.0).astype(BF16), part, preferred_element_type=F32)
                for part in _split3(da))
    a_cum_t = a_cum.T

    expand = e_ref[...]
    per_head = jnp.concatenate([dt, jnp.exp(a_cum), jnp.exp(a_cum[L - 1:L, :] - a_cum)], axis=0)
    full = sum(jnp.dot(part, expand, preferred_element_type=F32) for part in _split3(per_head))
    dt_full, ea_full, dend_full = full[:L], full[L:2 * L], full[2 * L:]
    cdec_full = ea_full[L - 1:L, :]

    xdt = xs * dt_full
    xdt_b = xdt.astype(BF16)
    xw_b = (xdt * dend_full).astype(BF16)

    y_parts = []
    for g in range(SSM_GROUPS):
        cg = cm[:, g * SSM_STATE:(g + 1) * SSM_STATE].astype(BF16)
        bg_t = bm[:, g * SSM_STATE:(g + 1) * SSM_STATE].T.astype(BF16)
        cbm = jnp.dot(cg, bg_t, preferred_element_type=F32)
        h_in = h_ref[:, g * gw:(g + 1) * gw]
        y_off = jnp.dot(cg, h_in.astype(BF16), preferred_element_type=F32)
        states = jnp.dot(bg_t, xw_b[:, g * gw:(g + 1) * gw], preferred_element_type=F32)
        h_ref[:, g * gw:(g + 1) * gw] = h_in * cdec_full[:, g * gw:(g + 1) * gw] + states
        yd = []
        for r in range(hpg):
            hh = g * hpg + r
            seg = a_cum[:, hh:hh + 1] - a_cum_t[hh:hh + 1, :]
            dec = jnp.exp(jnp.where(causal, seg, -jnp.inf))
            yd.append(jnp.dot((cbm * dec).astype(BF16), xdt_b[:, hh * SSM_HEADDIM:(hh + 1) * SSM_HEADDIM],
                              preferred_element_type=F32))
        y_g = (jnp.concatenate(yd, axis=-1) + y_off * ea_full[:, g * gw:(g + 1) * gw]
               + xs[:, g * gw:(g + 1) * gw] * dsk_ref[:, g * gw:(g + 1) * gw])
        zg = z_ref[:, g * gw:(g + 1) * gw].astype(F32)
        y_parts.append(_rms(y_g * (zg * _sigmoid(zg)), 1e-5))
    o_ref[...] = (jnp.concatenate(y_parts, axis=-1) * nw_ref[...]).astype(o_ref.dtype)


def _mamba2_ssd(proj3, dt3, conv_w, conv_b, dt_bias, a_log, d_skip, norm_w):
    b, s, _ = proj3.shape
    L = SSM_CHUNK
    di, cd = SSM_D_INNER, SSM_CONV_DIM
    pad_h = lambda a: jnp.pad(a.astype(F32), (0, LANES - SSM_HEADS)).reshape(1, LANES)
    head_of_channel = jnp.arange(di, dtype=jnp.int32) // SSM_HEADDIM
    expand = (jnp.arange(LANES, dtype=jnp.int32)[:, None] == head_of_channel[None, :]).astype(BF16)
    const = lambda shape: pl.BlockSpec(shape, lambda bb, c: (0, 0))
    return pl.pallas_call(
        functools.partial(_ssd_kernel, chunk=L),
        grid=(b, s // L),
        in_specs=[pl.BlockSpec((None, L, di), lambda bb, c: (bb, c, COL_Z // di)),
                  pl.BlockSpec((None, L, cd), lambda bb, c: (bb, c, COL_XBC // cd)),
                  pl.BlockSpec((None, L, LANES), lambda bb, c: (bb, c, 0)),
                  const((SSM_CONV_K, cd)), const((1, cd)), const((1, LANES)), const((1, LANES)),
                  const((1, di)), const((1, di)), const((LANES, di))],
        out_specs=pl.BlockSpec((None, L, di), lambda bb, c: (bb, c, 0)),
        out_shape=jax.ShapeDtypeStruct((b, s, di), BF16),
        scratch_shapes=[pltpu.VMEM((SSD_HALO + L, cd), BF16), pltpu.VMEM((SSM_STATE, di), F32)],
        compiler_params=_params("parallel", "arbitrary"),
        name="mamba2_ssd",
    )(proj3, proj3, dt3, conv_w.astype(F32), conv_b.reshape(1, cd).astype(F32), pad_h(dt_bias),
      pad_h(-jnp.exp(a_log.astype(F32))), jnp.repeat(d_skip.astype(F32), SSM_HEADDIM).reshape(1, di),
      norm_w.reshape(1, di).astype(F32), expand)


def _merge_kernel(x_ref, ya_ref, yc_ref, ys_ref, ga_ref, gc_ref, gs_ref, wa_ref, wc_ref, ws_ref, wm_ref, o_ref):
    def branch(y_ref, g_ref, w_ref):
        return _sigmoid(g_ref[...].astype(F32)) * jnp.dot(y_ref[...], w_ref[...], preferred_element_type=F32)

    merged = branch(ya_ref, ga_ref, wa_ref) + branch(yc_ref, gc_ref, wc_ref) + branch(ys_ref, gs_ref, ws_ref)
    o_ref[...] = x_ref[...] + jnp.dot(merged.astype(BF16), wm_ref[...], preferred_element_type=F32)


def _merge(x, ya, yc, ys, proj, wa, wc, ws, wm, tm):
    t, d = x.shape
    w = ya.shape[1]
    rows = lambda width, cb: pl.BlockSpec((tm, width), lambda i: (i, cb))
    const = lambda shape: pl.BlockSpec(shape, lambda i: (0, 0))
    return pl.pallas_call(
        _merge_kernel,
        grid=(t // tm,),
        in_specs=[rows(d, 0), rows(w, 0), rows(w, 0), rows(w, 0),
                  rows(d, COL_G // d), rows(d, COL_G // d + 1), rows(d, COL_G // d + 2),
                  const((w, d)), const((w, d)), const((w, d)), const((d, d))],
        out_specs=rows(d, 0),
        out_shape=jax.ShapeDtypeStruct((t, d), F32),
        compiler_params=_params("parallel"),
        name="gated_merge",
    )(x, ya, yc, ys, proj, proj, proj, wa, wc, ws, wm)


def _ffn_kernel(x_ref, g_ref, wg_ref, wu_ref, wd_ref, o_ref, h_ref, acc_ref):
    j = pl.program_id(1)

    @pl.when(j == 0)
    def _():
        h_ref[...] = (_rms(x_ref[...], NORM_EPS) * g_ref[...]).astype(h_ref.dtype)
        acc_ref[...] = jnp.zeros(acc_ref.shape, F32)

    h = h_ref[...]
    a = jnp.dot(h, wg_ref[...], preferred_element_type=F32)
    u = jnp.dot(h, wu_ref[...], preferred_element_type=F32)
    acc_ref[...] += jnp.dot((a * _sigmoid(a) * u).astype(BF16), wd_ref[...], preferred_element_type=F32)

    @pl.when(j == pl.num_programs(1) - 1)
    def _():
        o_ref[...] = x_ref[...] + acc_ref[...]


def _ffn_dense(x, g, wg, wu, wd, tm, tf):
    t, d = x.shape
    f = wg.shape[1]
    return pl.pallas_call(
        _ffn_kernel,
        grid=(t // tm, f // tf),
        in_specs=[pl.BlockSpec((tm, d), lambda i, j: (i, 0)),
                  pl.BlockSpec((1, d), lambda i, j: (0, 0)),
                  pl.BlockSpec((d, tf), lambda i, j: (0, j)),
                  pl.BlockSpec((d, tf), lambda i, j: (0, j)),
                  pl.BlockSpec((tf, d), lambda i, j: (j, 0))],
        out_specs=pl.BlockSpec((tm, d), lambda i, j: (i, 0)),
        out_shape=jax.ShapeDtypeStruct((t, d), F32),
        scratch_shapes=[pltpu.VMEM((tm, d), BF16), pltpu.VMEM((tm, d), F32)],
        compiler_params=_params("parallel", "arbitrary"),
        name="ffn_dense",
    )(x, g.reshape(1, d), wg, wu, wd)


def _router_kernel(x_ref, g_ref, wr_ref, br_ref, h_ref, idx_ref, wt_ref, rank_ref, cnt_ref, tri_ref, base_ref):
    i = pl.program_id(0)
    tm = x_ref.shape[0]

    @pl.when(i == 0)
    def _():
        r = lax.broadcasted_iota(jnp.int32, (tm, tm), 0)
        c = lax.broadcasted_iota(jnp.int32, (tm, tm), 1)
        tri_ref[...] = jnp.where(c < r, 1.0, 0.0).astype(BF16)
        base_ref[...] = jnp.zeros(base_ref.shape, F32)

    h = _rms(x_ref[...], NORM_EPS) * g_ref[...]
    h_ref[...] = _pack_halves(h)
    logits = jnp.dot(h, wr_ref[...], precision=HIGHEST, preferred_element_type=F32) + br_ref[...]
    lane = lax.broadcasted_iota(jnp.int32, logits.shape, 1)
    logits = jnp.where(lane < N_EXPERTS, logits, -jnp.inf)
    m1 = jnp.max(logits, axis=-1, keepdims=True)
    i1 = jnp.min(jnp.where(logits == m1, lane, LANES), axis=-1, keepdims=True)
    rest = jnp.where(lane == i1, -jnp.inf, logits)
    m2 = jnp.max(rest, axis=-1, keepdims=True)
    i2 = jnp.min(jnp.where(rest == m2, lane, LANES), axis=-1, keepdims=True)
    e = jnp.exp(m2 - m1)
    w1 = 1.0 / (1.0 + e)
    idx_ref[...] = jnp.where(lane == 0, i1, i2)[:, :TOP_K]
    wt_ref[...] = jnp.where(lane == 0, w1, e * w1)[:, :TOP_K]

    chosen = jnp.logical_or(lane == i1, lane == i2)
    before = jnp.dot(tri_ref[...], jnp.where(chosen, 1.0, 0.0).astype(BF16), preferred_element_type=F32)
    before = before + base_ref[...]
    r1 = jnp.sum(jnp.where(lane == i1, before, 0.0), axis=-1, keepdims=True)
    r2 = jnp.sum(jnp.where(lane == i2, before, 0.0), axis=-1, keepdims=True)
    rank_ref[...] = jnp.where(lane == 0, r1, r2)[:, :TOP_K].astype(jnp.int32)
    base_ref[...] += jnp.sum(jnp.where(chosen, 1.0, 0.0), axis=0, keepdims=True)
    cnt_ref[...] = base_ref[...].astype(jnp.int32)


def _router(x, g, w_router, b_router, tm):
    t, d = x.shape
    wr = jnp.pad(w_router.astype(F32), ((0, 0), (0, LANES - N_EXPERTS)))
    br = jnp.pad(b_router.astype(F32), (0, LANES - N_EXPERTS)).reshape(1, LANES)
    pair = pl.BlockSpec((tm, TOP_K), lambda i: (i, 0))
    return pl.pallas_call(
        _router_kernel,
        grid=(t // tm,),
        in_specs=[pl.BlockSpec((tm, d), lambda i: (i, 0)),
                  pl.BlockSpec((1, d), lambda i: (0, 0)),
                  pl.BlockSpec((d, LANES), lambda i: (0, 0)),
                  pl.BlockSpec((1, LANES), lambda i: (0, 0))],
        out_specs=[pl.BlockSpec((tm, d // 2), lambda i: (i, 0)), pair, pair, pair,
                   pl.BlockSpec((1, LANES), lambda i: (0, 0))],
        out_shape=[jax.ShapeDtypeStruct((t, d // 2), jnp.uint32),
                   jax.ShapeDtypeStruct((t, TOP_K), jnp.int32),
                   jax.ShapeDtypeStruct((t, TOP_K), F32),
                   jax.ShapeDtypeStruct((t, TOP_K), jnp.int32),
                   jax.ShapeDtypeStruct((1, LANES), jnp.int32)],
        scratch_shapes=[pltpu.VMEM((tm, tm), BF16), pltpu.VMEM((1, LANES), F32)],
        compiler_params=_params("arbitrary"),
        name="moe_router",
    )(x, g.reshape(1, d), wr, br)


def _expert_kernel(be_ref, nu_ref, x_ref, wg_ref, wu_ref, wd_ref, o_ref, xb_ref, acc_ref):
    i = pl.program_id(0)
    j = pl.program_id(1)
    last = pl.num_programs(1) - 1
    used = i < nu_ref[0]

    @pl.when(jnp.logical_and(used, j == 0))
    def _():
        xb_ref[...] = _unpack_halves(x_ref[...]).astype(BF16)
        acc_ref[...] = jnp.zeros(acc_ref.shape, F32)

    @pl.when(used)
    def _():
        x = xb_ref[...]
        a = jnp.dot(x, wg_ref[...], preferred_element_type=F32)
        u = jnp.dot(x, wu_ref[...], preferred_element_type=F32)
        acc_ref[...] += jnp.dot((a * _sigmoid(a) * u).astype(BF16), wd_ref[...], preferred_element_type=F32)

    @pl.when(jnp.logical_and(used, j == last))
    def _():
        o_ref[...] = _pack_halves(acc_ref[...])

    @pl.when(jnp.logical_and(jnp.logical_not(used), j == last))
    def _():
        o_ref[...] = jnp.zeros(o_ref.shape, o_ref.dtype)


def _expert_ffn(xs, block_exp, n_used, wg, wu, wd, bm, tf):
    n_rows, half = xs.shape
    d = 2 * half
    f = wg.shape[2]
    nj = f // tf

    def jj(i, j, nu):
        return jnp.where(i < nu[0], j, nj - 1)

    grid_spec = pltpu.PrefetchScalarGridSpec(
        num_scalar_prefetch=2,
        grid=(n_rows // bm, nj),
        in_specs=[pl.BlockSpec((bm, half), lambda i, j, be, nu: (i, 0)),
                  pl.BlockSpec((None, d, tf), lambda i, j, be, nu: (be[i], 0, jj(i, j, nu))),
                  pl.BlockSpec((None, d, tf), lambda i, j, be, nu: (be[i], 0, jj(i, j, nu))),
                  pl.BlockSpec((None, tf, d), lambda i, j, be, nu: (be[i], jj(i, j, nu), 0))],
        out_specs=pl.BlockSpec((bm, half), lambda i, j, be, nu: (i, 0)),
        scratch_shapes=[pltpu.VMEM((bm, d), BF16), pltpu.VMEM((bm, d), F32)],
    )
    return pl.pallas_call(
        _expert_kernel,
        grid_spec=grid_spec,
        out_shape=jax.ShapeDtypeStruct((n_rows, half), jnp.uint32),
        compiler_params=_params("parallel", "arbitrary"),
        name="moe_experts",
    )(block_exp, n_used, xs, wg, wu, wd)


def _route(top_idx, rank, counts, bm):
    t = top_idx.shape[0]
    tk = t * TOP_K
    padded = (counts + bm - 1) // bm * bm
    end_padded = jnp.cumsum(padded)
    start_padded = end_padded - padded
    experts = jnp.arange(N_EXPERTS, dtype=jnp.int32)
    dest = rank + jnp.sum(jnp.where(top_idx[..., None] == experts, start_padded, 0), axis=-1)
    n_blocks = tk // bm + N_EXPERTS
    flat_tok = jnp.repeat(jnp.arange(t, dtype=jnp.int32), TOP_K)
    row_tok = jnp.zeros((n_blocks * bm,), jnp.int32).at[dest.reshape(tk)].set(flat_tok, unique_indices=True)
    block_start = jnp.arange(n_blocks, dtype=jnp.int32) * bm
    block_exp = jnp.minimum(jnp.sum(block_start[:, None] >= end_padded[None, :], axis=1), N_EXPERTS - 1)
    n_used = (end_padded[-1] // bm).reshape(1)
    return row_tok, dest, block_exp.astype(jnp.int32), n_used.astype(jnp.int32)


def _combine_kernel(x_ref, a_ref, b_ref, w_ref, g_ref, o_ref, *, final_norm):
    w = w_ref[...]
    y = x_ref[...] + w[:, 0:1] * _unpack_halves(a_ref[...]) + w[:, 1:2] * _unpack_halves(b_ref[...])
    o_ref[...] = _rms(y, NORM_EPS) * g_ref[...] if final_norm else y


def _combine(x, ab, w, g, tm):
    t, d = x.shape
    spec = pl.BlockSpec((tm, d), lambda i: (i, 0))
    final_norm = g is not None
    gain = g.reshape(1, d) if final_norm else jnp.ones((1, d), F32)
    return pl.pallas_call(
        functools.partial(_combine_kernel, final_norm=final_norm), grid=(t // tm,),
        in_specs=[spec, pl.BlockSpec((tm, d // 2), lambda i: (i, 0)),
                  pl.BlockSpec((tm, d // 2), lambda i: (i + t // tm, 0)),
                  pl.BlockSpec((tm, TOP_K), lambda i: (i, 0)),
                  pl.BlockSpec((1, d), lambda i: (0, 0))],
        out_specs=spec,
        out_shape=jax.ShapeDtypeStruct((t, d), F32), compiler_params=_params("parallel"),
        name="moe_combine",
    )(x, ab, ab, w, gain)


def _moe(x, g, w_router, b_router, wg, wu, wd, final_g, tm, bm, tf):
    h, top_idx, top_w, rank, counts = _router(x, g, w_router, b_router, tm)
    row_tok, dest, block_exp, n_used = _route(top_idx, rank, counts[0, :N_EXPERTS], bm)
    xs = _gather_rows(h, row_tok)
    ys = _expert_ffn(xs, block_exp, n_used, wg, wu, wd, bm, tf)
    picked = _gather_rows(ys, jnp.concatenate([dest[:, 0], dest[:, 1]]))
    return _combine(x, picked, top_w, final_g, tm)


def _final_norm_kernel(x_ref, g_ref, o_ref):
    o_ref[...] = _rms(x_ref[...], NORM_EPS) * g_ref[...]


def _final_norm(x, g, tm):
    t, d = x.shape
    return pl.pallas_call(
        _final_norm_kernel, grid=(t // tm,),
        in_specs=[pl.BlockSpec((tm, d), lambda i: (i, 0)), pl.BlockSpec((1, d), lambda i: (0, 0))],
        out_specs=pl.BlockSpec((tm, d), lambda i: (i, 0)),
        out_shape=jax.ShapeDtypeStruct((t, d), F32), compiler_params=_params("parallel"),
        name="final_norm",
    )(x, g.reshape(1, d))


def _tiles(t, s):
    pick = lambda n, pref: pref if n % pref == 0 else n
    return dict(tm_proj=pick(t, 1024), tn_proj=1024, tb_attn=pick(s, 512), ts_conv=pick(s, 512),
                tm_merge=pick(t, 512), tm_ffn=pick(t, 512), tm_row=pick(t, 1024), bm_moe=pick(t, 1024))


def kernel(x, ln_mix_w, w_in, attn_lq1, attn_lk1, attn_lq2, attn_lk2, attn_subln_w, attn_w_o, conv_dw_w, conv_dw_b, conv_ln_w, conv_ln_b, conv_w_pw, ssm_conv_w, ssm_conv_b, ssm_dt_bias, ssm_A_log, ssm_D, ssm_norm_w, ssm_w_out, w_mix_out, ln_ffn_w, ffn_w_gate, ffn_w_up, ffn_w_down, moe_w_router, moe_b_router, moe_w_gate, moe_w_up, moe_w_down, ln_final_w):
    b, s, d = x.shape
    t = b * s
    depth = w_in.shape[0]
    tl = _tiles(t, s)
    xf = x.reshape(t, d)
    for l in range(depth):
        w_main = jnp.concatenate([w_in[l][:, :COL_DT], w_in[l][:, COL_DT + SSM_HEADS:]], axis=1).astype(BF16)
        w_dt = jnp.pad(w_in[l][:, COL_DT:COL_DT + SSM_HEADS], ((0, 0), (0, LANES - SSM_HEADS))).astype(BF16)
        proj = _norm_matmul(xf, ln_mix_w[l], w_main, BF16, tl["tm_proj"], tl["tn_proj"])
        dt = _norm_matmul(xf, ln_mix_w[l], w_dt, F32, tl["tm_proj"], LANES)
        proj3 = proj.reshape(b, s, PROJ_COLS)

        lam_init = 0.8 - 0.6 * math.exp(-0.3 * l)
        lam = (jnp.exp(jnp.sum(attn_lq1[l].astype(F32) * attn_lk1[l].astype(F32)))
               - jnp.exp(jnp.sum(attn_lq2[l].astype(F32) * attn_lk2[l].astype(F32))) + lam_init)
        y_att = _diff_attention(proj3, lam, attn_subln_w[l], lam_init, tl["tb_attn"])
        y_conv = _conformer_conv(proj3, conv_dw_w[l], conv_dw_b[l], conv_ln_w[l], conv_ln_b[l], tl["ts_conv"])
        y_ssm = _mamba2_ssd(proj3, dt.reshape(b, s, LANES), ssm_conv_w[l], ssm_conv_b[l], ssm_dt_bias[l],
                            ssm_A_log[l], ssm_D[l], ssm_norm_w[l])
        xf = _merge(xf, y_att.reshape(t, ATT_WIDTH), y_conv.reshape(t, CONV_CH), y_ssm.reshape(t, SSM_D_INNER),
                    proj, attn_w_o[l].astype(BF16), conv_w_pw[l].astype(BF16), ssm_w_out[l].astype(BF16),
                    w_mix_out[l].astype(BF16), tl["tm_merge"])

        i = l // 2
        if l % 2 == 0:
            f = ffn_w_gate.shape[2]
            tf = f // 2 if (f // 2) % LANES == 0 else f
            xf = _ffn_dense(xf, ln_ffn_w[l], ffn_w_gate[i].astype(BF16), ffn_w_up[i].astype(BF16),
                            ffn_w_down[i].astype(BF16), tl["tm_ffn"], tf)
        else:
            f = moe_w_gate.shape[3]
            tf = 512 if f % 512 == 0 else f
            final_g = ln_final_w if l == depth - 1 else None
            xf = _moe(xf, ln_ffn_w[l], moe_w_router[i], moe_b_router[i], moe_w_gate[i].astype(BF16),
                      moe_w_up[i].astype(BF16), moe_w_down[i].astype(BF16), final_g, tl["tm_row"], tl["bm_moe"], tf)
    if depth % 2 == 1:
        xf = _final_norm(xf, ln_final_w, tl["tm_row"])
    return xf.reshape(b, s, d)
```

```python
import functools
import math

import jax
import jax.numpy as jnp
from jax import lax
from jax.experimental import pallas as pl
from jax.experimental.pallas import tpu as pltpu
from jax.experimental.pallas import tpu_sc as plsc

F32 = jnp.float32
BF16 = jnp.bfloat16
HIGHEST = lax.Precision.HIGHEST

D_MODEL = 1024
ATT_HEADS = 4
ATT_DH = 64
ATT_WIDTH = ATT_HEADS * 2 * ATT_DH
CONV_CH = 512
CONV_K = 31
SSM_D_INNER = 512
SSM_HEADDIM = 64
SSM_HEADS = SSM_D_INNER // SSM_HEADDIM
SSM_GROUPS = 2
SSM_STATE = 128
SSM_CONV_K = 4
SSM_CONV_DIM = SSM_D_INNER + 2 * SSM_GROUPS * SSM_STATE
SSM_CHUNK = 128
N_EXPERTS = 8
TOP_K = 2
NORM_EPS = 1e-6

LANES = 128
SUBLANES = 8
VMEM_LIMIT_BYTES = 48 * 1024 * 1024

COL_Q = 0
COL_K = COL_Q + ATT_WIDTH
COL_V = COL_K + ATT_WIDTH
COL_U = COL_V + ATT_WIDTH
COL_Z = COL_U + 2 * CONV_CH
COL_XBC = COL_Z + SSM_D_INNER
COL_DT = COL_XBC + SSM_CONV_DIM
COL_G = COL_XBC + SSM_CONV_DIM
PROJ_COLS = COL_G + 3 * D_MODEL


def _params(*semantics):
    return pltpu.CompilerParams(dimension_semantics=semantics, vmem_limit_bytes=VMEM_LIMIT_BYTES)


def _sigmoid(x):
    return 1.0 / (1.0 + jnp.exp(-x))


def _rms(x, eps):
    return x * lax.rsqrt(jnp.mean(x * x, axis=-1, keepdims=True) + eps)


def _split3(x):
    hi = x.astype(BF16)
    rest = x - hi.astype(F32)
    mid = rest.astype(BF16)
    return hi, mid, (rest - mid.astype(F32)).astype(BF16)


def _pack_halves(x):
    n = x.shape[1] // 2
    bits = lambda v: lax.bitcast_convert_type(v.astype(BF16).astype(F32), jnp.uint32)
    return (bits(x[:, :n]) >> 16) | bits(x[:, n:])


def _unpack_halves(w):
    lo = lax.bitcast_convert_type(w << 16, F32)
    hi = lax.bitcast_convert_type(w & jnp.uint32(0xFFFF0000), F32)
    return jnp.concatenate([lo, hi], axis=1)


SC_GATHER_ROWS = 64


def _gather_rows(table, idx):
    n_idx = idx.shape[0]
    width = table.shape[1]
    info = plsc.get_sparse_core_info()
    n_workers = info.num_cores * info.num_subcores
    per_worker = n_idx // n_workers
    assert n_idx % n_workers == 0 and per_worker % SC_GATHER_ROWS == 0, (n_idx, n_workers)
    mesh = plsc.VectorSubcoreMesh(core_axis_name="c", subcore_axis_name="s")

    @functools.partial(
        pl.kernel, mesh=mesh,
        out_type=jax.ShapeDtypeStruct((n_idx, width), table.dtype),
        scratch_types=[pltpu.VMEM((SC_GATHER_ROWS,), jnp.int32), pltpu.VMEM((SC_GATHER_ROWS, width), table.dtype),
                       pltpu.SemaphoreType.DMA],
    )
    def gather(table_hbm, idx_hbm, out_hbm, idx_v, rows_v, sem):
        worker = lax.axis_index("s") * info.num_cores + lax.axis_index("c")

        @pl.loop(0, per_worker // SC_GATHER_ROWS)
        def _(c):
            off = pl.multiple_of(worker * per_worker + c * SC_GATHER_ROWS, SUBLANES)
            pltpu.sync_copy(idx_hbm.at[pl.ds(off, SC_GATHER_ROWS)], idx_v)
            pltpu.async_copy(table_hbm.at[idx_v], rows_v, sem).wait()
            pltpu.sync_copy(rows_v, out_hbm.at[pl.ds(off, SC_GATHER_ROWS)])

    return gather(table, idx)


def _norm_matmul_kernel(x_ref, g_ref, w_ref, o_ref, h_ref):
    @pl.when(pl.program_id(1) == 0)
    def _():
        h_ref[...] = (_rms(x_ref[...], NORM_EPS) * g_ref[...]).astype(h_ref.dtype)

    o_ref[...] = jnp.dot(h_ref[...], w_ref[...], preferred_element_type=F32).astype(o_ref.dtype)


def _norm_matmul(x, g, w, out_dtype, tm, tn):
    t, d = x.shape
    n = w.shape[1]
    return pl.pallas_call(
        _norm_matmul_kernel,
        grid=(t // tm, n // tn),
        in_specs=[pl.BlockSpec((tm, d), lambda i, j: (i, 0)),
                  pl.BlockSpec((1, d), lambda i, j: (0, 0)),
                  pl.BlockSpec((d, tn), lambda i, j: (0, j))],
        out_specs=pl.BlockSpec((tm, tn), lambda i, j: (i, j)),
        out_shape=jax.ShapeDtypeStruct((t, n), out_dtype),
        scratch_shapes=[pltpu.VMEM((tm, d), BF16)],
        compiler_params=_params("parallel", "arbitrary"),
        name="norm_matmul",
    )(x, g.reshape(1, d), w)


ATT_ONES = 16
LOG2E = 1.4426950408889634


def _attn_kernel(items_ref, lam_ref, q_ref, k_ref, v_ref, sw_ref, o_ref,
                 vt_ref, q2_ref, m_ref, acc_ref, sa_ref, ca_ref, sb_ref, cb_ref, *, tb, n_off, n_diag, out_scale):
    hw = 2 * ATT_DH
    nb = vt_ref.shape[0]

    lane = lax.broadcasted_iota(jnp.int32, (tb, hw), 1)
    for c in range(nb):
        rows = slice(c * tb, (c + 1) * tb)
        vt_ref[c, 0:hw, :] = v_ref[rows, :].astype(F32).T.astype(BF16)
        vt_ref[c, hw:hw + ATT_ONES, :] = jnp.ones((ATT_ONES, tb), BF16)
        q = (q_ref[rows, :].astype(F32) * (ATT_DH ** -0.5 * LOG2E)).astype(BF16)
        zero = jnp.zeros_like(q)
        q2_ref[c, 0:tb, :] = jnp.where(lane < ATT_DH, q, zero)
        q2_ref[c, tb:2 * tb, :] = jnp.where(lane >= ATT_DH, q, zero)
        m_ref[c] = jnp.full(m_ref.shape[1:], -jnp.inf, F32)
        acc_ref[c] = jnp.zeros(acc_ref.shape[1:], F32)

    def fill(buf, t, masked):
        s_ref, cmax_ref = buf
        qi, j = items_ref[0, t], items_ref[1, t]
        k = k_ref[pl.ds(pl.multiple_of(j * tb, tb), tb), :]
        st = lax.dot_general(k, q2_ref[qi], (((1,), (1,)), ((), ())), preferred_element_type=F32)
        if masked:
            r = lax.broadcasted_iota(jnp.int32, st.shape, 0)
            c = lax.broadcasted_iota(jnp.int32, st.shape, 1)
            st = jnp.where(r <= jnp.where(c >= tb, c - tb, c), st, -jnp.inf)
        s_ref[...] = st
        cmax_ref[...] = jnp.max(st, axis=0, keepdims=True)

    def update(buf, t, last):
        s_ref, cmax_ref = buf
        qi, j = items_ref[0, t], items_ref[1, t]
        m_old = m_ref[qi]
        m_new = jnp.maximum(m_old, cmax_ref[...])
        p = jnp.exp2(s_ref[...] - m_new).astype(BF16)
        acc = jnp.exp2(m_old - m_new) * acc_ref[qi] + jnp.dot(vt_ref[j], p, preferred_element_type=F32)
        if not last:
            acc_ref[qi] = acc
            m_ref[qi] = m_new
        else:
            o_t = acc[:hw] * (1.0 / acc[hw:hw + 1])
            o = (o_t[:, :tb] - lam_ref[0] * o_t[:, tb:]).T
            o_ref[pl.ds(pl.multiple_of(qi * tb, tb), tb), :] = (
                _rms(o, 1e-5) * (sw_ref[...] * out_scale)).astype(o_ref.dtype)

    def run(first, count, diag, cur, other):
        trips = (count - 1) // 2

        def body(i, carry):
            t = first + 2 * i
            fill(other, t + 1, diag)
            update(cur, t, diag)
            fill(cur, t + 2, diag)
            update(other, t + 1, diag)
            return carry

        lax.fori_loop(0, trips, body, 0)
        if count - 1 > 2 * trips:
            t = first + 2 * trips
            fill(other, t + 1, diag)
            update(cur, t, diag)
            cur, other = other, cur
        return cur, other

    cur, other = (sa_ref, ca_ref), (sb_ref, cb_ref)
    if n_off:
        fill(cur, 0, False)
        cur, other = run(0, n_off, False, cur, other)
        fill(other, n_off, True)
        update(cur, n_off - 1, False)
        cur, other = other, cur
    else:
        fill(cur, 0, True)
    cur, other = run(n_off, n_diag, True, cur, other)
    update(cur, n_off + n_diag - 1, True)


def _diff_attention(proj3, lam, subln_w, lam_init, tb):
    b, s, _ = proj3.shape
    hw = 2 * ATT_DH
    nb = s // tb
    off = [(qi, j) for qi in range(nb) for j in range(qi)]
    items = jnp.asarray(list(zip(*(off + [(qi, qi) for qi in range(nb)]))), jnp.int32)
    kern = functools.partial(_attn_kernel, tb=tb, n_off=len(off), n_diag=nb, out_scale=1.0 - lam_init)
    seq = lambda col: pl.BlockSpec((None, s, hw), lambda bb, h: (bb, 0, col // hw + h))
    return pl.pallas_call(
        kern,
        grid=(b, ATT_HEADS),
        in_specs=[pl.BlockSpec(memory_space=pltpu.SMEM), pl.BlockSpec(memory_space=pltpu.SMEM),
                  seq(COL_Q), seq(COL_K), seq(COL_V), pl.BlockSpec((1, hw), lambda bb, h: (0, 0))],
        out_specs=pl.BlockSpec((None, s, hw), lambda bb, h: (bb, 0, h)),
        out_shape=jax.ShapeDtypeStruct((b, s, ATT_WIDTH), BF16),
        scratch_shapes=[pltpu.VMEM((nb, hw + ATT_ONES, tb), BF16), pltpu.VMEM((nb, 2 * tb, hw), BF16),
                        pltpu.VMEM((nb, 1, 2 * tb), F32), pltpu.VMEM((nb, hw + ATT_ONES, 2 * tb), F32),
                        pltpu.VMEM((tb, 2 * tb), F32), pltpu.VMEM((1, 2 * tb), F32),
                        pltpu.VMEM((tb, 2 * tb), F32), pltpu.VMEM((1, 2 * tb), F32)],
        compiler_params=_params("parallel", "parallel"),
        name="diff_attention",
    )(items, lam.reshape(1).astype(F32), proj3, proj3, proj3, subln_w.reshape(1, hw).astype(F32))


CONV_HALO = 32
CONV_ROWS = 64


def _cconv_kernel(a_ref, g_ref, w_ref, b_ref, lnw_ref, lnb_ref, o_ref, buf_ref, *, ts):
    i = pl.program_id(1)

    @pl.when(i == 0)
    def _():
        buf_ref[0:CONV_HALO, :] = jnp.zeros((CONV_HALO, CONV_CH), F32)

    @pl.when(i > 0)
    def _():
        buf_ref[0:CONV_HALO, :] = buf_ref[ts:ts + CONV_HALO, :]

    buf_ref[CONV_HALO:CONV_HALO + ts, :] = a_ref[...].astype(F32) * _sigmoid(g_ref[...].astype(F32))

    first = CONV_HALO - (CONV_K - 1)
    for c in range(ts // CONV_ROWS):
        base = c * CONV_ROWS
        acc = jnp.zeros((CONV_ROWS, CONV_CH), F32) + b_ref[...]
        for res in range(SUBLANES):
            taps = [k for k in range(CONV_K) if (first + k) % SUBLANES == res]
            rows = CONV_ROWS + (SUBLANES if res else 0)
            part = None
            for k in taps:
                lo = base + first + k - res
                term = w_ref[k:k + 1, :] * buf_ref[lo:lo + rows, :]
                part = term if part is None else part + term
            acc = acc + part[res:res + CONV_ROWS]
        mu = jnp.mean(acc, axis=-1, keepdims=True)
        xc = acc - mu
        var = jnp.mean(xc * xc, axis=-1, keepdims=True)
        y = xc * lax.rsqrt(var + 1e-5) * lnw_ref[...] + lnb_ref[...]
        o_ref[base:base + CONV_ROWS, :] = (y * _sigmoid(y)).astype(o_ref.dtype)


def _conformer_conv(proj3, dw_w, dw_b, ln_w, ln_b, ts):
    b, s, _ = proj3.shape
    c = CONV_CH
    row = lambda a: a.reshape(1, c).astype(F32)
    return pl.pallas_call(
        functools.partial(_cconv_kernel, ts=ts),
        grid=(b, s // ts),
        in_specs=[pl.BlockSpec((None, ts, c), lambda bb, i: (bb, i, COL_U // c)),
                  pl.BlockSpec((None, ts, c), lambda bb, i: (bb, i, COL_U // c + 1)),
                  pl.BlockSpec((CONV_K, c), lambda bb, i: (0, 0)),
                  pl.BlockSpec((1, c), lambda bb, i: (0, 0)),
                  pl.BlockSpec((1, c), lambda bb, i: (0, 0)),
                  pl.BlockSpec((1, c), lambda bb, i: (0, 0))],
        out_specs=pl.BlockSpec((None, ts, c), lambda bb, i: (bb, i, 0)),
        out_shape=jax.ShapeDtypeStruct((b, s, c), BF16),
        scratch_shapes=[pltpu.VMEM((CONV_HALO + ts, c), F32)],
        compiler_params=_params("parallel", "arbitrary"),
        name="conformer_conv",
    )(proj3, proj3, dw_w.astype(F32), row(dw_b), row(ln_w), row(ln_b))


SSD_HALO = 16


def _ssd_kernel(z_ref, xbc_ref, dt_ref, cw_ref, cb_ref, dtb_ref, a_ref, dsk_ref, nw_ref, e_ref,
                o_ref, buf_ref, h_ref, *, chunk):
    L = chunk
    ci = pl.program_id(1)
    gw = SSM_D_INNER // SSM_GROUPS
    hpg = SSM_HEADS // SSM_GROUPS

    @pl.when(ci == 0)
    def _():
        buf_ref[0:SSD_HALO, :] = jnp.zeros((SSD_HALO, SSM_CONV_DIM), BF16)
        h_ref[...] = jnp.zeros(h_ref.shape, F32)

    @pl.when(ci > 0)
    def _():
        buf_ref[0:SSD_HALO, :] = buf_ref[L:L + SSD_HALO, :]

    buf_ref[SSD_HALO:SSD_HALO + L, :] = xbc_ref[...]
    xraw = buf_ref[...]
    r = lax.broadcasted_iota(jnp.int32, (L, SSD_HALO + L), 0)
    c = lax.broadcasted_iota(jnp.int32, (L, SSD_HALO + L), 1)
    xc = jnp.zeros((L, SSM_CONV_DIM), F32) + cb_ref[...]
    for k in range(SSM_CONV_K):
        shift = jnp.where(c == r + (SSD_HALO - (SSM_CONV_K - 1) + k), 1.0, 0.0).astype(BF16)
        xc = xc + cw_ref[k:k + 1, :] * jnp.dot(shift, xraw, preferred_element_type=F32)
    xc = xc * _sigmoid(xc)
    xs = xc[:, :SSM_D_INNER]
    bm = xc[:, SSM_D_INNER:SSM_D_INNER + SSM_GROUPS * SSM_STATE]
    cm = xc[:, SSM_D_INNER + SSM_GROUPS * SSM_STATE:]

    dtr = dt_ref[...] + dtb_ref[...]
    dt = jnp.maximum(dtr, 0.0) + jnp.log(1.0 + jnp.exp(-jnp.abs(dtr)))
    da = dt * a_ref[...]
    row = lax.broadcasted_iota(jnp.int32, (L, L), 0)
    col = lax.broadcasted_iota(jnp.int32, (L, L), 1)
    causal = col <= row
    a_cum = sum(jnp.dot(jnp.where(causal, 1.0, 0.0).astype(BF16), part, preferred_element_type=F32)
                for part in _split3(da))
    a_cum_t = a_cum.T

    expand = e_ref[...]
    per_head = jnp.concatenate([dt, jnp.exp(a_cum), jnp.exp(a_cum[L - 1:L, :] - a_cum)], axis=0)
    full = sum(jnp.dot(part, expand, preferred_element_type=F32) for part in _split3(per_head))
    dt_full, ea_full, dend_full = full[:L], full[L:2 * L], full[2 * L:]
    cdec_full = ea_full[L - 1:L, :]

    xdt = xs * dt_full
    xdt_b = xdt.astype(BF16)
    xw_b = (xdt * dend_full).astype(BF16)

    y_parts = []
    for g in range(SSM_GROUPS):
        cg = cm[:, g * SSM_STATE:(g + 1) * SSM_STATE].astype(BF16)
        bg_t = bm[:, g * SSM_STATE:(g + 1) * SSM_STATE].T.astype(BF16)
        cbm = jnp.dot(cg, bg_t, preferred_element_type=F32)
        h_in = h_ref[:, g * gw:(g + 1) * gw]
        y_off = jnp.dot(cg, h_in.astype(BF16), preferred_element_type=F32)
        states = jnp.dot(bg_t, xw_b[:, g * gw:(g + 1) * gw], preferred_element_type=F32)
        h_ref[:, g * gw:(g + 1) * gw] = h_in * cdec_full[:, g * gw:(g + 1) * gw] + states
        yd = []
        for r in range(hpg):
            hh = g * hpg + r
            seg = a_cum[:, hh:hh + 1] - a_cum_t[hh:hh + 1, :]
            dec = jnp.exp(jnp.where(causal, seg, -jnp.inf))
            yd.append(jnp.dot((cbm * dec).astype(BF16), xdt_b[:, hh * SSM_HEADDIM:(hh + 1) * SSM_HEADDIM],
                              preferred_element_type=F32))
        y_g = (jnp.concatenate(yd, axis=-1) + y_off * ea_full[:, g * gw:(g + 1) * gw]
               + xs[:, g * gw:(g + 1) * gw] * dsk_ref[:, g * gw:(g + 1) * gw])
        zg = z_ref[:, g * gw:(g + 1) * gw].astype(F32)
        y_parts.append(_rms(y_g * (zg * _sigmoid(zg)), 1e-5))
    o_ref[...] = (jnp.concatenate(y_parts, axis=-1) * nw_ref[...]).astype(o_ref.dtype)


def _mamba2_ssd(proj3, dt3, conv_w, conv_b, dt_bias, a_log, d_skip, norm_w):
    b, s, _ = proj3.shape
    L = SSM_CHUNK
    di, cd = SSM_D_INNER, SSM_CONV_DIM
    pad_h = lambda a: jnp.pad(a.astype(F32), (0, LANES - SSM_HEADS)).reshape(1, LANES)
    head_of_channel = jnp.arange(di, dtype=jnp.int32) // SSM_HEADDIM
    expand = (jnp.arange(LANES, dtype=jnp.int32)[:, None] == head_of_channel[None, :]).astype(BF16)
    const = lambda shape: pl.BlockSpec(shape, lambda bb, c: (0, 0))
    return pl.pallas_call(
        functools.partial(_ssd_kernel, chunk=L),
        grid=(b, s // L),
        in_specs=[pl.BlockSpec((None, L, di), lambda bb, c: (bb, c, COL_Z // di)),
                  pl.BlockSpec((None, L, cd), lambda bb, c: (bb, c, COL_XBC // cd)),
                  pl.BlockSpec((None, L, LANES), lambda bb, c: (bb, c, 0)),
                  const((SSM_CONV_K, cd)), const((1, cd)), const((1, LANES)), const((1, LANES)),
                  const((1, di)), const((1, di)), const((LANES, di))],
        out_specs=pl.BlockSpec((None, L, di), lambda bb, c: (bb, c, 0)),
        out_shape=jax.ShapeDtypeStruct((b, s, di), BF16),
        scratch_shapes=[pltpu.VMEM((SSD_HALO + L, cd), BF16), pltpu.VMEM((SSM_STATE, di), F32)],
        compiler_params=_params("parallel", "arbitrary"),
        name="mamba2_ssd",
    )(proj3, proj3, dt3, conv_w.astype(F32), conv_b.reshape(1, cd).astype(F32), pad_h(dt_bias),
      pad_h(-jnp.exp(a_log.astype(F32))), jnp.repeat(d_skip.astype(F32), SSM_HEADDIM).reshape(1, di),
      norm_w.reshape(1, di).astype(F32), expand)


def _merge_kernel(x_ref, ya_ref, yc_ref, ys_ref, ga_ref, gc_ref, gs_ref, wa_ref, wc_ref, ws_ref, wm_ref, o_ref):
    def branch(y_ref, g_ref, w_ref):
        return _sigmoid(g_ref[...].astype(F32)) * jnp.dot(y_ref[...], w_ref[...], preferred_element_type=F32)

    merged = branch(ya_ref, ga_ref, wa_ref) + branch(yc_ref, gc_ref, wc_ref) + branch(ys_ref, gs_ref, ws_ref)
    o_ref[...] = x_ref[...] + jnp.dot(merged.astype(BF16), wm_ref[...], preferred_element_type=F32)


def _merge(x, ya, yc, ys, proj, wa, wc, ws, wm, tm):
    t, d = x.shape
    w = ya.shape[1]
    rows = lambda width, cb: pl.BlockSpec((tm, width), lambda i: (i, cb))
    const = lambda shape: pl.BlockSpec(shape, lambda i: (0, 0))
    return pl.pallas_call(
        _merge_kernel,
        grid=(t // tm,),
        in_specs=[rows(d, 0), rows(w, 0), rows(w, 0), rows(w, 0),
                  rows(d, COL_G // d), rows(d, COL_G // d + 1), rows(d, COL_G // d + 2),
                  const((w, d)), const((w, d)), const((w, d)), const((d, d))],
        out_specs=rows(d, 0),
        out_shape=jax.ShapeDtypeStruct((t, d), F32),
        compiler_params=_params("parallel"),
        name="gated_merge",
    )(x, ya, yc, ys, proj, proj, proj, wa, wc, ws, wm)


def _ffn_kernel(x_ref, g_ref, wg_ref, wu_ref, wd_ref, o_ref):
    x = x_ref[...]
    h = (_rms(x, NORM_EPS) * g_ref[...]).astype(BF16)
    a = jnp.dot(h, wg_ref[...], preferred_element_type=F32)
    u = jnp.dot(h, wu_ref[...], preferred_element_type=F32)
    o_ref[...] = x + jnp.dot((a * _sigmoid(a) * u).astype(BF16), wd_ref[...], preferred_element_type=F32)


def _ffn_dense(x, g, wg, wu, wd, tm):
    t, d = x.shape
    f = wg.shape[1]
    resident = lambda shape: pl.BlockSpec(shape, lambda i: (0, 0), pipeline_mode=pl.Buffered(1))
    return pl.pallas_call(
        _ffn_kernel,
        grid=(t // tm,),
        in_specs=[pl.BlockSpec((tm, d), lambda i: (i, 0)),
                  pl.BlockSpec((1, d), lambda i: (0, 0)),
                  resident((d, f)), resident((d, f)), resident((f, d))],
        out_specs=pl.BlockSpec((tm, d), lambda i: (i, 0)),
        out_shape=jax.ShapeDtypeStruct((t, d), F32),
        compiler_params=_params("parallel"),
        name="ffn_dense",
    )(x, g.reshape(1, d), wg, wu, wd)


def _router_kernel(x_ref, g_ref, wr_ref, br_ref, h_ref, idx_ref, wt_ref, rank_ref, cnt_ref, tri_ref, base_ref):
    i = pl.program_id(0)
    tm = x_ref.shape[0]

    @pl.when(i == 0)
    def _():
        r = lax.broadcasted_iota(jnp.int32, (tm, tm), 0)
        c = lax.broadcasted_iota(jnp.int32, (tm, tm), 1)
        tri_ref[...] = jnp.where(c < r, 1.0, 0.0).astype(BF16)
        base_ref[...] = jnp.zeros(base_ref.shape, F32)

    h = _rms(x_ref[...], NORM_EPS) * g_ref[...]
    h_ref[...] = _pack_halves(h)
    logits = jnp.dot(h, wr_ref[...], precision=HIGHEST, preferred_element_type=F32) + br_ref[...]
    lane = lax.broadcasted_iota(jnp.int32, logits.shape, 1)
    logits = jnp.where(lane < N_EXPERTS, logits, -jnp.inf)
    m1 = jnp.max(logits, axis=-1, keepdims=True)
    i1 = jnp.min(jnp.where(logits == m1, lane, LANES), axis=-1, keepdims=True)
    rest = jnp.where(lane == i1, -jnp.inf, logits)
    m2 = jnp.max(rest, axis=-1, keepdims=True)
    i2 = jnp.min(jnp.where(rest == m2, lane, LANES), axis=-1, keepdims=True)
    e = jnp.exp(m2 - m1)
    w1 = 1.0 / (1.0 + e)
    idx_ref[...] = jnp.where(lane == 0, i1, i2)[:, :TOP_K]
    wt_ref[...] = jnp.where(lane == 0, w1, e * w1)[:, :TOP_K]

    chosen = jnp.logical_or(lane == i1, lane == i2)
    before = jnp.dot(tri_ref[...], jnp.where(chosen, 1.0, 0.0).astype(BF16), preferred_element_type=F32)
    before = before + base_ref[...]
    r1 = jnp.sum(jnp.where(lane == i1, before, 0.0), axis=-1, keepdims=True)
    r2 = jnp.sum(jnp.where(lane == i2, before, 0.0), axis=-1, keepdims=True)
    rank_ref[...] = jnp.where(lane == 0, r1, r2)[:, :TOP_K].astype(jnp.int32)
    base_ref[...] += jnp.sum(jnp.where(chosen, 1.0, 0.0), axis=0, keepdims=True)
    cnt_ref[...] = base_ref[...].astype(jnp.int32)


def _router(x, g, w_router, b_router, tm):
    t, d = x.shape
    wr = jnp.pad(w_router.astype(F32), ((0, 0), (0, LANES - N_EXPERTS)))
    br = jnp.pad(b_router.astype(F32), (0, LANES - N_EXPERTS)).reshape(1, LANES)
    pair = pl.BlockSpec((tm, TOP_K), lambda i: (i, 0))
    return pl.pallas_call(
        _router_kernel,
        grid=(t // tm,),
        in_specs=[pl.BlockSpec((tm, d), lambda i: (i, 0)),
                  pl.BlockSpec((1, d), lambda i: (0, 0)),
                  pl.BlockSpec((d, LANES), lambda i: (0, 0)),
                  pl.BlockSpec((1, LANES), lambda i: (0, 0))],
        out_specs=[pl.BlockSpec((tm, d // 2), lambda i: (i, 0)), pair, pair, pair,
                   pl.BlockSpec((1, LANES), lambda i: (0, 0))],
        out_shape=[jax.ShapeDtypeStruct((t, d // 2), jnp.uint32),
                   jax.ShapeDtypeStruct((t, TOP_K), jnp.int32),
                   jax.ShapeDtypeStruct((t, TOP_K), F32),
                   jax.ShapeDtypeStruct((t, TOP_K), jnp.int32),
                   jax.ShapeDtypeStruct((1, LANES), jnp.int32)],
        scratch_shapes=[pltpu.VMEM((tm, tm), BF16), pltpu.VMEM((1, LANES), F32)],
        compiler_params=_params("arbitrary"),
        name="moe_router",
    )(x, g.reshape(1, d), wr, br)


def _expert_kernel(be_ref, nu_ref, x_ref, wg_ref, wu_ref, wd_ref, o_ref, xb_ref, acc_ref):
    i = pl.program_id(0)
    j = pl.program_id(1)
    last = pl.num_programs(1) - 1
    used = i < nu_ref[0]

    @pl.when(jnp.logical_and(used, j == 0))
    def _():
        xb_ref[...] = _unpack_halves(x_ref[...]).astype(BF16)
        acc_ref[...] = jnp.zeros(acc_ref.shape, F32)

    @pl.when(used)
    def _():
        x = xb_ref[...]
        a = jnp.dot(x, wg_ref[...], preferred_element_type=F32)
        u = jnp.dot(x, wu_ref[...], preferred_element_type=F32)
        acc_ref[...] += jnp.dot((a * _sigmoid(a) * u).astype(BF16), wd_ref[...], preferred_element_type=F32)

    @pl.when(jnp.logical_and(used, j == last))
    def _():
        o_ref[...] = _pack_halves(acc_ref[...])

    @pl.when(jnp.logical_and(jnp.logical_not(used), j == last))
    def _():
        o_ref[...] = jnp.zeros(o_ref.shape, o_ref.dtype)


def _expert_ffn(xs, block_exp, n_used, wg, wu, wd, bm, tf):
    n_rows, half = xs.shape
    d = 2 * half
    f = wg.shape[2]
    nj = f // tf

    def jj(i, j, nu):
        return jnp.where(i < nu[0], j, nj - 1)

    grid_spec = pltpu.PrefetchScalarGridSpec(
        num_scalar_prefetch=2,
        grid=(n_rows // bm, nj),
        in_specs=[pl.BlockSpec((bm, half), lambda i, j, be, nu: (i, 0)),
                  pl.BlockSpec((None, d, tf), lambda i, j, be, nu: (be[i], 0, jj(i, j, nu))),
                  pl.BlockSpec((None, d, tf), lambda i, j, be, nu: (be[i], 0, jj(i, j, nu))),
                  pl.BlockSpec((None, tf, d), lambda i, j, be, nu: (be[i], jj(i, j, nu), 0))],
        out_specs=pl.BlockSpec((bm, half), lambda i, j, be, nu: (i, 0)),
        scratch_shapes=[pltpu.VMEM((bm, d), BF16), pltpu.VMEM((bm, d), F32)],
    )
    return pl.pallas_call(
        _expert_kernel,
        grid_spec=grid_spec,
        out_shape=jax.ShapeDtypeStruct((n_rows, half), jnp.uint32),
        compiler_params=_params("parallel", "arbitrary"),
        name="moe_experts",
    )(block_exp, n_used, xs, wg, wu, wd)


def _route(top_idx, rank, counts, bm):
    t = top_idx.shape[0]
    tk = t * TOP_K
    padded = (counts + bm - 1) // bm * bm
    end_padded = jnp.cumsum(padded)
    start_padded = end_padded - padded
    experts = jnp.arange(N_EXPERTS, dtype=jnp.int32)
    dest = rank + jnp.sum(jnp.where(top_idx[..., None] == experts, start_padded, 0), axis=-1)
    n_blocks = tk // bm + N_EXPERTS
    flat_tok = jnp.repeat(jnp.arange(t, dtype=jnp.int32), TOP_K)
    row_tok = jnp.zeros((n_blocks * bm,), jnp.int32).at[dest.reshape(tk)].set(flat_tok, unique_indices=True)
    block_start = jnp.arange(n_blocks, dtype=jnp.int32) * bm
    block_exp = jnp.minimum(jnp.sum(block_start[:, None] >= end_padded[None, :], axis=1), N_EXPERTS - 1)
    n_used = (end_padded[-1] // bm).reshape(1)
    return row_tok, dest, block_exp.astype(jnp.int32), n_used.astype(jnp.int32)


def _combine_kernel(x_ref, a_ref, b_ref, w_ref, g_ref, o_ref, *, final_norm):
    w = w_ref[...]
    y = x_ref[...] + w[:, 0:1] * _unpack_halves(a_ref[...]) + w[:, 1:2] * _unpack_halves(b_ref[...])
    o_ref[...] = _rms(y, NORM_EPS) * g_ref[...] if final_norm else y


def _combine(x, ab, w, g, tm):
    t, d = x.shape
    spec = pl.BlockSpec((tm, d), lambda i: (i, 0))
    final_norm = g is not None
    gain = g.reshape(1, d) if final_norm else jnp.ones((1, d), F32)
    return pl.pallas_call(
        functools.partial(_combine_kernel, final_norm=final_norm), grid=(t // tm,),
        in_specs=[spec, pl.BlockSpec((tm, d // 2), lambda i: (i, 0)),
                  pl.BlockSpec((tm, d // 2), lambda i: (i + t // tm, 0)),
                  pl.BlockSpec((tm, TOP_K), lambda i: (i, 0)),
                  pl.BlockSpec((1, d), lambda i: (0, 0))],
        out_specs=spec,
        out_shape=jax.ShapeDtypeStruct((t, d), F32), compiler_params=_params("parallel"),
        name="moe_combine",
    )(x, ab, ab, w, gain)


def _moe(x, g, w_router, b_router, wg, wu, wd, final_g, tm, bm, tf):
    h, top_idx, top_w, rank, counts = _router(x, g, w_router, b_router, tm)
    row_tok, dest, block_exp, n_used = _route(top_idx, rank, counts[0, :N_EXPERTS], bm)
    xs = _gather_rows(h, row_tok)
    ys = _expert_ffn(xs, block_exp, n_used, wg, wu, wd, bm, tf)
    picked = _gather_rows(ys, jnp.concatenate([dest[:, 0], dest[:, 1]]))
    return _combine(x, picked, top_w, final_g, tm)


def _final_norm_kernel(x_ref, g_ref, o_ref):
    o_ref[...] = _rms(x_ref[...], NORM_EPS) * g_ref[...]


def _final_norm(x, g, tm):
    t, d = x.shape
    return pl.pallas_call(
        _final_norm_kernel, grid=(t // tm,),
        in_specs=[pl.BlockSpec((tm, d), lambda i: (i, 0)), pl.BlockSpec((1, d), lambda i: (0, 0))],
        out_specs=pl.BlockSpec((tm, d), lambda i: (i, 0)),
        out_shape=jax.ShapeDtypeStruct((t, d), F32), compiler_params=_params("parallel"),
        name="final_norm",
    )(x, g.reshape(1, d))


def _tiles(t, s):
    pick = lambda n, pref: pref if n % pref == 0 else n
    return dict(tm_proj=pick(t, 1024), tn_proj=PROJ_COLS // 4, tb_attn=pick(s, 512), ts_conv=pick(s, 512),
                tm_merge=pick(t, 512), tm_ffn=pick(t, 512), tm_row=pick(t, 1024), bm_moe=pick(t, 512))


def kernel(x, ln_mix_w, w_in, attn_lq1, attn_lk1, attn_lq2, attn_lk2, attn_subln_w, attn_w_o, conv_dw_w, conv_dw_b, conv_ln_w, conv_ln_b, conv_w_pw, ssm_conv_w, ssm_conv_b, ssm_dt_bias, ssm_A_log, ssm_D, ssm_norm_w, ssm_w_out, w_mix_out, ln_ffn_w, ffn_w_gate, ffn_w_up, ffn_w_down, moe_w_router, moe_b_router, moe_w_gate, moe_w_up, moe_w_down, ln_final_w):
    b, s, d = x.shape
    t = b * s
    depth = w_in.shape[0]
    tl = _tiles(t, s)
    xf = x.reshape(t, d)
    for l in range(depth):
        w_main = jnp.concatenate([w_in[l][:, :COL_DT], w_in[l][:, COL_DT + SSM_HEADS:]], axis=1).astype(BF16)
        w_dt = jnp.pad(w_in[l][:, COL_DT:COL_DT + SSM_HEADS], ((0, 0), (0, LANES - SSM_HEADS))).astype(BF16)
        proj = _norm_matmul(xf, ln_mix_w[l], w_main, BF16, tl["tm_proj"], tl["tn_proj"])
        dt = _norm_matmul(xf, ln_mix_w[l], w_dt, F32, tl["tm_proj"], LANES)
        proj3 = proj.reshape(b, s, PROJ_COLS)

        lam_init = 0.8 - 0.6 * math.exp(-0.3 * l)
        lam = (jnp.exp(jnp.sum(attn_lq1[l].astype(F32) * attn_lk1[l].astype(F32)))
               - jnp.exp(jnp.sum(attn_lq2[l].astype(F32) * attn_lk2[l].astype(F32))) + lam_init)
        y_att = _diff_attention(proj3, lam, attn_subln_w[l], lam_init, tl["tb_attn"])
        y_conv = _conformer_conv(proj3, conv_dw_w[l], conv_dw_b[l], conv_ln_w[l], conv_ln_b[l], tl["ts_conv"])
        y_ssm = _mamba2_ssd(proj3, dt.reshape(b, s, LANES), ssm_conv_w[l], ssm_conv_b[l], ssm_dt_bias[l],
                            ssm_A_log[l], ssm_D[l], ssm_norm_w[l])
        xf = _merge(xf, y_att.reshape(t, ATT_WIDTH), y_conv.reshape(t, CONV_CH), y_ssm.reshape(t, SSM_D_INNER),
                    proj, attn_w_o[l].astype(BF16), conv_w_pw[l].astype(BF16), ssm_w_out[l].astype(BF16),
                    w_mix_out[l].astype(BF16), tl["tm_merge"])

        i = l // 2
        if l % 2 == 0:
            xf = _ffn_dense(xf, ln_ffn_w[l], ffn_w_gate[i].astype(BF16), ffn_w_up[i].astype(BF16),
                            ffn_w_down[i].astype(BF16), tl["tm_ffn"])
        else:
            f = moe_w_gate.shape[3]
            tf = f // 2 if (f // 2) % (2 * LANES) == 0 else f
            final_g = ln_final_w if l == depth - 1 else None
            xf = _moe(xf, ln_ffn_w[l], moe_w_router[i], moe_b_router[i], moe_w_gate[i].astype(BF16),
                      moe_w_up[i].astype(BF16), moe_w_down[i].astype(BF16), final_g, tl["tm_row"], tl["bm_moe"], tf)
    if depth % 2 == 1:
        xf = _final_norm(xf, ln_final_w, tl["tm_row"])
    return xf.reshape(b, s, d)
```

```python
import functools
import math

import jax
import jax.numpy as jnp
from jax import lax
from jax.experimental import pallas as pl
from jax.experimental.pallas import tpu as pltpu
from jax.experimental.pallas import tpu_sc as plsc

F32 = jnp.float32
BF16 = jnp.bfloat16
HIGHEST = lax.Precision.HIGHEST

D_MODEL = 1024
ATT_HEADS = 4
ATT_DH = 64
ATT_WIDTH = ATT_HEADS * 2 * ATT_DH
CONV_CH = 512
CONV_K = 31
SSM_D_INNER = 512
SSM_HEADDIM = 64
SSM_HEADS = SSM_D_INNER // SSM_HEADDIM
SSM_GROUPS = 2
SSM_STATE = 128
SSM_CONV_K = 4
SSM_CONV_DIM = SSM_D_INNER + 2 * SSM_GROUPS * SSM_STATE
SSM_CHUNK = 128
N_EXPERTS = 8
TOP_K = 2
NORM_EPS = 1e-6

LANES = 128
SUBLANES = 8
VMEM_LIMIT_BYTES = 48 * 1024 * 1024

COL_Q = 0
COL_K = COL_Q + ATT_WIDTH
COL_V = COL_K + ATT_WIDTH
COL_U = COL_V + ATT_WIDTH
COL_Z = COL_U + 2 * CONV_CH
COL_XBC = COL_Z + SSM_D_INNER
COL_DT = COL_XBC + SSM_CONV_DIM
COL_G = COL_XBC + SSM_CONV_DIM
PROJ_COLS = COL_G + 3 * D_MODEL


def _params(*semantics):
    return pltpu.CompilerParams(dimension_semantics=semantics, vmem_limit_bytes=VMEM_LIMIT_BYTES)


def _sigmoid(x):
    return 1.0 / (1.0 + jnp.exp(-x))


def _rms(x, eps):
    return x * lax.rsqrt(jnp.mean(x * x, axis=-1, keepdims=True) + eps)


def _split3(x):
    hi = x.astype(BF16)
    rest = x - hi.astype(F32)
    mid = rest.astype(BF16)
    return hi, mid, (rest - mid.astype(F32)).astype(BF16)


def _pack_halves(x):
    n = x.shape[1] // 2
    bits = lambda v: lax.bitcast_convert_type(v.astype(BF16).astype(F32), jnp.uint32)
    return (bits(x[:, :n]) >> 16) | bits(x[:, n:])


def _unpack_halves(w):
    lo = lax.bitcast_convert_type(w << 16, F32)
    hi = lax.bitcast_convert_type(w & jnp.uint32(0xFFFF0000), F32)
    return jnp.concatenate([lo, hi], axis=1)


SC_GATHER_ROWS = 64


def _gather_rows(table, idx):
    n_idx = idx.shape[0]
    width = table.shape[1]
    info = plsc.get_sparse_core_info()
    n_workers = info.num_cores * info.num_subcores
    per_worker = n_idx // n_workers
    n_chunks = per_worker // SC_GATHER_ROWS
    assert n_idx % n_workers == 0 and per_worker % SC_GATHER_ROWS == 0 and n_chunks % 2 == 0, (n_idx, n_workers)
    mesh = plsc.VectorSubcoreMesh(core_axis_name="c", subcore_axis_name="s")
    idx_buf = pltpu.VMEM((SC_GATHER_ROWS,), jnp.int32)
    row_buf = pltpu.VMEM((SC_GATHER_ROWS, width), table.dtype)

    @functools.partial(
        pl.kernel, mesh=mesh,
        out_type=jax.ShapeDtypeStruct((n_idx, width), table.dtype),
        scratch_types=[idx_buf, idx_buf, row_buf, row_buf, pltpu.SemaphoreType.DMA, pltpu.SemaphoreType.DMA],
    )
    def gather(table_hbm, idx_hbm, out_hbm, idx0, idx1, rows0, rows1, sem0, sem1):
        worker = lax.axis_index("s") * info.num_cores + lax.axis_index("c")
        slots = ((idx0, rows0, sem0), (idx1, rows1, sem1))

        def offset(c):
            return pl.multiple_of(worker * per_worker + c * SC_GATHER_ROWS, SUBLANES)

        def stream(slot):
            idx_v, rows_v, sem = slot
            return pltpu.make_async_copy(table_hbm.at[idx_v], rows_v, sem)

        def start(c, slot):
            pltpu.sync_copy(idx_hbm.at[pl.ds(offset(c), SC_GATHER_ROWS)], slot[0])
            stream(slot).start()

        start(0, slots[0])

        @pl.loop(0, n_chunks, step=2)
        def _(c0):
            for b in range(2):
                c = c0 + b
                stream(slots[b]).wait()

                @pl.when(c + 1 < n_chunks)
                def _():
                    start(c + 1, slots[1 - b])

                pltpu.sync_copy(slots[b][1], out_hbm.at[pl.ds(offset(c), SC_GATHER_ROWS)])

    return gather(table, idx)


def _norm_matmul_kernel(x_ref, g_ref, w_ref, ws_ref, o_ref, os_ref, h_ref):
    @pl.when(pl.program_id(1) == 0)
    def _():
        h = (_rms(x_ref[...], NORM_EPS) * g_ref[...]).astype(h_ref.dtype)
        h_ref[...] = h
        os_ref[...] = jnp.dot(h, ws_ref[...], preferred_element_type=F32)

    o_ref[...] = jnp.dot(h_ref[...], w_ref[...], preferred_element_type=F32).astype(o_ref.dtype)


def _norm_matmul(x, g, w, w_side, tm, tn):
    t, d = x.shape
    n = w.shape[1]
    ns = w_side.shape[1]
    return pl.pallas_call(
        _norm_matmul_kernel,
        grid=(t // tm, n // tn),
        in_specs=[pl.BlockSpec((tm, d), lambda i, j: (i, 0)),
                  pl.BlockSpec((1, d), lambda i, j: (0, 0)),
                  pl.BlockSpec((d, tn), lambda i, j: (0, j)),
                  pl.BlockSpec((d, ns), lambda i, j: (0, 0))],
        out_specs=[pl.BlockSpec((tm, tn), lambda i, j: (i, j)),
                   pl.BlockSpec((tm, ns), lambda i, j: (i, 0))],
        out_shape=[jax.ShapeDtypeStruct((t, n), BF16), jax.ShapeDtypeStruct((t, ns), F32)],
        scratch_shapes=[pltpu.VMEM((tm, d), BF16)],
        compiler_params=_params("parallel", "arbitrary"),
        name="norm_matmul",
    )(x, g.reshape(1, d), w, w_side)


ATT_ONES = 16
LOG2E = 1.4426950408889634


def _attn_kernel(items_ref, lam_ref, q_ref, k_ref, v_ref, sw_ref, o_ref,
                 vt_ref, q2_ref, m_ref, acc_ref, sa_ref, ca_ref, sb_ref, cb_ref, *, tb, n_off, n_diag, out_scale):
    hw = 2 * ATT_DH
    nb = vt_ref.shape[0]

    lane = lax.broadcasted_iota(jnp.int32, (tb, hw), 1)
    for c in range(nb):
        rows = slice(c * tb, (c + 1) * tb)
        vt_ref[c, 0:hw, :] = v_ref[rows, :].astype(F32).T.astype(BF16)
        vt_ref[c, hw:hw + ATT_ONES, :] = jnp.ones((ATT_ONES, tb), BF16)
        q = (q_ref[rows, :].astype(F32) * (ATT_DH ** -0.5 * LOG2E)).astype(BF16)
        zero = jnp.zeros_like(q)
        q2_ref[c, 0:tb, :] = jnp.where(lane < ATT_DH, q, zero)
        q2_ref[c, tb:2 * tb, :] = jnp.where(lane >= ATT_DH, q, zero)
        m_ref[c] = jnp.full(m_ref.shape[1:], -jnp.inf, F32)
        acc_ref[c] = jnp.zeros(acc_ref.shape[1:], F32)

    def fill(buf, t, masked):
        s_ref, cmax_ref = buf
        qi, j = items_ref[0, t], items_ref[1, t]
        k = k_ref[pl.ds(pl.multiple_of(j * tb, tb), tb), :]
        st = lax.dot_general(k, q2_ref[qi], (((1,), (1,)), ((), ())), preferred_element_type=F32)
        if masked:
            r = lax.broadcasted_iota(jnp.int32, st.shape, 0)
            c = lax.broadcasted_iota(jnp.int32, st.shape, 1)
            st = jnp.where(r <= jnp.where(c >= tb, c - tb, c), st, -jnp.inf)
        s_ref[...] = st
        cmax_ref[...] = jnp.max(st, axis=0, keepdims=True)

    def update(buf, t, last):
        s_ref, cmax_ref = buf
        qi, j = items_ref[0, t], items_ref[1, t]
        m_old = m_ref[qi]
        m_new = jnp.maximum(m_old, cmax_ref[...])
        p = jnp.exp2(s_ref[...] - m_new).astype(BF16)
        acc = jnp.exp2(m_old - m_new) * acc_ref[qi] + jnp.dot(vt_ref[j], p, preferred_element_type=F32)
        if not last:
            acc_ref[qi] = acc
            m_ref[qi] = m_new
        else:
            o_t = acc[:hw] * (1.0 / acc[hw:hw + 1])
            o = (o_t[:, :tb] - lam_ref[0] * o_t[:, tb:]).T
            o_ref[pl.ds(pl.multiple_of(qi * tb, tb), tb), :] = (
                _rms(o, 1e-5) * (sw_ref[...] * out_scale)).astype(o_ref.dtype)

    def run(first, count, diag, cur, other):
        trips = (count - 1) // 2

        def body(i, carry):
            t = first + 2 * i
            fill(other, t + 1, diag)
            update(cur, t, diag)
            fill(cur, t + 2, diag)
            update(other, t + 1, diag)
            return carry

        lax.fori_loop(0, trips, body, 0)
        if count - 1 > 2 * trips:
            t = first + 2 * trips
            fill(other, t + 1, diag)
            update(cur, t, diag)
            cur, other = other, cur
        return cur, other

    cur, other = (sa_ref, ca_ref), (sb_ref, cb_ref)
    if n_off:
        fill(cur, 0, False)
        cur, other = run(0, n_off, False, cur, other)
        fill(other, n_off, True)
        update(cur, n_off - 1, False)
        cur, other = other, cur
    else:
        fill(cur, 0, True)
    cur, other = run(n_off, n_diag, True, cur, other)
    update(cur, n_off + n_diag - 1, True)


def _diff_attention(proj3, lam, subln_w, lam_init, tb):
    b, s, _ = proj3.shape
    hw = 2 * ATT_DH
    nb = s // tb
    off = [(qi, j) for qi in range(nb) for j in range(qi)]
    items = jnp.asarray(list(zip(*(off + [(qi, qi) for qi in range(nb)]))), jnp.int32)
    kern = functools.partial(_attn_kernel, tb=tb, n_off=len(off), n_diag=nb, out_scale=1.0 - lam_init)
    seq = lambda col: pl.BlockSpec((None, s, hw), lambda bb, h: (bb, 0, col // hw + h))
    return pl.pallas_call(
        kern,
        grid=(b, ATT_HEADS),
        in_specs=[pl.BlockSpec(memory_space=pltpu.SMEM), pl.BlockSpec(memory_space=pltpu.SMEM),
                  seq(COL_Q), seq(COL_K), seq(COL_V), pl.BlockSpec((1, hw), lambda bb, h: (0, 0))],
        out_specs=pl.BlockSpec((None, s, hw), lambda bb, h: (bb, 0, h)),
        out_shape=jax.ShapeDtypeStruct((b, s, ATT_WIDTH), BF16),
        scratch_shapes=[pltpu.VMEM((nb, hw + ATT_ONES, tb), BF16), pltpu.VMEM((nb, 2 * tb, hw), BF16),
                        pltpu.VMEM((nb, 1, 2 * tb), F32), pltpu.VMEM((nb, hw + ATT_ONES, 2 * tb), F32),
                        pltpu.VMEM((tb, 2 * tb), F32), pltpu.VMEM((1, 2 * tb), F32),
                        pltpu.VMEM((tb, 2 * tb), F32), pltpu.VMEM((1, 2 * tb), F32)],
        compiler_params=_params("parallel", "parallel"),
        name="diff_attention",
    )(items, lam.reshape(1).astype(F32), proj3, proj3, proj3, subln_w.reshape(1, hw).astype(F32))


CONV_HALO = 32
CONV_ROWS = 64


def _cconv_kernel(a_ref, g_ref, w_ref, b_ref, lnw_ref, lnb_ref, o_ref, buf_ref, *, ts):
    i = pl.program_id(1)

    @pl.when(i == 0)
    def _():
        buf_ref[0:CONV_HALO, :] = jnp.zeros((CONV_HALO, CONV_CH), F32)

    @pl.when(i > 0)
    def _():
        buf_ref[0:CONV_HALO, :] = buf_ref[ts:ts + CONV_HALO, :]

    buf_ref[CONV_HALO:CONV_HALO + ts, :] = a_ref[...].astype(F32) * _sigmoid(g_ref[...].astype(F32))

    first = CONV_HALO - (CONV_K - 1)
    for c in range(ts // CONV_ROWS):
        base = c * CONV_ROWS
        acc = jnp.zeros((CONV_ROWS, CONV_CH), F32) + b_ref[...]
        for res in range(SUBLANES):
            taps = [k for k in range(CONV_K) if (first + k) % SUBLANES == res]
            rows = CONV_ROWS + (SUBLANES if res else 0)
            part = None
            for k in taps:
                lo = base + first + k - res
                term = w_ref[k:k + 1, :] * buf_ref[lo:lo + rows, :]
                part = term if part is None else part + term
            acc = acc + part[res:res + CONV_ROWS]
        mu = jnp.mean(acc, axis=-1, keepdims=True)
        xc = acc - mu
        var = jnp.mean(xc * xc, axis=-1, keepdims=True)
        y = xc * lax.rsqrt(var + 1e-5) * lnw_ref[...] + lnb_ref[...]
        o_ref[base:base + CONV_ROWS, :] = (y * _sigmoid(y)).astype(o_ref.dtype)


def _conformer_conv(proj3, dw_w, dw_b, ln_w, ln_b, ts):
    b, s, _ = proj3.shape
    c = CONV_CH
    row = lambda a: a.reshape(1, c).astype(F32)
    return pl.pallas_call(
        functools.partial(_cconv_kernel, ts=ts),
        grid=(b, s // ts),
        in_specs=[pl.BlockSpec((None, ts, c), lambda bb, i: (bb, i, COL_U // c)),
                  pl.BlockSpec((None, ts, c), lambda bb, i: (bb, i, COL_U // c + 1)),
                  pl.BlockSpec((CONV_K, c), lambda bb, i: (0, 0)),
                  pl.BlockSpec((1, c), lambda bb, i: (0, 0)),
                  pl.BlockSpec((1, c), lambda bb, i: (0, 0)),
                  pl.BlockSpec((1, c), lambda bb, i: (0, 0))],
        out_specs=pl.BlockSpec((None, ts, c), lambda bb, i: (bb, i, 0)),
        out_shape=jax.ShapeDtypeStruct((b, s, c), BF16),
        scratch_shapes=[pltpu.VMEM((CONV_HALO + ts, c), F32)],
        compiler_params=_params("parallel", "arbitrary"),
        name="conformer_conv",
    )(proj3, proj3, dw_w.astype(F32), row(dw_b), row(ln_w), row(ln_b))


SSD_HALO = 16


def _ssd_kernel(z_ref, xbc_ref, dt_ref, cw_ref, cb_ref, dtb_ref, a_ref, dsk_ref, nw_ref, e_ref,
                o_ref, buf_ref, h_ref, *, chunk):
    L = chunk
    ci = pl.program_id(1)
    gw = SSM_D_INNER // SSM_GROUPS
    hpg = SSM_HEADS // SSM_GROUPS

    @pl.when(ci == 0)
    def _():
        buf_ref[0:SSD_HALO, :] = jnp.zeros((SSD_HALO, SSM_CONV_DIM), BF16)
        h_ref[...] = jnp.zeros(h_ref.shape, F32)

    @pl.when(ci > 0)
    def _():
        buf_ref[0:SSD_HALO, :] = buf_ref[L:L + SSD_HALO, :]

    buf_ref[SSD_HALO:SSD_HALO + L, :] = xbc_ref[...]
    xraw = buf_ref[...]
    r = lax.broadcasted_iota(jnp.int32, (L, SSD_HALO + L), 0)
    c = lax.broadcasted_iota(jnp.int32, (L, SSD_HALO + L), 1)
    xc = jnp.zeros((L, SSM_CONV_DIM), F32) + cb_ref[...]
    for k in range(SSM_CONV_K):
        shift = jnp.where(c == r + (SSD_HALO - (SSM_CONV_K - 1) + k), 1.0, 0.0).astype(BF16)
        xc = xc + cw_ref[k:k + 1, :] * jnp.dot(shift, xraw, preferred_element_type=F32)
    xc = xc * _sigmoid(xc)
    xs = xc[:, :SSM_D_INNER]
    bm = xc[:, SSM_D_INNER:SSM_D_INNER + SSM_GROUPS * SSM_STATE]
    cm = xc[:, SSM_D_INNER + SSM_GROUPS * SSM_STATE:]

    dtr = dt_ref[...] + dtb_ref[...]
    dt = jnp.maximum(dtr, 0.0) + jnp.log(1.0 + jnp.exp(-jnp.abs(dtr)))
    da = dt * a_ref[...]
    row = lax.broadcasted_iota(jnp.int32, (L, L), 0)
    col = lax.broadcasted_iota(jnp.int32, (L, L), 1)
    causal = col <= row
    a_cum = sum(jnp.dot(jnp.where(causal, 1.0, 0.0).astype(BF16), part, preferred_element_type=F32)
                for part in _split3(da))
    a_cum_t = a_cum.T

    expand = e_ref[...]
    per_head = jnp.concatenate([dt, jnp.exp(a_cum), jnp.exp(a_cum[L - 1:L, :] - a_cum)], axis=0)
    full = sum(jnp.dot(part, expand, preferred_element_type=F32) for part in _split3(per_head))
    dt_full, ea_full, dend_full = full[:L], full[L:2 * L], full[2 * L:]
    cdec_full = ea_full[L - 1:L, :]

    xdt = xs * dt_full
    xdt_b = xdt.astype(BF16)
    xw_b = (xdt * dend_full).astype(BF16)

    y_parts = []
    for g in range(SSM_GROUPS):
        cg = cm[:, g * SSM_STATE:(g + 1) * SSM_STATE].astype(BF16)
        bg_t = bm[:, g * SSM_STATE:(g + 1) * SSM_STATE].T.astype(BF16)
        cbm = jnp.dot(cg, bg_t, preferred_element_type=F32)
        h_in = h_ref[:, g * gw:(g + 1) * gw]
        y_off = jnp.dot(cg, h_in.astype(BF16), preferred_element_type=F32)
        states = jnp.dot(bg_t, xw_b[:, g * gw:(g + 1) * gw], preferred_element_type=F32)
        h_ref[:, g * gw:(g + 1) * gw] = h_in * cdec_full[:, g * gw:(g + 1) * gw] + states
        yd = []
        for r in range(hpg):
            hh = g * hpg + r
            seg = a_cum[:, hh:hh + 1] - a_cum_t[hh:hh + 1, :]
            dec = jnp.exp(jnp.where(causal, seg, -jnp.inf))
            yd.append(jnp.dot((cbm * dec).astype(BF16), xdt_b[:, hh * SSM_HEADDIM:(hh + 1) * SSM_HEADDIM],
                              preferred_element_type=F32))
        y_g = (jnp.concatenate(yd, axis=-1) + y_off * ea_full[:, g * gw:(g + 1) * gw]
               + xs[:, g * gw:(g + 1) * gw] * dsk_ref[:, g * gw:(g + 1) * gw])
        zg = z_ref[:, g * gw:(g + 1) * gw].astype(F32)
        y_parts.append(_rms(y_g * (zg * _sigmoid(zg)), 1e-5))
    o_ref[...] = (jnp.concatenate(y_parts, axis=-1) * nw_ref[...]).astype(o_ref.dtype)


def _mamba2_ssd(proj3, dt3, conv_w, conv_b, dt_bias, a_log, d_skip, norm_w):
    b, s, _ = proj3.shape
    L = SSM_CHUNK
    di, cd = SSM_D_INNER, SSM_CONV_DIM
    pad_h = lambda a: jnp.pad(a.astype(F32), (0, LANES - SSM_HEADS)).reshape(1, LANES)
    head_of_channel = jnp.arange(di, dtype=jnp.int32) // SSM_HEADDIM
    expand = (jnp.arange(LANES, dtype=jnp.int32)[:, None] == head_of_channel[None, :]).astype(BF16)
    const = lambda shape: pl.BlockSpec(shape, lambda bb, c: (0, 0))
    return pl.pallas_call(
        functools.partial(_ssd_kernel, chunk=L),
        grid=(b, s // L),
        in_specs=[pl.BlockSpec((None, L, di), lambda bb, c: (bb, c, COL_Z // di)),
                  pl.BlockSpec((None, L, cd), lambda bb, c: (bb, c, COL_XBC // cd)),
                  pl.BlockSpec((None, L, LANES), lambda bb, c: (bb, c, 0)),
                  const((SSM_CONV_K, cd)), const((1, cd)), const((1, LANES)), const((1, LANES)),
                  const((1, di)), const((1, di)), const((LANES, di))],
        out_specs=pl.BlockSpec((None, L, di), lambda bb, c: (bb, c, 0)),
        out_shape=jax.ShapeDtypeStruct((b, s, di), BF16),
        scratch_shapes=[pltpu.VMEM((SSD_HALO + L, cd), BF16), pltpu.VMEM((SSM_STATE, di), F32)],
        compiler_params=_params("parallel", "arbitrary"),
        name="mamba2_ssd",
    )(proj3, proj3, dt3, conv_w.astype(F32), conv_b.reshape(1, cd).astype(F32), pad_h(dt_bias),
      pad_h(-jnp.exp(a_log.astype(F32))), jnp.repeat(d_skip.astype(F32), SSM_HEADDIM).reshape(1, di),
      norm_w.reshape(1, di).astype(F32), expand)


def _merge_kernel(x_ref, ya_ref, yc_ref, ys_ref, ga_ref, gc_ref, gs_ref, wa_ref, wc_ref, ws_ref, wm_ref, o_ref):
    def branch(y_ref, g_ref, w_ref):
        return _sigmoid(g_ref[...].astype(F32)) * jnp.dot(y_ref[...], w_ref[...], preferred_element_type=F32)

    merged = branch(ya_ref, ga_ref, wa_ref) + branch(yc_ref, gc_ref, wc_ref) + branch(ys_ref, gs_ref, ws_ref)
    o_ref[...] = x_ref[...] + jnp.dot(merged.astype(BF16), wm_ref[...], preferred_element_type=F32)


def _merge(x, ya, yc, ys, proj, wa, wc, ws, wm, tm):
    t, d = x.shape
    w = ya.shape[1]
    rows = lambda width, cb: pl.BlockSpec((tm, width), lambda i: (i, cb))
    const = lambda shape: pl.BlockSpec(shape, lambda i: (0, 0))
    return pl.pallas_call(
        _merge_kernel,
        grid=(t // tm,),
        in_specs=[rows(d, 0), rows(w, 0), rows(w, 0), rows(w, 0),
                  rows(d, COL_G // d), rows(d, COL_G // d + 1), rows(d, COL_G // d + 2),
                  const((w, d)), const((w, d)), const((w, d)), const((d, d))],
        out_specs=rows(d, 0),
        out_shape=jax.ShapeDtypeStruct((t, d), F32),
        compiler_params=_params("parallel"),
        name="gated_merge",
    )(x, ya, yc, ys, proj, proj, proj, wa, wc, ws, wm)


def _ffn_kernel(x_ref, g_ref, wg_ref, wu_ref, wd_ref, o_ref):
    x = x_ref[...]
    h = (_rms(x, NORM_EPS) * g_ref[...]).astype(BF16)
    a = jnp.dot(h, wg_ref[...], preferred_element_type=F32)
    u = jnp.dot(h, wu_ref[...], preferred_element_type=F32)
    o_ref[...] = x + jnp.dot((a * _sigmoid(a) * u).astype(BF16), wd_ref[...], preferred_element_type=F32)


def _ffn_dense(x, g, wg, wu, wd, tm):
    t, d = x.shape
    f = wg.shape[1]
    resident = lambda shape: pl.BlockSpec(shape, lambda i: (0, 0), pipeline_mode=pl.Buffered(1))
    return pl.pallas_call(
        _ffn_kernel,
        grid=(t // tm,),
        in_specs=[pl.BlockSpec((tm, d), lambda i: (i, 0)),
                  pl.BlockSpec((1, d), lambda i: (0, 0)),
                  resident((d, f)), resident((d, f)), resident((f, d))],
        out_specs=pl.BlockSpec((tm, d), lambda i: (i, 0)),
        out_shape=jax.ShapeDtypeStruct((t, d), F32),
        compiler_params=_params("parallel"),
        name="ffn_dense",
    )(x, g.reshape(1, d), wg, wu, wd)


def _router_kernel(x_ref, g_ref, wr_ref, br_ref, h_ref, idx_ref, wt_ref, rank_ref, cnt_ref, tri_ref, base_ref):
    i = pl.program_id(0)
    tm = x_ref.shape[0]

    @pl.when(i == 0)
    def _():
        r = lax.broadcasted_iota(jnp.int32, (tm, tm), 0)
        c = lax.broadcasted_iota(jnp.int32, (tm, tm), 1)
        tri_ref[...] = jnp.where(c < r, 1.0, 0.0).astype(BF16)
        base_ref[...] = jnp.zeros(base_ref.shape, F32)

    h = _rms(x_ref[...], NORM_EPS) * g_ref[...]
    h_ref[...] = _pack_halves(h)
    logits = jnp.dot(h, wr_ref[...], precision=HIGHEST, preferred_element_type=F32) + br_ref[...]
    lane = lax.broadcasted_iota(jnp.int32, logits.shape, 1)
    logits = jnp.where(lane < N_EXPERTS, logits, -jnp.inf)
    m1 = jnp.max(logits, axis=-1, keepdims=True)
    i1 = jnp.min(jnp.where(logits == m1, lane, LANES), axis=-1, keepdims=True)
    rest = jnp.where(lane == i1, -jnp.inf, logits)
    m2 = jnp.max(rest, axis=-1, keepdims=True)
    i2 = jnp.min(jnp.where(rest == m2, lane, LANES), axis=-1, keepdims=True)
    e = jnp.exp(m2 - m1)
    w1 = 1.0 / (1.0 + e)
    idx_ref[...] = jnp.where(lane == 0, i1, i2)[:, :TOP_K]
    wt_ref[...] = jnp.where(lane == 0, w1, e * w1)[:, :TOP_K]

    chosen = jnp.logical_or(lane == i1, lane == i2)
    before = jnp.dot(tri_ref[...], jnp.where(chosen, 1.0, 0.0).astype(BF16), preferred_element_type=F32)
    before = before + base_ref[...]
    r1 = jnp.sum(jnp.where(lane == i1, before, 0.0), axis=-1, keepdims=True)
    r2 = jnp.sum(jnp.where(lane == i2, before, 0.0), axis=-1, keepdims=True)
    rank_ref[...] = jnp.where(lane == 0, r1, r2)[:, :TOP_K].astype(jnp.int32)
    base_ref[...] += jnp.sum(jnp.where(chosen, 1.0, 0.0), axis=0, keepdims=True)
    cnt_ref[...] = base_ref[...].astype(jnp.int32)


def _router(x, g, w_router, b_router, tm):
    t, d = x.shape
    wr = jnp.pad(w_router.astype(F32), ((0, 0), (0, LANES - N_EXPERTS)))
    br = jnp.pad(b_router.astype(F32), (0, LANES - N_EXPERTS)).reshape(1, LANES)
    pair = pl.BlockSpec((tm, TOP_K), lambda i: (i, 0))
    return pl.pallas_call(
        _router_kernel,
        grid=(t // tm,),
        in_specs=[pl.BlockSpec((tm, d), lambda i: (i, 0)),
                  pl.BlockSpec((1, d), lambda i: (0, 0)),
                  pl.BlockSpec((d, LANES), lambda i: (0, 0)),
                  pl.BlockSpec((1, LANES), lambda i: (0, 0))],
        out_specs=[pl.BlockSpec((tm, d // 2), lambda i: (i, 0)), pair, pair, pair,
                   pl.BlockSpec((1, LANES), lambda i: (0, 0))],
        out_shape=[jax.ShapeDtypeStruct((t, d // 2), jnp.uint32),
                   jax.ShapeDtypeStruct((t, TOP_K), jnp.int32),
                   jax.ShapeDtypeStruct((t, TOP_K), F32),
                   jax.ShapeDtypeStruct((t, TOP_K), jnp.int32),
                   jax.ShapeDtypeStruct((1, LANES), jnp.int32)],
        scratch_shapes=[pltpu.VMEM((tm, tm), BF16), pltpu.VMEM((1, LANES), F32)],
        compiler_params=_params("arbitrary"),
        name="moe_router",
    )(x, g.reshape(1, d), wr, br)


def _expert_kernel(be_ref, nu_ref, x_ref, wg_ref, wu_ref, wd_ref, o_ref, xb_ref, acc_ref):
    i = pl.program_id(0)
    j = pl.program_id(1)
    last = pl.num_programs(1) - 1
    used = i < nu_ref[0]

    @pl.when(jnp.logical_and(used, j == 0))
    def _():
        xb_ref[...] = _unpack_halves(x_ref[...]).astype(BF16)
        acc_ref[...] = jnp.zeros(acc_ref.shape, F32)

    @pl.when(used)
    def _():
        x = xb_ref[...]
        a = jnp.dot(x, wg_ref[...], preferred_element_type=F32)
        u = jnp.dot(x, wu_ref[...], preferred_element_type=F32)
        acc_ref[...] += jnp.dot((a * _sigmoid(a) * u).astype(BF16), wd_ref[...], preferred_element_type=F32)

    @pl.when(jnp.logical_and(used, j == last))
    def _():
        o_ref[...] = _pack_halves(acc_ref[...])

    @pl.when(jnp.logical_and(jnp.logical_not(used), j == last))
    def _():
        o_ref[...] = jnp.zeros(o_ref.shape, o_ref.dtype)


def _expert_ffn(xs, block_exp, n_used, wg, wu, wd, bm, tf):
    n_rows, half = xs.shape
    d = 2 * half
    f = wg.shape[2]
    nj = f // tf

    def jj(i, j, nu):
        return jnp.where(i < nu[0], j, nj - 1)

    grid_spec = pltpu.PrefetchScalarGridSpec(
        num_scalar_prefetch=2,
        grid=(n_rows // bm, nj),
        in_specs=[pl.BlockSpec((bm, half), lambda i, j, be, nu: (i, 0)),
                  pl.BlockSpec((None, d, tf), lambda i, j, be, nu: (be[i], 0, jj(i, j, nu))),
                  pl.BlockSpec((None, d, tf), lambda i, j, be, nu: (be[i], 0, jj(i, j, nu))),
                  pl.BlockSpec((None, tf, d), lambda i, j, be, nu: (be[i], jj(i, j, nu), 0))],
        out_specs=pl.BlockSpec((bm, half), lambda i, j, be, nu: (i, 0)),
        scratch_shapes=[pltpu.VMEM((bm, d), BF16), pltpu.VMEM((bm, d), F32)],
    )
    return pl.pallas_call(
        _expert_kernel,
        grid_spec=grid_spec,
        out_shape=jax.ShapeDtypeStruct((n_rows, half), jnp.uint32),
        compiler_params=_params("parallel", "arbitrary"),
        name="moe_experts",
    )(block_exp, n_used, xs, wg, wu, wd)


def _route(top_idx, rank, counts, bm):
    t = top_idx.shape[0]
    tk = t * TOP_K
    padded = (counts + bm - 1) // bm * bm
    end_padded = jnp.cumsum(padded)
    start_padded = end_padded - padded
    experts = jnp.arange(N_EXPERTS, dtype=jnp.int32)
    dest = rank + jnp.sum(jnp.where(top_idx[..., None] == experts, start_padded, 0), axis=-1)
    n_blocks = tk // bm + N_EXPERTS
    flat_tok = jnp.repeat(jnp.arange(t, dtype=jnp.int32), TOP_K)
    row_tok = jnp.zeros((n_blocks * bm,), jnp.int32).at[dest.reshape(tk)].set(flat_tok, unique_indices=True)
    block_start = jnp.arange(n_blocks, dtype=jnp.int32) * bm
    block_exp = jnp.minimum(jnp.sum(block_start[:, None] >= end_padded[None, :], axis=1), N_EXPERTS - 1)
    n_used = (end_padded[-1] // bm).reshape(1)
    return row_tok, dest, block_exp.astype(jnp.int32), n_used.astype(jnp.int32)


def _combine_kernel(x_ref, a_ref, b_ref, w_ref, g_ref, o_ref, *, final_norm):
    w = w_ref[...]
    y = x_ref[...] + w[:, 0:1] * _unpack_halves(a_ref[...]) + w[:, 1:2] * _unpack_halves(b_ref[...])
    o_ref[...] = _rms(y, NORM_EPS) * g_ref[...] if final_norm else y


def _combine(x, ab, w, g, tm):
    t, d = x.shape
    spec = pl.BlockSpec((tm, d), lambda i: (i, 0))
    final_norm = g is not None
    gain = g.reshape(1, d) if final_norm else jnp.ones((1, d), F32)
    return pl.pallas_call(
        functools.partial(_combine_kernel, final_norm=final_norm), grid=(t // tm,),
        in_specs=[spec, pl.BlockSpec((tm, d // 2), lambda i: (i, 0)),
                  pl.BlockSpec((tm, d // 2), lambda i: (i + t // tm, 0)),
                  pl.BlockSpec((tm, TOP_K), lambda i: (i, 0)),
                  pl.BlockSpec((1, d), lambda i: (0, 0))],
        out_specs=spec,
        out_shape=jax.ShapeDtypeStruct((t, d), F32), compiler_params=_params("parallel"),
        name="moe_combine",
    )(x, ab, ab, w, gain)


def _moe(x, g, w_router, b_router, wg, wu, wd, final_g, tm, bm, tf):
    h, top_idx, top_w, rank, counts = _router(x, g, w_router, b_router, tm)
    row_tok, dest, block_exp, n_used = _route(top_idx, rank, counts[0, :N_EXPERTS], bm)
    xs = _gather_rows(h, row_tok)
    ys = _expert_ffn(xs, block_exp, n_used, wg, wu, wd, bm, tf)
    picked = _gather_rows(ys, jnp.concatenate([dest[:, 0], dest[:, 1]]))
    return _combine(x, picked, top_w, final_g, tm)


def _final_norm_kernel(x_ref, g_ref, o_ref):
    o_ref[...] = _rms(x_ref[...], NORM_EPS) * g_ref[...]


def _final_norm(x, g, tm):
    t, d = x.shape
    return pl.pallas_call(
        _final_norm_kernel, grid=(t // tm,),
        in_specs=[pl.BlockSpec((tm, d), lambda i: (i, 0)), pl.BlockSpec((1, d), lambda i: (0, 0))],
        out_specs=pl.BlockSpec((tm, d), lambda i: (i, 0)),
        out_shape=jax.ShapeDtypeStruct((t, d), F32), compiler_params=_params("parallel"),
        name="final_norm",
    )(x, g.reshape(1, d))


def _tiles(t, s):
    pick = lambda n, pref: pref if n % pref == 0 else n
    return dict(tm_proj=pick(t, 1024), tn_proj=PROJ_COLS // 4, tb_attn=pick(s, 512), ts_conv=pick(s, 512),
                tm_merge=pick(t, 512), tm_ffn=pick(t, 512), tm_row=pick(t, 1024), bm_moe=pick(t, 512))


def kernel(x, ln_mix_w, w_in, attn_lq1, attn_lk1, attn_lq2, attn_lk2, attn_subln_w, attn_w_o, conv_dw_w, conv_dw_b, conv_ln_w, conv_ln_b, conv_w_pw, ssm_conv_w, ssm_conv_b, ssm_dt_bias, ssm_A_log, ssm_D, ssm_norm_w, ssm_w_out, w_mix_out, ln_ffn_w, ffn_w_gate, ffn_w_up, ffn_w_down, moe_w_router, moe_b_router, moe_w_gate, moe_w_up, moe_w_down, ln_final_w):
    b, s, d = x.shape
    t = b * s
    depth = w_in.shape[0]
    tl = _tiles(t, s)
    xf = x.reshape(t, d)
    for l in range(depth):
        w_main = jnp.concatenate([w_in[l][:, :COL_DT], w_in[l][:, COL_DT + SSM_HEADS:]], axis=1).astype(BF16)
        w_dt = jnp.pad(w_in[l][:, COL_DT:COL_DT + SSM_HEADS], ((0, 0), (0, LANES - SSM_HEADS))).astype(BF16)
        proj, dt = _norm_matmul(xf, ln_mix_w[l], w_main, w_dt, tl["tm_proj"], tl["tn_proj"])
        proj3 = proj.reshape(b, s, PROJ_COLS)

        lam_init = 0.8 - 0.6 * math.exp(-0.3 * l)
        lam = (jnp.exp(jnp.sum(attn_lq1[l].astype(F32) * attn_lk1[l].astype(F32)))
               - jnp.exp(jnp.sum(attn_lq2[l].astype(F32) * attn_lk2[l].astype(F32))) + lam_init)
        y_att = _diff_attention(proj3, lam, attn_subln_w[l], lam_init, tl["tb_attn"])
        y_conv = _conformer_conv(proj3, conv_dw_w[l], conv_dw_b[l], conv_ln_w[l], conv_ln_b[l], tl["ts_conv"])
        y_ssm = _mamba2_ssd(proj3, dt.reshape(b, s, LANES), ssm_conv_w[l], ssm_conv_b[l], ssm_dt_bias[l],
                            ssm_A_log[l], ssm_D[l], ssm_norm_w[l])
        xf = _merge(xf, y_att.reshape(t, ATT_WIDTH), y_conv.reshape(t, CONV_CH), y_ssm.reshape(t, SSM_D_INNER),
                    proj, attn_w_o[l].astype(BF16), conv_w_pw[l].astype(BF16), ssm_w_out[l].astype(BF16),
                    w_mix_out[l].astype(BF16), tl["tm_merge"])

        i = l // 2
        if l % 2 == 0:
            xf = _ffn_dense(xf, ln_ffn_w[l], ffn_w_gate[i].astype(BF16), ffn_w_up[i].astype(BF16),
                            ffn_w_down[i].astype(BF16), tl["tm_ffn"])
        else:
            f = moe_w_gate.shape[3]
            tf = f // 2 if (f // 2) % (2 * LANES) == 0 else f
            final_g = ln_final_w if l == depth - 1 else None
            xf = _moe(xf, ln_ffn_w[l], moe_w_router[i], moe_b_router[i], moe_w_gate[i].astype(BF16),
                      moe_w_up[i].astype(BF16), moe_w_down[i].astype(BF16), final_g, tl["tm_row"], tl["bm_moe"], tf)
    if depth % 2 == 1:
        xf = _final_norm(xf, ln_final_w, tl["tm_row"])
    return xf.reshape(b, s, d)
```

```python
import functools
import math

import jax
import jax.numpy as jnp
from jax import lax
from jax.experimental import pallas as pl
from jax.experimental.pallas import tpu as pltpu
from jax.experimental.pallas import tpu_sc as plsc

F32 = jnp.float32
BF16 = jnp.bfloat16
HIGHEST = lax.Precision.HIGHEST

D_MODEL = 1024
ATT_HEADS = 4
ATT_DH = 64
ATT_WIDTH = ATT_HEADS * 2 * ATT_DH
CONV_CH = 512
CONV_K = 31
SSM_D_INNER = 512
SSM_HEADDIM = 64
SSM_HEADS = SSM_D_INNER // SSM_HEADDIM
SSM_GROUPS = 2
SSM_STATE = 128
SSM_CONV_K = 4
SSM_CONV_DIM = SSM_D_INNER + 2 * SSM_GROUPS * SSM_STATE
SSM_CHUNK = 128
N_EXPERTS = 8
TOP_K = 2
NORM_EPS = 1e-6

LANES = 128
SUBLANES = 8
VMEM_LIMIT_BYTES = 48 * 1024 * 1024

COL_Q = 0
COL_K = COL_Q + ATT_WIDTH
COL_V = COL_K + ATT_WIDTH
COL_U = COL_V + ATT_WIDTH
COL_Z = COL_U + 2 * CONV_CH
COL_XBC = COL_Z + SSM_D_INNER
COL_DT = COL_XBC + SSM_CONV_DIM
COL_G = COL_XBC + SSM_CONV_DIM
PROJ_COLS = COL_G + 3 * D_MODEL


def _params(*semantics):
    return pltpu.CompilerParams(dimension_semantics=semantics, vmem_limit_bytes=VMEM_LIMIT_BYTES)


def _sigmoid(x):
    return 1.0 / (1.0 + jnp.exp(-x))


def _rms(x, eps):
    return x * lax.rsqrt(jnp.mean(x * x, axis=-1, keepdims=True) + eps)


def _split3(x):
    hi = x.astype(BF16)
    rest = x - hi.astype(F32)
    mid = rest.astype(BF16)
    return hi, mid, (rest - mid.astype(F32)).astype(BF16)


def _pack_halves(x):
    n = x.shape[1] // 2
    bits = lambda v: lax.bitcast_convert_type(v.astype(BF16).astype(F32), jnp.uint32)
    return (bits(x[:, :n]) >> 16) | bits(x[:, n:])


def _unpack_halves(w):
    lo = lax.bitcast_convert_type(w << 16, F32)
    hi = lax.bitcast_convert_type(w & jnp.uint32(0xFFFF0000), F32)
    return jnp.concatenate([lo, hi], axis=1)


SC_GATHER_ROWS = 64


def _gather_rows(table, idx):
    n_idx = idx.shape[0]
    width = table.shape[1]
    info = plsc.get_sparse_core_info()
    n_workers = info.num_cores * info.num_subcores
    per_worker = n_idx // n_workers
    n_chunks = per_worker // SC_GATHER_ROWS
    assert n_idx % n_workers == 0 and per_worker % SC_GATHER_ROWS == 0 and n_chunks % 2 == 0, (n_idx, n_workers)
    mesh = plsc.VectorSubcoreMesh(core_axis_name="c", subcore_axis_name="s")
    idx_buf = pltpu.VMEM((SC_GATHER_ROWS,), jnp.int32)
    row_buf = pltpu.VMEM((SC_GATHER_ROWS, width), table.dtype)

    @functools.partial(
        pl.kernel, mesh=mesh,
        out_type=jax.ShapeDtypeStruct((n_idx, width), table.dtype),
        scratch_types=[idx_buf, idx_buf, row_buf, row_buf, pltpu.SemaphoreType.DMA, pltpu.SemaphoreType.DMA],
    )
    def gather(table_hbm, idx_hbm, out_hbm, idx0, idx1, rows0, rows1, sem0, sem1):
        worker = lax.axis_index("s") * info.num_cores + lax.axis_index("c")
        slots = ((idx0, rows0, sem0), (idx1, rows1, sem1))

        def offset(c):
            return pl.multiple_of(worker * per_worker + c * SC_GATHER_ROWS, SUBLANES)

        def stream(slot):
            idx_v, rows_v, sem = slot
            return pltpu.make_async_copy(table_hbm.at[idx_v], rows_v, sem)

        def start(c, slot):
            pltpu.sync_copy(idx_hbm.at[pl.ds(offset(c), SC_GATHER_ROWS)], slot[0])
            stream(slot).start()

        start(0, slots[0])

        @pl.loop(0, n_chunks, step=2)
        def _(c0):
            for b in range(2):
                c = c0 + b
                stream(slots[b]).wait()

                @pl.when(c + 1 < n_chunks)
                def _():
                    start(c + 1, slots[1 - b])

                pltpu.sync_copy(slots[b][1], out_hbm.at[pl.ds(offset(c), SC_GATHER_ROWS)])

    return gather(table, idx)


def _norm_matmul_kernel(x_ref, g_ref, w_ref, ws_ref, o_ref, os_ref, h_ref):
    @pl.when(pl.program_id(1) == 0)
    def _():
        h = (_rms(x_ref[...], NORM_EPS) * g_ref[...]).astype(h_ref.dtype)
        h_ref[...] = h
        os_ref[...] = jnp.dot(h, ws_ref[...], preferred_element_type=F32)

    o_ref[...] = jnp.dot(h_ref[...], w_ref[...], preferred_element_type=F32).astype(o_ref.dtype)


def _norm_matmul(x, g, w, w_side, tm, tn):
    t, d = x.shape
    n = w.shape[1]
    ns = w_side.shape[1]
    return pl.pallas_call(
        _norm_matmul_kernel,
        grid=(t // tm, n // tn),
        in_specs=[pl.BlockSpec((tm, d), lambda i, j: (i, 0)),
                  pl.BlockSpec((1, d), lambda i, j: (0, 0)),
                  pl.BlockSpec((d, tn), lambda i, j: (0, j)),
                  pl.BlockSpec((d, ns), lambda i, j: (0, 0))],
        out_specs=[pl.BlockSpec((tm, tn), lambda i, j: (i, j)),
                   pl.BlockSpec((tm, ns), lambda i, j: (i, 0))],
        out_shape=[jax.ShapeDtypeStruct((t, n), BF16), jax.ShapeDtypeStruct((t, ns), F32)],
        scratch_shapes=[pltpu.VMEM((tm, d), BF16)],
        compiler_params=_params("parallel", "arbitrary"),
        name="norm_matmul",
    )(x, g.reshape(1, d), w, w_side)


ATT_ONES = 16
LOG2E = 1.4426950408889634


def _attn_kernel(items_ref, lam_ref, q_ref, k_ref, v_ref, sw_ref, o_ref,
                 vt_ref, q2_ref, m_ref, acc_ref, sa_ref, ca_ref, sb_ref, cb_ref, *, tb, n_off, n_diag, out_scale):
    hw = 2 * ATT_DH
    nb = vt_ref.shape[0]

    lane = lax.broadcasted_iota(jnp.int32, (tb, hw), 1)
    for c in range(nb):
        rows = slice(c * tb, (c + 1) * tb)
        vt_ref[c, 0:hw, :] = v_ref[rows, :].astype(F32).T.astype(BF16)
        vt_ref[c, hw:hw + ATT_ONES, :] = jnp.ones((ATT_ONES, tb), BF16)
        q = (q_ref[rows, :].astype(F32) * (ATT_DH ** -0.5 * LOG2E)).astype(BF16)
        zero = jnp.zeros_like(q)
        q2_ref[c, 0:tb, :] = jnp.where(lane < ATT_DH, q, zero)
        q2_ref[c, tb:2 * tb, :] = jnp.where(lane >= ATT_DH, q, zero)
        m_ref[c] = jnp.full(m_ref.shape[1:], -jnp.inf, F32)
        acc_ref[c] = jnp.zeros(acc_ref.shape[1:], F32)

    def fill(buf, t, masked):
        s_ref, cmax_ref = buf
        qi, j = items_ref[0, t], items_ref[1, t]
        k = k_ref[pl.ds(pl.multiple_of(j * tb, tb), tb), :]
        st = lax.dot_general(k, q2_ref[qi], (((1,), (1,)), ((), ())), preferred_element_type=F32)
        if masked:
            r = lax.broadcasted_iota(jnp.int32, st.shape, 0)
            c = lax.broadcasted_iota(jnp.int32, st.shape, 1)
            st = jnp.where(r <= jnp.where(c >= tb, c - tb, c), st, -jnp.inf)
        s_ref[...] = st
        cmax_ref[...] = jnp.max(st, axis=0, keepdims=True)

    def update(buf, t, last):
        s_ref, cmax_ref = buf
        qi, j = items_ref[0, t], items_ref[1, t]
        m_old = m_ref[qi]
        m_new = jnp.maximum(m_old, cmax_ref[...])
        p = jnp.exp2(s_ref[...] - m_new).astype(BF16)
        acc = jnp.exp2(m_old - m_new) * acc_ref[qi] + jnp.dot(vt_ref[j], p, preferred_element_type=F32)
        if not last:
            acc_ref[qi] = acc
            m_ref[qi] = m_new
        else:
            o_t = acc[:hw] * (1.0 / acc[hw:hw + 1])
            o = (o_t[:, :tb] - lam_ref[0] * o_t[:, tb:]).T
            o_ref[pl.ds(pl.multiple_of(qi * tb, tb), tb), :] = (
                _rms(o, 1e-5) * (sw_ref[...] * out_scale)).astype(o_ref.dtype)

    def run(first, count, diag, cur, other):
        trips = (count - 1) // 2

        def body(i, carry):
            t = first + 2 * i
            fill(other, t + 1, diag)
            update(cur, t, diag)
            fill(cur, t + 2, diag)
            update(other, t + 1, diag)
            return carry

        lax.fori_loop(0, trips, body, 0)
        if count - 1 > 2 * trips:
            t = first + 2 * trips
            fill(other, t + 1, diag)
            update(cur, t, diag)
            cur, other = other, cur
        return cur, other

    cur, other = (sa_ref, ca_ref), (sb_ref, cb_ref)
    if n_off:
        fill(cur, 0, False)
        cur, other = run(0, n_off, False, cur, other)
        fill(other, n_off, True)
        update(cur, n_off - 1, False)
        cur, other = other, cur
    else:
        fill(cur, 0, True)
    cur, other = run(n_off, n_diag, True, cur, other)
    update(cur, n_off + n_diag - 1, True)


def _diff_attention(proj3, lam, subln_w, lam_init, tb):
    b, s, _ = proj3.shape
    hw = 2 * ATT_DH
    nb = s // tb
    off = [(qi, j) for qi in range(nb) for j in range(qi)]
    items = jnp.asarray(list(zip(*(off + [(qi, qi) for qi in range(nb)]))), jnp.int32)
    kern = functools.partial(_attn_kernel, tb=tb, n_off=len(off), n_diag=nb, out_scale=1.0 - lam_init)
    seq = lambda col: pl.BlockSpec((None, s, hw), lambda bb, h: (bb, 0, col // hw + h))
    return pl.pallas_call(
        kern,
        grid=(b, ATT_HEADS),
        in_specs=[pl.BlockSpec(memory_space=pltpu.SMEM), pl.BlockSpec(memory_space=pltpu.SMEM),
                  seq(COL_Q), seq(COL_K), seq(COL_V), pl.BlockSpec((1, hw), lambda bb, h: (0, 0))],
        out_specs=pl.BlockSpec((None, s, hw), lambda bb, h: (bb, 0, h)),
        out_shape=jax.ShapeDtypeStruct((b, s, ATT_WIDTH), BF16),
        scratch_shapes=[pltpu.VMEM((nb, hw + ATT_ONES, tb), BF16), pltpu.VMEM((nb, 2 * tb, hw), BF16),
                        pltpu.VMEM((nb, 1, 2 * tb), F32), pltpu.VMEM((nb, hw + ATT_ONES, 2 * tb), F32),
                        pltpu.VMEM((tb, 2 * tb), F32), pltpu.VMEM((1, 2 * tb), F32),
                        pltpu.VMEM((tb, 2 * tb), F32), pltpu.VMEM((1, 2 * tb), F32)],
        compiler_params=_params("parallel", "parallel"),
        name="diff_attention",
    )(items, lam.reshape(1).astype(F32), proj3, proj3, proj3, subln_w.reshape(1, hw).astype(F32))


CONV_HALO = 32
CONV_ROWS = 64


def _cconv_kernel(a_ref, g_ref, w_ref, b_ref, lnw_ref, lnb_ref, o_ref, buf_ref, *, ts):
    i = pl.program_id(1)

    @pl.when(i == 0)
    def _():
        buf_ref[0:CONV_HALO, :] = jnp.zeros((CONV_HALO, CONV_CH), F32)

    @pl.when(i > 0)
    def _():
        buf_ref[0:CONV_HALO, :] = buf_ref[ts:ts + CONV_HALO, :]

    buf_ref[CONV_HALO:CONV_HALO + ts, :] = a_ref[...].astype(F32) * _sigmoid(g_ref[...].astype(F32))

    first = CONV_HALO - (CONV_K - 1)
    for c in range(ts // CONV_ROWS):
        base = c * CONV_ROWS
        acc = jnp.zeros((CONV_ROWS, CONV_CH), F32) + b_ref[...]
        for res in range(SUBLANES):
            taps = [k for k in range(CONV_K) if (first + k) % SUBLANES == res]
            rows = CONV_ROWS + (SUBLANES if res else 0)
            part = None
            for k in taps:
                lo = base + first + k - res
                term = w_ref[k:k + 1, :] * buf_ref[lo:lo + rows, :]
                part = term if part is None else part + term
            acc = acc + part[res:res + CONV_ROWS]
        mu = jnp.mean(acc, axis=-1, keepdims=True)
        xc = acc - mu
        var = jnp.mean(xc * xc, axis=-1, keepdims=True)
        y = xc * lax.rsqrt(var + 1e-5) * lnw_ref[...] + lnb_ref[...]
        o_ref[base:base + CONV_ROWS, :] = (y * _sigmoid(y)).astype(o_ref.dtype)


def _conformer_conv(proj3, dw_w, dw_b, ln_w, ln_b, ts):
    b, s, _ = proj3.shape
    c = CONV_CH
    row = lambda a: a.reshape(1, c).astype(F32)
    return pl.pallas_call(
        functools.partial(_cconv_kernel, ts=ts),
        grid=(b, s // ts),
        in_specs=[pl.BlockSpec((None, ts, c), lambda bb, i: (bb, i, COL_U // c)),
                  pl.BlockSpec((None, ts, c), lambda bb, i: (bb, i, COL_U // c + 1)),
                  pl.BlockSpec((CONV_K, c), lambda bb, i: (0, 0)),
                  pl.BlockSpec((1, c), lambda bb, i: (0, 0)),
                  pl.BlockSpec((1, c), lambda bb, i: (0, 0)),
                  pl.BlockSpec((1, c), lambda bb, i: (0, 0))],
        out_specs=pl.BlockSpec((None, ts, c), lambda bb, i: (bb, i, 0)),
        out_shape=jax.ShapeDtypeStruct((b, s, c), BF16),
        scratch_shapes=[pltpu.VMEM((CONV_HALO + ts, c), F32)],
        compiler_params=_params("parallel", "arbitrary"),
        name="conformer_conv",
    )(proj3, proj3, dw_w.astype(F32), row(dw_b), row(ln_w), row(ln_b))


SSD_HALO = 16


def _ssd_kernel(z_ref, xbc_ref, dt_ref, cw_ref, cb_ref, dtb_ref, a_ref, dsk_ref, nw_ref, e_ref,
                o_ref, buf_ref, h_ref, *, chunk):
    L = chunk
    ci = pl.program_id(1)
    rows = range(z_ref.shape[0])

    @pl.when(ci == 0)
    def _():
        for bb in rows:
            buf_ref[bb, 0:SSD_HALO, :] = jnp.zeros((SSD_HALO, SSM_CONV_DIM), BF16)
        h_ref[...] = jnp.zeros(h_ref.shape, F32)

    @pl.when(ci > 0)
    def _():
        for bb in rows:
            buf_ref[bb, 0:SSD_HALO, :] = buf_ref[bb, L:L + SSD_HALO, :]

    for bb in rows:
        _ssd_chunk(z_ref.at[bb], xbc_ref.at[bb], dt_ref.at[bb], cw_ref, cb_ref, dtb_ref, a_ref, dsk_ref, nw_ref,
                   e_ref, o_ref.at[bb], buf_ref.at[bb], h_ref.at[bb], L)


def _ssd_chunk(z_ref, xbc_ref, dt_ref, cw_ref, cb_ref, dtb_ref, a_ref, dsk_ref, nw_ref, e_ref,
               o_ref, buf_ref, h_ref, L):
    gw = SSM_D_INNER // SSM_GROUPS
    hpg = SSM_HEADS // SSM_GROUPS

    buf_ref[SSD_HALO:SSD_HALO + L, :] = xbc_ref[...]
    xraw = buf_ref[...]
    r = lax.broadcasted_iota(jnp.int32, (L, SSD_HALO + L), 0)
    c = lax.broadcasted_iota(jnp.int32, (L, SSD_HALO + L), 1)
    xc = jnp.zeros((L, SSM_CONV_DIM), F32) + cb_ref[...]
    for k in range(SSM_CONV_K):
        shift = jnp.where(c == r + (SSD_HALO - (SSM_CONV_K - 1) + k), 1.0, 0.0).astype(BF16)
        xc = xc + cw_ref[k:k + 1, :] * jnp.dot(shift, xraw, preferred_element_type=F32)
    xc = xc * _sigmoid(xc)
    xs = xc[:, :SSM_D_INNER]
    bm = xc[:, SSM_D_INNER:SSM_D_INNER + SSM_GROUPS * SSM_STATE]
    cm = xc[:, SSM_D_INNER + SSM_GROUPS * SSM_STATE:]

    dtr = dt_ref[...] + dtb_ref[...]
    dt = jnp.maximum(dtr, 0.0) + jnp.log(1.0 + jnp.exp(-jnp.abs(dtr)))
    da = dt * a_ref[...]
    row = lax.broadcasted_iota(jnp.int32, (L, L), 0)
    col = lax.broadcasted_iota(jnp.int32, (L, L), 1)
    causal = col <= row
    a_cum = sum(jnp.dot(jnp.where(causal, 1.0, 0.0).astype(BF16), part, preferred_element_type=F32)
                for part in _split3(da))
    a_cum_t = a_cum.T

    expand = e_ref[...]
    per_head = jnp.concatenate([dt, jnp.exp(a_cum), jnp.exp(a_cum[L - 1:L, :] - a_cum)], axis=0)
    full = sum(jnp.dot(part, expand, preferred_element_type=F32) for part in _split3(per_head))
    dt_full, ea_full, dend_full = full[:L], full[L:2 * L], full[2 * L:]
    cdec_full = ea_full[L - 1:L, :]

    xdt = xs * dt_full
    xdt_b = xdt.astype(BF16)
    xw_b = (xdt * dend_full).astype(BF16)

    y_parts = []
    for g in range(SSM_GROUPS):
        cg = cm[:, g * SSM_STATE:(g + 1) * SSM_STATE].astype(BF16)
        bg_t = bm[:, g * SSM_STATE:(g + 1) * SSM_STATE].T.astype(BF16)
        cbm = jnp.dot(cg, bg_t, preferred_element_type=F32)
        h_in = h_ref[:, g * gw:(g + 1) * gw]
        y_off = jnp.dot(cg, h_in.astype(BF16), preferred_element_type=F32)
        states = jnp.dot(bg_t, xw_b[:, g * gw:(g + 1) * gw], preferred_element_type=F32)
        h_ref[:, g * gw:(g + 1) * gw] = h_in * cdec_full[:, g * gw:(g + 1) * gw] + states
        yd = []
        for r in range(hpg):
            hh = g * hpg + r
            seg = a_cum[:, hh:hh + 1] - a_cum_t[hh:hh + 1, :]
            dec = jnp.exp(jnp.where(causal, seg, -jnp.inf))
            yd.append(jnp.dot((cbm * dec).astype(BF16), xdt_b[:, hh * SSM_HEADDIM:(hh + 1) * SSM_HEADDIM],
                              preferred_element_type=F32))
        y_g = (jnp.concatenate(yd, axis=-1) + y_off * ea_full[:, g * gw:(g + 1) * gw]
               + xs[:, g * gw:(g + 1) * gw] * dsk_ref[:, g * gw:(g + 1) * gw])
        zg = z_ref[:, g * gw:(g + 1) * gw].astype(F32)
        y_parts.append(_rms(y_g * (zg * _sigmoid(zg)), 1e-5))
    o_ref[...] = (jnp.concatenate(y_parts, axis=-1) * nw_ref[...]).astype(o_ref.dtype)


def _mamba2_ssd(proj3, dt3, conv_w, conv_b, dt_bias, a_log, d_skip, norm_w):
    b, s, _ = proj3.shape
    L = SSM_CHUNK
    di, cd = SSM_D_INNER, SSM_CONV_DIM
    pad_h = lambda a: jnp.pad(a.astype(F32), (0, LANES - SSM_HEADS)).reshape(1, LANES)
    head_of_channel = jnp.arange(di, dtype=jnp.int32) // SSM_HEADDIM
    expand = (jnp.arange(LANES, dtype=jnp.int32)[:, None] == head_of_channel[None, :]).astype(BF16)
    const = lambda shape: pl.BlockSpec(shape, lambda bb, c: (0, 0))
    nb = 2 if b % 2 == 0 else 1
    return pl.pallas_call(
        functools.partial(_ssd_kernel, chunk=L),
        grid=(b // nb, s // L),
        in_specs=[pl.BlockSpec((nb, L, di), lambda bb, c: (bb, c, COL_Z // di)),
                  pl.BlockSpec((nb, L, cd), lambda bb, c: (bb, c, COL_XBC // cd)),
                  pl.BlockSpec((nb, L, LANES), lambda bb, c: (bb, c, 0)),
                  const((SSM_CONV_K, cd)), const((1, cd)), const((1, LANES)), const((1, LANES)),
                  const((1, di)), const((1, di)), const((LANES, di))],
        out_specs=pl.BlockSpec((nb, L, di), lambda bb, c: (bb, c, 0)),
        out_shape=jax.ShapeDtypeStruct((b, s, di), BF16),
        scratch_shapes=[pltpu.VMEM((nb, SSD_HALO + L, cd), BF16), pltpu.VMEM((nb, SSM_STATE, di), F32)],
        compiler_params=_params("parallel", "arbitrary"),
        name="mamba2_ssd",
    )(proj3, proj3, dt3, conv_w.astype(F32), conv_b.reshape(1, cd).astype(F32), pad_h(dt_bias),
      pad_h(-jnp.exp(a_log.astype(F32))), jnp.repeat(d_skip.astype(F32), SSM_HEADDIM).reshape(1, di),
      norm_w.reshape(1, di).astype(F32), expand)


def _merge_kernel(x_ref, ya_ref, yc_ref, ys_ref, ga_ref, gc_ref, gs_ref, wa_ref, wc_ref, ws_ref, wm_ref, o_ref):
    def branch(y_ref, g_ref, w_ref):
        return _sigmoid(g_ref[...].astype(F32)) * jnp.dot(y_ref[...], w_ref[...], preferred_element_type=F32)

    merged = branch(ya_ref, ga_ref, wa_ref) + branch(yc_ref, gc_ref, wc_ref) + branch(ys_ref, gs_ref, ws_ref)
    o_ref[...] = x_ref[...] + jnp.dot(merged.astype(BF16), wm_ref[...], preferred_element_type=F32)


def _merge(x, ya, yc, ys, proj, wa, wc, ws, wm, tm):
    t, d = x.shape
    w = ya.shape[1]
    rows = lambda width, cb: pl.BlockSpec((tm, width), lambda i: (i, cb))
    const = lambda shape: pl.BlockSpec(shape, lambda i: (0, 0))
    return pl.pallas_call(
        _merge_kernel,
        grid=(t // tm,),
        in_specs=[rows(d, 0), rows(w, 0), rows(w, 0), rows(w, 0),
                  rows(d, COL_G // d), rows(d, COL_G // d + 1), rows(d, COL_G // d + 2),
                  const((w, d)), const((w, d)), const((w, d)), const((d, d))],
        out_specs=rows(d, 0),
        out_shape=jax.ShapeDtypeStruct((t, d), F32),
        compiler_params=_params("parallel"),
        name="gated_merge",
    )(x, ya, yc, ys, proj, proj, proj, wa, wc, ws, wm)


def _ffn_kernel(x_ref, g_ref, wg_ref, wu_ref, wd_ref, o_ref):
    x = x_ref[...]
    h = (_rms(x, NORM_EPS) * g_ref[...]).astype(BF16)
    a = jnp.dot(h, wg_ref[...], preferred_element_type=F32)
    u = jnp.dot(h, wu_ref[...], preferred_element_type=F32)
    o_ref[...] = x + jnp.dot((a * _sigmoid(a) * u).astype(BF16), wd_ref[...], preferred_element_type=F32)


def _ffn_dense(x, g, wg, wu, wd, tm):
    t, d = x.shape
    f = wg.shape[1]
    resident = lambda shape: pl.BlockSpec(shape, lambda i: (0, 0), pipeline_mode=pl.Buffered(1))
    return pl.pallas_call(
        _ffn_kernel,
        grid=(t // tm,),
        in_specs=[pl.BlockSpec((tm, d), lambda i: (i, 0)),
                  pl.BlockSpec((1, d), lambda i: (0, 0)),
                  resident((d, f)), resident((d, f)), resident((f, d))],
        out_specs=pl.BlockSpec((tm, d), lambda i: (i, 0)),
        out_shape=jax.ShapeDtypeStruct((t, d), F32),
        compiler_params=_params("parallel"),
        name="ffn_dense",
    )(x, g.reshape(1, d), wg, wu, wd)


def _router_kernel(x_ref, g_ref, wr_ref, br_ref, h_ref, idx_ref, wt_ref, rank_ref, cnt_ref, tri_ref, base_ref):
    i = pl.program_id(0)
    tm = x_ref.shape[0]

    @pl.when(i == 0)
    def _():
        r = lax.broadcasted_iota(jnp.int32, (tm, tm), 0)
        c = lax.broadcasted_iota(jnp.int32, (tm, tm), 1)
        tri_ref[...] = jnp.where(c < r, 1.0, 0.0).astype(BF16)
        base_ref[...] = jnp.zeros(base_ref.shape, F32)

    h = _rms(x_ref[...], NORM_EPS) * g_ref[...]
    h_ref[...] = _pack_halves(h)
    logits = jnp.dot(h, wr_ref[...], precision=HIGHEST, preferred_element_type=F32) + br_ref[...]
    lane = lax.broadcasted_iota(jnp.int32, logits.shape, 1)
    logits = jnp.where(lane < N_EXPERTS, logits, -jnp.inf)
    m1 = jnp.max(logits, axis=-1, keepdims=True)
    i1 = jnp.min(jnp.where(logits == m1, lane, LANES), axis=-1, keepdims=True)
    rest = jnp.where(lane == i1, -jnp.inf, logits)
    m2 = jnp.max(rest, axis=-1, keepdims=True)
    i2 = jnp.min(jnp.where(rest == m2, lane, LANES), axis=-1, keepdims=True)
    e = jnp.exp(m2 - m1)
    w1 = 1.0 / (1.0 + e)
    idx_ref[...] = jnp.where(lane == 0, i1, i2)[:, :TOP_K]
    wt_ref[...] = jnp.where(lane == 0, w1, e * w1)[:, :TOP_K]

    chosen = jnp.logical_or(lane == i1, lane == i2)
    before = jnp.dot(tri_ref[...], jnp.where(chosen, 1.0, 0.0).astype(BF16), preferred_element_type=F32)
    before = before + base_ref[...]
    r1 = jnp.sum(jnp.where(lane == i1, before, 0.0), axis=-1, keepdims=True)
    r2 = jnp.sum(jnp.where(lane == i2, before, 0.0), axis=-1, keepdims=True)
    rank_ref[...] = jnp.where(lane == 0, r1, r2)[:, :TOP_K].astype(jnp.int32)
    base_ref[...] += jnp.sum(jnp.where(chosen, 1.0, 0.0), axis=0, keepdims=True)
    cnt_ref[...] = base_ref[...].astype(jnp.int32)


def _router(x, g, w_router, b_router, tm):
    t, d = x.shape
    wr = jnp.pad(w_router.astype(F32), ((0, 0), (0, LANES - N_EXPERTS)))
    br = jnp.pad(b_router.astype(F32), (0, LANES - N_EXPERTS)).reshape(1, LANES)
    pair = pl.BlockSpec((tm, TOP_K), lambda i: (i, 0))
    return pl.pallas_call(
        _router_kernel,
        grid=(t // tm,),
        in_specs=[pl.BlockSpec((tm, d), lambda i: (i, 0)),
                  pl.BlockSpec((1, d), lambda i: (0, 0)),
                  pl.BlockSpec((d, LANES), lambda i: (0, 0)),
                  pl.BlockSpec((1, LANES), lambda i: (0, 0))],
        out_specs=[pl.BlockSpec((tm, d // 2), lambda i: (i, 0)), pair, pair, pair,
                   pl.BlockSpec((1, LANES), lambda i: (0, 0))],
        out_shape=[jax.ShapeDtypeStruct((t, d // 2), jnp.uint32),
                   jax.ShapeDtypeStruct((t, TOP_K), jnp.int32),
                   jax.ShapeDtypeStruct((t, TOP_K), F32),
                   jax.ShapeDtypeStruct((t, TOP_K), jnp.int32),
                   jax.ShapeDtypeStruct((1, LANES), jnp.int32)],
        scratch_shapes=[pltpu.VMEM((tm, tm), BF16), pltpu.VMEM((1, LANES), F32)],
        compiler_params=_params("arbitrary"),
        name="moe_router",
    )(x, g.reshape(1, d), wr, br)


def _expert_kernel(be_ref, nu_ref, x_ref, wg_ref, wu_ref, wd_ref, o_ref, xb_ref, acc_ref):
    i = pl.program_id(0)
    j = pl.program_id(1)
    last = pl.num_programs(1) - 1
    used = i < nu_ref[0]

    @pl.when(jnp.logical_and(used, j == 0))
    def _():
        xb_ref[...] = _unpack_halves(x_ref[...]).astype(BF16)
        acc_ref[...] = jnp.zeros(acc_ref.shape, F32)

    @pl.when(used)
    def _():
        x = xb_ref[...]
        a = jnp.dot(x, wg_ref[...], preferred_element_type=F32)
        u = jnp.dot(x, wu_ref[...], preferred_element_type=F32)
        acc_ref[...] += jnp.dot((a * _sigmoid(a) * u).astype(BF16), wd_ref[...], preferred_element_type=F32)

    @pl.when(jnp.logical_and(used, j == last))
    def _():
        o_ref[...] = _pack_halves(acc_ref[...])

    @pl.when(jnp.logical_and(jnp.logical_not(used), j == last))
    def _():
        o_ref[...] = jnp.zeros(o_ref.shape, o_ref.dtype)


def _expert_ffn(xs, block_exp, n_used, wg, wu, wd, bm, tf):
    n_rows, half = xs.shape
    d = 2 * half
    f = wg.shape[2]
    nj = f // tf

    def jj(i, j, nu):
        return jnp.where(i < nu[0], j, nj - 1)

    grid_spec = pltpu.PrefetchScalarGridSpec(
        num_scalar_prefetch=2,
        grid=(n_rows // bm, nj),
        in_specs=[pl.BlockSpec((bm, half), lambda i, j, be, nu: (i, 0)),
                  pl.BlockSpec((None, d, tf), lambda i, j, be, nu: (be[i], 0, jj(i, j, nu))),
                  pl.BlockSpec((None, d, tf), lambda i, j, be, nu: (be[i], 0, jj(i, j, nu))),
                  pl.BlockSpec((None, tf, d), lambda i, j, be, nu: (be[i], jj(i, j, nu), 0))],
        out_specs=pl.BlockSpec((bm, half), lambda i, j, be, nu: (i, 0)),
        scratch_shapes=[pltpu.VMEM((bm, d), BF16), pltpu.VMEM((bm, d), F32)],
    )
    return pl.pallas_call(
        _expert_kernel,
        grid_spec=grid_spec,
        out_shape=jax.ShapeDtypeStruct((n_rows, half), jnp.uint32),
        compiler_params=_params("parallel", "arbitrary"),
        name="moe_experts",
    )(block_exp, n_used, xs, wg, wu, wd)


def _route(top_idx, rank, counts, bm):
    t = top_idx.shape[0]
    tk = t * TOP_K
    padded = (counts + bm - 1) // bm * bm
    end_padded = jnp.cumsum(padded)
    start_padded = end_padded - padded
    experts = jnp.arange(N_EXPERTS, dtype=jnp.int32)
    dest = rank + jnp.sum(jnp.where(top_idx[..., None] == experts, start_padded, 0), axis=-1)
    n_blocks = tk // bm + N_EXPERTS
    flat_tok = jnp.repeat(jnp.arange(t, dtype=jnp.int32), TOP_K)
    row_tok = jnp.zeros((n_blocks * bm,), jnp.int32).at[dest.reshape(tk)].set(flat_tok, unique_indices=True)
    block_start = jnp.arange(n_blocks, dtype=jnp.int32) * bm
    block_exp = jnp.minimum(jnp.sum(block_start[:, None] >= end_padded[None, :], axis=1), N_EXPERTS - 1)
    n_used = (end_padded[-1] // bm).reshape(1)
    return row_tok, dest, block_exp.astype(jnp.int32), n_used.astype(jnp.int32)


def _combine_kernel(x_ref, a_ref, b_ref, w_ref, g_ref, o_ref, *, final_norm):
    w = w_ref[...]
    y = x_ref[...] + w[:, 0:1] * _unpack_halves(a_ref[...]) + w[:, 1:2] * _unpack_halves(b_ref[...])
    o_ref[...] = _rms(y, NORM_EPS) * g_ref[...] if final_norm else y


def _combine(x, ab, w, g, tm):
    t, d = x.shape
    spec = pl.BlockSpec((tm, d), lambda i: (i, 0))
    final_norm = g is not None
    gain = g.reshape(1, d) if final_norm else jnp.ones((1, d), F32)
    return pl.pallas_call(
        functools.partial(_combine_kernel, final_norm=final_norm), grid=(t // tm,),
        in_specs=[spec, pl.BlockSpec((tm, d // 2), lambda i: (i, 0)),
                  pl.BlockSpec((tm, d // 2), lambda i: (i + t // tm, 0)),
                  pl.BlockSpec((tm, TOP_K), lambda i: (i, 0)),
                  pl.BlockSpec((1, d), lambda i: (0, 0))],
        out_specs=spec,
        out_shape=jax.ShapeDtypeStruct((t, d), F32), compiler_params=_params("parallel"),
        name="moe_combine",
    )(x, ab, ab, w, gain)


def _moe(x, g, w_router, b_router, wg, wu, wd, final_g, tm, bm, tf):
    h, top_idx, top_w, rank, counts = _router(x, g, w_router, b_router, tm)
    row_tok, dest, block_exp, n_used = _route(top_idx, rank, counts[0, :N_EXPERTS], bm)
    xs = _gather_rows(h, row_tok)
    ys = _expert_ffn(xs, block_exp, n_used, wg, wu, wd, bm, tf)
    picked = _gather_rows(ys, jnp.concatenate([dest[:, 0], dest[:, 1]]))
    return _combine(x, picked, top_w, final_g, tm)


def _final_norm_kernel(x_ref, g_ref, o_ref):
    o_ref[...] = _rms(x_ref[...], NORM_EPS) * g_ref[...]


def _final_norm(x, g, tm):
    t, d = x.shape
    return pl.pallas_call(
        _final_norm_kernel, grid=(t // tm,),
        in_specs=[pl.BlockSpec((tm, d), lambda i: (i, 0)), pl.BlockSpec((1, d), lambda i: (0, 0))],
        out_specs=pl.BlockSpec((tm, d), lambda i: (i, 0)),
        out_shape=jax.ShapeDtypeStruct((t, d), F32), compiler_params=_params("parallel"),
        name="final_norm",
    )(x, g.reshape(1, d))


def _tiles(t, s):
    pick = lambda n, pref: pref if n % pref == 0 else n
    return dict(tm_proj=pick(t, 1024), tn_proj=PROJ_COLS // 4, tb_attn=pick(s, 512), ts_conv=pick(s, 512),
                tm_merge=pick(t, 512), tm_ffn=pick(t, 512), tm_row=pick(t, 1024), bm_moe=pick(t, 512))


def kernel(x, ln_mix_w, w_in, attn_lq1, attn_lk1, attn_lq2, attn_lk2, attn_subln_w, attn_w_o, conv_dw_w, conv_dw_b, conv_ln_w, conv_ln_b, conv_w_pw, ssm_conv_w, ssm_conv_b, ssm_dt_bias, ssm_A_log, ssm_D, ssm_norm_w, ssm_w_out, w_mix_out, ln_ffn_w, ffn_w_gate, ffn_w_up, ffn_w_down, moe_w_router, moe_b_router, moe_w_gate, moe_w_up, moe_w_down, ln_final_w):
    b, s, d = x.shape
    t = b * s
    depth = w_in.shape[0]
    tl = _tiles(t, s)
    xf = x.reshape(t, d)
    for l in range(depth):
        w_main = jnp.concatenate([w_in[l][:, :COL_DT], w_in[l][:, COL_DT + SSM_HEADS:]], axis=1).astype(BF16)
        w_dt = jnp.pad(w_in[l][:, COL_DT:COL_DT + SSM_HEADS], ((0, 0), (0, LANES - SSM_HEADS))).astype(BF16)
        proj, dt = _norm_matmul(xf, ln_mix_w[l], w_main, w_dt, tl["tm_proj"], tl["tn_proj"])
        proj3 = proj.reshape(b, s, PROJ_COLS)

        lam_init = 0.8 - 0.6 * math.exp(-0.3 * l)
        lam = (jnp.exp(jnp.sum(attn_lq1[l].astype(F32) * attn_lk1[l].astype(F32)))
               - jnp.exp(jnp.sum(attn_lq2[l].astype(F32) * attn_lk2[l].astype(F32))) + lam_init)
        y_att = _diff_attention(proj3, lam, attn_subln_w[l], lam_init, tl["tb_attn"])
        y_conv = _conformer_conv(proj3, conv_dw_w[l], conv_dw_b[l], conv_ln_w[l], conv_ln_b[l], tl["ts_conv"])
        y_ssm = _mamba2_ssd(proj3, dt.reshape(b, s, LANES), ssm_conv_w[l], ssm_conv_b[l], ssm_dt_bias[l],
                            ssm_A_log[l], ssm_D[l], ssm_norm_w[l])
        xf = _merge(xf, y_att.reshape(t, ATT_WIDTH), y_conv.reshape(t, CONV_CH), y_ssm.reshape(t, SSM_D_INNER),
                    proj, attn_w_o[l].astype(BF16), conv_w_pw[l].astype(BF16), ssm_w_out[l].astype(BF16),
                    w_mix_out[l].astype(BF16), tl["tm_merge"])

        i = l // 2
        if l % 2 == 0:
            xf = _ffn_dense(xf, ln_ffn_w[l], ffn_w_gate[i].astype(BF16), ffn_w_up[i].astype(BF16),
                            ffn_w_down[i].astype(BF16), tl["tm_ffn"])
        else:
            f = moe_w_gate.shape[3]
            tf = f // 2 if (f // 2) % (2 * LANES) == 0 else f
            final_g = ln_final_w if l == depth - 1 else None
            xf = _moe(xf, ln_ffn_w[l], moe_w_router[i], moe_b_router[i], moe_w_gate[i].astype(BF16),
                      moe_w_up[i].astype(BF16), moe_w_down[i].astype(BF16), final_g, tl["tm_row"], tl["bm_moe"], tf)
    if depth % 2 == 1:
        xf = _final_norm(xf, ln_final_w, tl["tm_row"])
    return xf.reshape(b, s, d)
```

```python
import functools
import math

import jax
import jax.numpy as jnp
from jax import lax
from jax.experimental import pallas as pl
from jax.experimental.pallas import tpu as pltpu
from jax.experimental.pallas import tpu_sc as plsc

F32 = jnp.float32
BF16 = jnp.bfloat16
HIGHEST = lax.Precision.HIGHEST

D_MODEL = 1024
ATT_HEADS = 4
ATT_DH = 64
ATT_WIDTH = ATT_HEADS * 2 * ATT_DH
CONV_CH = 512
CONV_K = 31
SSM_D_INNER = 512
SSM_HEADDIM = 64
SSM_HEADS = SSM_D_INNER // SSM_HEADDIM
SSM_GROUPS = 2
SSM_STATE = 128
SSM_CONV_K = 4
SSM_CONV_DIM = SSM_D_INNER + 2 * SSM_GROUPS * SSM_STATE
SSM_CHUNK = 128
N_EXPERTS = 8
TOP_K = 2
NORM_EPS = 1e-6

LANES = 128
SUBLANES = 8
VMEM_LIMIT_BYTES = 48 * 1024 * 1024

COL_Q = 0
COL_K = COL_Q + ATT_WIDTH
COL_V = COL_K + ATT_WIDTH
COL_U = COL_V + ATT_WIDTH
COL_Z = COL_U + 2 * CONV_CH
COL_XBC = COL_Z + SSM_D_INNER
COL_DT = COL_XBC + SSM_CONV_DIM
COL_G = COL_XBC + SSM_CONV_DIM
PROJ_COLS = COL_G + 3 * D_MODEL


def _params(*semantics):
    return pltpu.CompilerParams(dimension_semantics=semantics, vmem_limit_bytes=VMEM_LIMIT_BYTES)


def _sigmoid(x):
    return 1.0 / (1.0 + jnp.exp(-x))


def _rms(x, eps):
    return x * lax.rsqrt(jnp.mean(x * x, axis=-1, keepdims=True) + eps)


def _split3(x):
    hi = x.astype(BF16)
    rest = x - hi.astype(F32)
    mid = rest.astype(BF16)
    return hi, mid, (rest - mid.astype(F32)).astype(BF16)


def _pack_halves(x):
    n = x.shape[1] // 2
    bits = lambda v: lax.bitcast_convert_type(v.astype(BF16).astype(F32), jnp.uint32)
    return (bits(x[:, :n]) >> 16) | bits(x[:, n:])


def _unpack_halves(w):
    lo = lax.bitcast_convert_type(w << 16, F32)
    hi = lax.bitcast_convert_type(w & jnp.uint32(0xFFFF0000), F32)
    return jnp.concatenate([lo, hi], axis=1)


SC_GATHER_ROWS = 64


def _gather_rows(table, idx):
    n_idx = idx.shape[0]
    width = table.shape[1]
    info = plsc.get_sparse_core_info()
    n_workers = info.num_cores * info.num_subcores
    per_worker = n_idx // n_workers
    n_chunks = per_worker // SC_GATHER_ROWS
    assert n_idx % n_workers == 0 and per_worker % SC_GATHER_ROWS == 0 and n_chunks % 2 == 0, (n_idx, n_workers)
    mesh = plsc.VectorSubcoreMesh(core_axis_name="c", subcore_axis_name="s")
    idx_buf = pltpu.VMEM((SC_GATHER_ROWS,), jnp.int32)
    row_buf = pltpu.VMEM((SC_GATHER_ROWS, width), table.dtype)

    @functools.partial(
        pl.kernel, mesh=mesh,
        out_type=jax.ShapeDtypeStruct((n_idx, width), table.dtype),
        scratch_types=[idx_buf, idx_buf, row_buf, row_buf, pltpu.SemaphoreType.DMA, pltpu.SemaphoreType.DMA],
    )
    def gather(table_hbm, idx_hbm, out_hbm, idx0, idx1, rows0, rows1, sem0, sem1):
        worker = lax.axis_index("s") * info.num_cores + lax.axis_index("c")
        slots = ((idx0, rows0, sem0), (idx1, rows1, sem1))

        def offset(c):
            return pl.multiple_of(worker * per_worker + c * SC_GATHER_ROWS, SUBLANES)

        def stream(slot):
            idx_v, rows_v, sem = slot
            return pltpu.make_async_copy(table_hbm.at[idx_v], rows_v, sem)

        def start(c, slot):
            pltpu.sync_copy(idx_hbm.at[pl.ds(offset(c), SC_GATHER_ROWS)], slot[0])
            stream(slot).start()

        start(0, slots[0])

        @pl.loop(0, n_chunks, step=2)
        def _(c0):
            for b in range(2):
                c = c0 + b
                stream(slots[b]).wait()

                @pl.when(c + 1 < n_chunks)
                def _():
                    start(c + 1, slots[1 - b])

                pltpu.sync_copy(slots[b][1], out_hbm.at[pl.ds(offset(c), SC_GATHER_ROWS)])

    return gather(table, idx)


def _scatter_rows(rows, idx, n_out):
    n_rows, width = rows.shape
    fan = idx.shape[0]
    info = plsc.get_sparse_core_info()
    n_workers = info.num_cores * info.num_subcores
    per_worker = n_rows // n_workers
    assert n_rows % n_workers == 0 and per_worker % SC_GATHER_ROWS == 0, (n_rows, n_workers)
    mesh = plsc.VectorSubcoreMesh(core_axis_name="c", subcore_axis_name="s")
    idx_buf = pltpu.VMEM((SC_GATHER_ROWS,), jnp.int32)

    @functools.partial(
        pl.kernel, mesh=mesh,
        out_type=jax.ShapeDtypeStruct((n_out, width), rows.dtype),
        scratch_types=[idx_buf] * fan + [pltpu.VMEM((SC_GATHER_ROWS, width), rows.dtype)],
    )
    def scatter(rows_hbm, idx_hbm, out_hbm, *scratch):
        idx_vs, rows_v = scratch[:fan], scratch[fan]
        worker = lax.axis_index("s") * info.num_cores + lax.axis_index("c")

        @pl.loop(0, per_worker // SC_GATHER_ROWS)
        def _(c):
            off = pl.multiple_of(worker * per_worker + c * SC_GATHER_ROWS, SUBLANES)
            pltpu.sync_copy(rows_hbm.at[pl.ds(off, SC_GATHER_ROWS)], rows_v)
            for k in range(fan):
                pltpu.sync_copy(idx_hbm.at[k, pl.ds(off, SC_GATHER_ROWS)], idx_vs[k])
                pltpu.sync_copy(rows_v, out_hbm.at[idx_vs[k]])

    return scatter(rows, idx)


def _norm_matmul_kernel(x_ref, g_ref, w_ref, ws_ref, o_ref, os_ref, h_ref):
    @pl.when(pl.program_id(1) == 0)
    def _():
        h = (_rms(x_ref[...], NORM_EPS) * g_ref[...]).astype(h_ref.dtype)
        h_ref[...] = h
        os_ref[...] = jnp.dot(h, ws_ref[...], preferred_element_type=F32)

    o_ref[...] = jnp.dot(h_ref[...], w_ref[...], preferred_element_type=F32).astype(o_ref.dtype)


def _norm_matmul(x, g, w, w_side, tm, tn):
    t, d = x.shape
    n = w.shape[1]
    ns = w_side.shape[1]
    return pl.pallas_call(
        _norm_matmul_kernel,
        grid=(t // tm, n // tn),
        in_specs=[pl.BlockSpec((tm, d), lambda i, j: (i, 0)),
                  pl.BlockSpec((1, d), lambda i, j: (0, 0)),
                  pl.BlockSpec((d, tn), lambda i, j: (0, j)),
                  pl.BlockSpec((d, ns), lambda i, j: (0, 0))],
        out_specs=[pl.BlockSpec((tm, tn), lambda i, j: (i, j)),
                   pl.BlockSpec((tm, ns), lambda i, j: (i, 0))],
        out_shape=[jax.ShapeDtypeStruct((t, n), BF16), jax.ShapeDtypeStruct((t, ns), F32)],
        scratch_shapes=[pltpu.VMEM((tm, d), BF16)],
        compiler_params=_params("parallel", "arbitrary"),
        name="norm_matmul",
    )(x, g.reshape(1, d), w, w_side)


ATT_ONES = 16
LOG2E = 1.4426950408889634


def _attn_kernel(items_ref, lam_ref, q_ref, k_ref, v_ref, sw_ref, o_ref,
                 vt_ref, q2_ref, m_ref, acc_ref, sa_ref, ca_ref, sb_ref, cb_ref, *, tb, n_off, n_diag, out_scale):
    hw = 2 * ATT_DH
    nb = vt_ref.shape[0]

    lane = lax.broadcasted_iota(jnp.int32, (tb, hw), 1)
    for c in range(nb):
        rows = slice(c * tb, (c + 1) * tb)
        vt_ref[c, 0:hw, :] = v_ref[rows, :].astype(F32).T.astype(BF16)
        vt_ref[c, hw:hw + ATT_ONES, :] = jnp.ones((ATT_ONES, tb), BF16)
        q = (q_ref[rows, :].astype(F32) * (ATT_DH ** -0.5 * LOG2E)).astype(BF16)
        zero = jnp.zeros_like(q)
        q2_ref[c, 0:tb, :] = jnp.where(lane < ATT_DH, q, zero)
        q2_ref[c, tb:2 * tb, :] = jnp.where(lane >= ATT_DH, q, zero)
        m_ref[c] = jnp.full(m_ref.shape[1:], -jnp.inf, F32)
        acc_ref[c] = jnp.zeros(acc_ref.shape[1:], F32)

    def fill(buf, t, masked):
        s_ref, cmax_ref = buf
        qi, j = items_ref[0, t], items_ref[1, t]
        k = k_ref[pl.ds(pl.multiple_of(j * tb, tb), tb), :]
        st = lax.dot_general(k, q2_ref[qi], (((1,), (1,)), ((), ())), preferred_element_type=F32)
        if masked:
            r = lax.broadcasted_iota(jnp.int32, st.shape, 0)
            c = lax.broadcasted_iota(jnp.int32, st.shape, 1)
            st = jnp.where(r <= jnp.where(c >= tb, c - tb, c), st, -jnp.inf)
        s_ref[...] = st
        cmax_ref[...] = jnp.max(st, axis=0, keepdims=True)

    def update(buf, t, last):
        s_ref, cmax_ref = buf
        qi, j = items_ref[0, t], items_ref[1, t]
        m_old = m_ref[qi]
        m_new = jnp.maximum(m_old, cmax_ref[...])
        p = jnp.exp2(s_ref[...] - m_new).astype(BF16)
        acc = jnp.exp2(m_old - m_new) * acc_ref[qi] + jnp.dot(vt_ref[j], p, preferred_element_type=F32)
        if not last:
            acc_ref[qi] = acc
            m_ref[qi] = m_new
        else:
            o_t = acc[:hw] * (1.0 / acc[hw:hw + 1])
            o = (o_t[:, :tb] - lam_ref[0] * o_t[:, tb:]).T
            o_ref[pl.ds(pl.multiple_of(qi * tb, tb), tb), :] = (
                _rms(o, 1e-5) * (sw_ref[...] * out_scale)).astype(o_ref.dtype)

    def run(first, count, diag, cur, other):
        trips = (count - 1) // 2

        def body(i, carry):
            t = first + 2 * i
            fill(other, t + 1, diag)
            update(cur, t, diag)
            fill(cur, t + 2, diag)
            update(other, t + 1, diag)
            return carry

        lax.fori_loop(0, trips, body, 0)
        if count - 1 > 2 * trips:
            t = first + 2 * trips
            fill(other, t + 1, diag)
            update(cur, t, diag)
            cur, other = other, cur
        return cur, other

    cur, other = (sa_ref, ca_ref), (sb_ref, cb_ref)
    if n_off:
        fill(cur, 0, False)
        cur, other = run(0, n_off, False, cur, other)
        fill(other, n_off, True)
        update(cur, n_off - 1, False)
        cur, other = other, cur
    else:
        fill(cur, 0, True)
    cur, other = run(n_off, n_diag, True, cur, other)
    update(cur, n_off + n_diag - 1, True)


def _diff_attention(proj3, lam, subln_w, lam_init, tb):
    b, s, _ = proj3.shape
    hw = 2 * ATT_DH
    nb = s // tb
    off = [(qi, j) for qi in range(nb) for j in range(qi)]
    items = jnp.asarray(list(zip(*(off + [(qi, qi) for qi in range(nb)]))), jnp.int32)
    kern = functools.partial(_attn_kernel, tb=tb, n_off=len(off), n_diag=nb, out_scale=1.0 - lam_init)
    seq = lambda col: pl.BlockSpec((None, s, hw), lambda bb, h: (bb, 0, col // hw + h))
    return pl.pallas_call(
        kern,
        grid=(b, ATT_HEADS),
        in_specs=[pl.BlockSpec(memory_space=pltpu.SMEM), pl.BlockSpec(memory_space=pltpu.SMEM),
                  seq(COL_Q), seq(COL_K), seq(COL_V), pl.BlockSpec((1, hw), lambda bb, h: (0, 0))],
        out_specs=pl.BlockSpec((None, s, hw), lambda bb, h: (bb, 0, h)),
        out_shape=jax.ShapeDtypeStruct((b, s, ATT_WIDTH), BF16),
        scratch_shapes=[pltpu.VMEM((nb, hw + ATT_ONES, tb), BF16), pltpu.VMEM((nb, 2 * tb, hw), BF16),
                        pltpu.VMEM((nb, 1, 2 * tb), F32), pltpu.VMEM((nb, hw + ATT_ONES, 2 * tb), F32),
                        pltpu.VMEM((tb, 2 * tb), F32), pltpu.VMEM((1, 2 * tb), F32),
                        pltpu.VMEM((tb, 2 * tb), F32), pltpu.VMEM((1, 2 * tb), F32)],
        compiler_params=_params("parallel", "parallel"),
        name="diff_attention",
    )(items, lam.reshape(1).astype(F32), proj3, proj3, proj3, subln_w.reshape(1, hw).astype(F32))


CONV_HALO = 32
CONV_ROWS = 64


def _cconv_kernel(a_ref, g_ref, w_ref, b_ref, lnw_ref, lnb_ref, o_ref, buf_ref, *, ts):
    i = pl.program_id(1)

    @pl.when(i == 0)
    def _():
        buf_ref[0:CONV_HALO, :] = jnp.zeros((CONV_HALO, CONV_CH), F32)

    @pl.when(i > 0)
    def _():
        buf_ref[0:CONV_HALO, :] = buf_ref[ts:ts + CONV_HALO, :]

    buf_ref[CONV_HALO:CONV_HALO + ts, :] = a_ref[...].astype(F32) * _sigmoid(g_ref[...].astype(F32))

    first = CONV_HALO - (CONV_K - 1)
    for c in range(ts // CONV_ROWS):
        base = c * CONV_ROWS
        acc = jnp.zeros((CONV_ROWS, CONV_CH), F32) + b_ref[...]
        for res in range(SUBLANES):
            taps = [k for k in range(CONV_K) if (first + k) % SUBLANES == res]
            rows = CONV_ROWS + (SUBLANES if res else 0)
            part = None
            for k in taps:
                lo = base + first + k - res
                term = w_ref[k:k + 1, :] * buf_ref[lo:lo + rows, :]
                part = term if part is None else part + term
            acc = acc + part[res:res + CONV_ROWS]
        mu = jnp.mean(acc, axis=-1, keepdims=True)
        xc = acc - mu
        var = jnp.mean(xc * xc, axis=-1, keepdims=True)
        y = xc * lax.rsqrt(var + 1e-5) * lnw_ref[...] + lnb_ref[...]
        o_ref[base:base + CONV_ROWS, :] = (y * _sigmoid(y)).astype(o_ref.dtype)


def _conformer_conv(proj3, dw_w, dw_b, ln_w, ln_b, ts):
    b, s, _ = proj3.shape
    c = CONV_CH
    row = lambda a: a.reshape(1, c).astype(F32)
    return pl.pallas_call(
        functools.partial(_cconv_kernel, ts=ts),
        grid=(b, s // ts),
        in_specs=[pl.BlockSpec((None, ts, c), lambda bb, i: (bb, i, COL_U // c)),
                  pl.BlockSpec((None, ts, c), lambda bb, i: (bb, i, COL_U // c + 1)),
                  pl.BlockSpec((CONV_K, c), lambda bb, i: (0, 0)),
                  pl.BlockSpec((1, c), lambda bb, i: (0, 0)),
                  pl.BlockSpec((1, c), lambda bb, i: (0, 0)),
                  pl.BlockSpec((1, c), lambda bb, i: (0, 0))],
        out_specs=pl.BlockSpec((None, ts, c), lambda bb, i: (bb, i, 0)),
        out_shape=jax.ShapeDtypeStruct((b, s, c), BF16),
        scratch_shapes=[pltpu.VMEM((CONV_HALO + ts, c), F32)],
        compiler_params=_params("parallel", "arbitrary"),
        name="conformer_conv",
    )(proj3, proj3, dw_w.astype(F32), row(dw_b), row(ln_w), row(ln_b))


SSD_HALO = 16


def _ssd_kernel(z_ref, xbc_ref, dt_ref, cw_ref, cb_ref, dtb_ref, a_ref, dsk_ref, nw_ref, e_ref,
                o_ref, buf_ref, h_ref, *, chunk):
    L = chunk
    ci = pl.program_id(1)
    rows = range(z_ref.shape[0])

    @pl.when(ci == 0)
    def _():
        for bb in rows:
            buf_ref[bb, 0:SSD_HALO, :] = jnp.zeros((SSD_HALO, SSM_CONV_DIM), BF16)
        h_ref[...] = jnp.zeros(h_ref.shape, F32)

    @pl.when(ci > 0)
    def _():
        for bb in rows:
            buf_ref[bb, 0:SSD_HALO, :] = buf_ref[bb, L:L + SSD_HALO, :]

    for bb in rows:
        _ssd_chunk(z_ref.at[bb], xbc_ref.at[bb], dt_ref.at[bb], cw_ref, cb_ref, dtb_ref, a_ref, dsk_ref, nw_ref,
                   e_ref, o_ref.at[bb], buf_ref.at[bb], h_ref.at[bb], L)


def _ssd_chunk(z_ref, xbc_ref, dt_ref, cw_ref, cb_ref, dtb_ref, a_ref, dsk_ref, nw_ref, e_ref,
               o_ref, buf_ref, h_ref, L):
    gw = SSM_D_INNER // SSM_GROUPS
    hpg = SSM_HEADS // SSM_GROUPS

    buf_ref[SSD_HALO:SSD_HALO + L, :] = xbc_ref[...]
    xraw = buf_ref[...]
    r = lax.broadcasted_iota(jnp.int32, (L, SSD_HALO + L), 0)
    c = lax.broadcasted_iota(jnp.int32, (L, SSD_HALO + L), 1)
    xc = jnp.zeros((L, SSM_CONV_DIM), F32) + cb_ref[...]
    for k in range(SSM_CONV_K):
        shift = jnp.where(c == r + (SSD_HALO - (SSM_CONV_K - 1) + k), 1.0, 0.0).astype(BF16)
        xc = xc + cw_ref[k:k + 1, :] * jnp.dot(shift, xraw, preferred_element_type=F32)
    xc = xc * _sigmoid(xc)
    xs = xc[:, :SSM_D_INNER]
    bm = xc[:, SSM_D_INNER:SSM_D_INNER + SSM_GROUPS * SSM_STATE]
    cm = xc[:, SSM_D_INNER + SSM_GROUPS * SSM_STATE:]

    dtr = dt_ref[...] + dtb_ref[...]
    dt = jnp.maximum(dtr, 0.0) + jnp.log(1.0 + jnp.exp(-jnp.abs(dtr)))
    da = dt * a_ref[...]
    row = lax.broadcasted_iota(jnp.int32, (L, L), 0)
    col = lax.broadcasted_iota(jnp.int32, (L, L), 1)
    causal = col <= row
    a_cum = sum(jnp.dot(jnp.where(causal, 1.0, 0.0).astype(BF16), part, preferred_element_type=F32)
                for part in _split3(da))
    a_cum_t = a_cum.T

    expand = e_ref[...]
    per_head = jnp.concatenate([dt, jnp.exp(a_cum), jnp.exp(a_cum[L - 1:L, :] - a_cum)], axis=0)
    full = sum(jnp.dot(part, expand, preferred_element_type=F32) for part in _split3(per_head))
    dt_full, ea_full, dend_full = full[:L], full[L:2 * L], full[2 * L:]
    cdec_full = ea_full[L - 1:L, :]

    xdt = xs * dt_full
    xdt_b = xdt.astype(BF16)
    xw_b = (xdt * dend_full).astype(BF16)

    y_parts = []
    for g in range(SSM_GROUPS):
        cg = cm[:, g * SSM_STATE:(g + 1) * SSM_STATE].astype(BF16)
        bg_t = bm[:, g * SSM_STATE:(g + 1) * SSM_STATE].T.astype(BF16)
        cbm = jnp.dot(cg, bg_t, preferred_element_type=F32)
        h_in = h_ref[:, g * gw:(g + 1) * gw]
        y_off = jnp.dot(cg, h_in.astype(BF16), preferred_element_type=F32)
        states = jnp.dot(bg_t, xw_b[:, g * gw:(g + 1) * gw], preferred_element_type=F32)
        h_ref[:, g * gw:(g + 1) * gw] = h_in * cdec_full[:, g * gw:(g + 1) * gw] + states
        yd = []
        for r in range(hpg):
            hh = g * hpg + r
            seg = a_cum[:, hh:hh + 1] - a_cum_t[hh:hh + 1, :]
            dec = jnp.exp(jnp.where(causal, seg, -jnp.inf))
            yd.append(jnp.dot((cbm * dec).astype(BF16), xdt_b[:, hh * SSM_HEADDIM:(hh + 1) * SSM_HEADDIM],
                              preferred_element_type=F32))
        y_g = (jnp.concatenate(yd, axis=-1) + y_off * ea_full[:, g * gw:(g + 1) * gw]
               + xs[:, g * gw:(g + 1) * gw] * dsk_ref[:, g * gw:(g + 1) * gw])
        zg = z_ref[:, g * gw:(g + 1) * gw].astype(F32)
        y_parts.append(_rms(y_g * (zg * _sigmoid(zg)), 1e-5))
    o_ref[...] = (jnp.concatenate(y_parts, axis=-1) * nw_ref[...]).astype(o_ref.dtype)


def _mamba2_ssd(proj3, dt3, conv_w, conv_b, dt_bias, a_log, d_skip, norm_w):
    b, s, _ = proj3.shape
    L = SSM_CHUNK
    di, cd = SSM_D_INNER, SSM_CONV_DIM
    pad_h = lambda a: jnp.pad(a.astype(F32), (0, LANES - SSM_HEADS)).reshape(1, LANES)
    head_of_channel = jnp.arange(di, dtype=jnp.int32) // SSM_HEADDIM
    expand = (jnp.arange(LANES, dtype=jnp.int32)[:, None] == head_of_channel[None, :]).astype(BF16)
    const = lambda shape: pl.BlockSpec(shape, lambda bb, c: (0, 0))
    nb = 2 if b % 2 == 0 else 1
    return pl.pallas_call(
        functools.partial(_ssd_kernel, chunk=L),
        grid=(b // nb, s // L),
        in_specs=[pl.BlockSpec((nb, L, di), lambda bb, c: (bb, c, COL_Z // di)),
                  pl.BlockSpec((nb, L, cd), lambda bb, c: (bb, c, COL_XBC // cd)),
                  pl.BlockSpec((nb, L, LANES), lambda bb, c: (bb, c, 0)),
                  const((SSM_CONV_K, cd)), const((1, cd)), const((1, LANES)), const((1, LANES)),
                  const((1, di)), const((1, di)), const((LANES, di))],
        out_specs=pl.BlockSpec((nb, L, di), lambda bb, c: (bb, c, 0)),
        out_shape=jax.ShapeDtypeStruct((b, s, di), BF16),
        scratch_shapes=[pltpu.VMEM((nb, SSD_HALO + L, cd), BF16), pltpu.VMEM((nb, SSM_STATE, di), F32)],
        compiler_params=_params("parallel", "arbitrary"),
        name="mamba2_ssd",
    )(proj3, proj3, dt3, conv_w.astype(F32), conv_b.reshape(1, cd).astype(F32), pad_h(dt_bias),
      pad_h(-jnp.exp(a_log.astype(F32))), jnp.repeat(d_skip.astype(F32), SSM_HEADDIM).reshape(1, di),
      norm_w.reshape(1, di).astype(F32), expand)


def _merge_kernel(x_ref, ya_ref, yc_ref, ys_ref, ga_ref, gc_ref, gs_ref, wa_ref, wc_ref, ws_ref, wm_ref, o_ref):
    def branch(y_ref, g_ref, w_ref):
        return _sigmoid(g_ref[...].astype(F32)) * jnp.dot(y_ref[...], w_ref[...], preferred_element_type=F32)

    merged = branch(ya_ref, ga_ref, wa_ref) + branch(yc_ref, gc_ref, wc_ref) + branch(ys_ref, gs_ref, ws_ref)
    o_ref[...] = x_ref[...] + jnp.dot(merged.astype(BF16), wm_ref[...], preferred_element_type=F32)


def _merge(x, ya, yc, ys, proj, wa, wc, ws, wm, tm):
    t, d = x.shape
    w = ya.shape[1]
    rows = lambda width, cb: pl.BlockSpec((tm, width), lambda i: (i, cb))
    const = lambda shape: pl.BlockSpec(shape, lambda i: (0, 0))
    return pl.pallas_call(
        _merge_kernel,
        grid=(t // tm,),
        in_specs=[rows(d, 0), rows(w, 0), rows(w, 0), rows(w, 0),
                  rows(d, COL_G // d), rows(d, COL_G // d + 1), rows(d, COL_G // d + 2),
                  const((w, d)), const((w, d)), const((w, d)), const((d, d))],
        out_specs=rows(d, 0),
        out_shape=jax.ShapeDtypeStruct((t, d), F32),
        compiler_params=_params("parallel"),
        name="gated_merge",
    )(x, ya, yc, ys, proj, proj, proj, wa, wc, ws, wm)


def _ffn_kernel(x_ref, g_ref, wg_ref, wu_ref, wd_ref, o_ref):
    x = x_ref[...]
    h = (_rms(x, NORM_EPS) * g_ref[...]).astype(BF16)
    a = jnp.dot(h, wg_ref[...], preferred_element_type=F32)
    u = jnp.dot(h, wu_ref[...], preferred_element_type=F32)
    o_ref[...] = x + jnp.dot((a * _sigmoid(a) * u).astype(BF16), wd_ref[...], preferred_element_type=F32)


def _ffn_dense(x, g, wg, wu, wd, tm):
    t, d = x.shape
    f = wg.shape[1]
    resident = lambda shape: pl.BlockSpec(shape, lambda i: (0, 0), pipeline_mode=pl.Buffered(1))
    return pl.pallas_call(
        _ffn_kernel,
        grid=(t // tm,),
        in_specs=[pl.BlockSpec((tm, d), lambda i: (i, 0)),
                  pl.BlockSpec((1, d), lambda i: (0, 0)),
                  resident((d, f)), resident((d, f)), resident((f, d))],
        out_specs=pl.BlockSpec((tm, d), lambda i: (i, 0)),
        out_shape=jax.ShapeDtypeStruct((t, d), F32),
        compiler_params=_params("parallel"),
        name="ffn_dense",
    )(x, g.reshape(1, d), wg, wu, wd)


def _router_kernel(x_ref, g_ref, wr_ref, br_ref, h_ref, idx_ref, wt_ref, rank_ref, cnt_ref, tri_ref, base_ref):
    i = pl.program_id(0)
    tm = x_ref.shape[0]

    @pl.when(i == 0)
    def _():
        r = lax.broadcasted_iota(jnp.int32, (tm, tm), 0)
        c = lax.broadcasted_iota(jnp.int32, (tm, tm), 1)
        tri_ref[...] = jnp.where(c < r, 1.0, 0.0).astype(BF16)
        base_ref[...] = jnp.zeros(base_ref.shape, F32)

    h = _rms(x_ref[...], NORM_EPS) * g_ref[...]
    h_ref[...] = _pack_halves(h)
    logits = jnp.dot(h, wr_ref[...], precision=HIGHEST, preferred_element_type=F32) + br_ref[...]
    lane = lax.broadcasted_iota(jnp.int32, logits.shape, 1)
    logits = jnp.where(lane < N_EXPERTS, logits, -jnp.inf)
    m1 = jnp.max(logits, axis=-1, keepdims=True)
    i1 = jnp.min(jnp.where(logits == m1, lane, LANES), axis=-1, keepdims=True)
    rest = jnp.where(lane == i1, -jnp.inf, logits)
    m2 = jnp.max(rest, axis=-1, keepdims=True)
    i2 = jnp.min(jnp.where(rest == m2, lane, LANES), axis=-1, keepdims=True)
    e = jnp.exp(m2 - m1)
    w1 = 1.0 / (1.0 + e)
    idx_ref[...] = jnp.where(lane == 0, i1, i2)[:, :TOP_K]
    wt_ref[...] = jnp.where(lane == 0, w1, e * w1)[:, :TOP_K]

    chosen = jnp.logical_or(lane == i1, lane == i2)
    before = jnp.dot(tri_ref[...], jnp.where(chosen, 1.0, 0.0).astype(BF16), preferred_element_type=F32)
    before = before + base_ref[...]
    r1 = jnp.sum(jnp.where(lane == i1, before, 0.0), axis=-1, keepdims=True)
    r2 = jnp.sum(jnp.where(lane == i2, before, 0.0), axis=-1, keepdims=True)
    rank_ref[...] = jnp.where(lane == 0, r1, r2)[:, :TOP_K].astype(jnp.int32)
    base_ref[...] += jnp.sum(jnp.where(chosen, 1.0, 0.0), axis=0, keepdims=True)
    cnt_ref[...] = base_ref[...].astype(jnp.int32)


def _router(x, g, w_router, b_router, tm):
    t, d = x.shape
    wr = jnp.pad(w_router.astype(F32), ((0, 0), (0, LANES - N_EXPERTS)))
    br = jnp.pad(b_router.astype(F32), (0, LANES - N_EXPERTS)).reshape(1, LANES)
    pair = pl.BlockSpec((tm, TOP_K), lambda i: (i, 0))
    return pl.pallas_call(
        _router_kernel,
        grid=(t // tm,),
        in_specs=[pl.BlockSpec((tm, d), lambda i: (i, 0)),
                  pl.BlockSpec((1, d), lambda i: (0, 0)),
                  pl.BlockSpec((d, LANES), lambda i: (0, 0)),
                  pl.BlockSpec((1, LANES), lambda i: (0, 0))],
        out_specs=[pl.BlockSpec((tm, d // 2), lambda i: (i, 0)), pair, pair, pair,
                   pl.BlockSpec((1, LANES), lambda i: (0, 0))],
        out_shape=[jax.ShapeDtypeStruct((t, d // 2), jnp.uint32),
                   jax.ShapeDtypeStruct((t, TOP_K), jnp.int32),
                   jax.ShapeDtypeStruct((t, TOP_K), F32),
                   jax.ShapeDtypeStruct((t, TOP_K), jnp.int32),
                   jax.ShapeDtypeStruct((1, LANES), jnp.int32)],
        scratch_shapes=[pltpu.VMEM((tm, tm), BF16), pltpu.VMEM((1, LANES), F32)],
        compiler_params=_params("arbitrary"),
        name="moe_router",
    )(x, g.reshape(1, d), wr, br)


def _expert_kernel(be_ref, nv_ref, x_ref, wg_ref, wu_ref, wd_ref, o_ref, xb_ref, acc_ref):
    i = pl.program_id(0)
    j = pl.program_id(1)
    last = pl.num_programs(1) - 1
    used = nv_ref[i] > 0

    @pl.when(jnp.logical_and(used, j == 0))
    def _():
        x = _unpack_halves(x_ref[...])
        row = lax.broadcasted_iota(jnp.int32, x.shape, 0)
        xb_ref[...] = jnp.where(row < nv_ref[i], x, 0.0).astype(BF16)
        acc_ref[...] = jnp.zeros(acc_ref.shape, F32)

    @pl.when(used)
    def _():
        x = xb_ref[...]
        a = jnp.dot(x, wg_ref[...], preferred_element_type=F32)
        u = jnp.dot(x, wu_ref[...], preferred_element_type=F32)
        acc_ref[...] += jnp.dot((a * _sigmoid(a) * u).astype(BF16), wd_ref[...], preferred_element_type=F32)

    @pl.when(jnp.logical_and(used, j == last))
    def _():
        o_ref[...] = _pack_halves(acc_ref[...])

    @pl.when(jnp.logical_and(jnp.logical_not(used), j == last))
    def _():
        o_ref[...] = jnp.zeros(o_ref.shape, o_ref.dtype)


def _expert_ffn(xs, block_exp, n_valid, wg, wu, wd, bm, tf):
    n_rows, half = xs.shape
    d = 2 * half
    f = wg.shape[2]
    nj = f // tf

    def jj(i, j, nv):
        return jnp.where(nv[i] > 0, j, nj - 1)

    grid_spec = pltpu.PrefetchScalarGridSpec(
        num_scalar_prefetch=2,
        grid=(n_rows // bm, nj),
        in_specs=[pl.BlockSpec((bm, half), lambda i, j, be, nu: (i, 0)),
                  pl.BlockSpec((None, d, tf), lambda i, j, be, nu: (be[i], 0, jj(i, j, nu))),
                  pl.BlockSpec((None, d, tf), lambda i, j, be, nu: (be[i], 0, jj(i, j, nu))),
                  pl.BlockSpec((None, tf, d), lambda i, j, be, nu: (be[i], jj(i, j, nu), 0))],
        out_specs=pl.BlockSpec((bm, half), lambda i, j, be, nu: (i, 0)),
        scratch_shapes=[pltpu.VMEM((bm, d), BF16), pltpu.VMEM((bm, d), F32)],
    )
    return pl.pallas_call(
        _expert_kernel,
        grid_spec=grid_spec,
        out_shape=jax.ShapeDtypeStruct((n_rows, half), jnp.uint32),
        compiler_params=_params("parallel", "arbitrary"),
        name="moe_experts",
    )(block_exp, n_valid, xs, wg, wu, wd)


def _route(top_idx, rank, counts, bm):
    t = top_idx.shape[0]
    padded = (counts + bm - 1) // bm * bm
    end_padded = jnp.cumsum(padded)
    start_padded = end_padded - padded
    experts = jnp.arange(N_EXPERTS, dtype=jnp.int32)
    dest = rank + jnp.sum(jnp.where(top_idx[..., None] == experts, start_padded, 0), axis=-1)
    n_blocks = t * TOP_K // bm + N_EXPERTS
    block_start = jnp.arange(n_blocks, dtype=jnp.int32) * bm
    block_exp = jnp.minimum(jnp.sum(block_start[:, None] >= end_padded[None, :], axis=1), N_EXPERTS - 1)
    n_valid = jnp.clip((start_padded + counts)[block_exp] - block_start, 0, bm)
    return dest, block_exp.astype(jnp.int32), n_valid.astype(jnp.int32)


def _combine_kernel(x_ref, a_ref, b_ref, w_ref, g_ref, o_ref, *, final_norm):
    w = w_ref[...]
    y = x_ref[...] + w[:, 0:1] * _unpack_halves(a_ref[...]) + w[:, 1:2] * _unpack_halves(b_ref[...])
    o_ref[...] = _rms(y, NORM_EPS) * g_ref[...] if final_norm else y


def _combine(x, ab, w, g, tm):
    t, d = x.shape
    spec = pl.BlockSpec((tm, d), lambda i: (i, 0))
    final_norm = g is not None
    gain = g.reshape(1, d) if final_norm else jnp.ones((1, d), F32)
    return pl.pallas_call(
        functools.partial(_combine_kernel, final_norm=final_norm), grid=(t // tm,),
        in_specs=[spec, pl.BlockSpec((tm, d // 2), lambda i: (i, 0)),
                  pl.BlockSpec((tm, d // 2), lambda i: (i + t // tm, 0)),
                  pl.BlockSpec((tm, TOP_K), lambda i: (i, 0)),
                  pl.BlockSpec((1, d), lambda i: (0, 0))],
        out_specs=spec,
        out_shape=jax.ShapeDtypeStruct((t, d), F32), compiler_params=_params("parallel"),
        name="moe_combine",
    )(x, ab, ab, w, gain)


def _moe(x, g, w_router, b_router, wg, wu, wd, final_g, tm, bm, tf):
    h, top_idx, top_w, rank, counts = _router(x, g, w_router, b_router, tm)
    dest, block_exp, n_valid = _route(top_idx, rank, counts[0, :N_EXPERTS], bm)
    dest = dest.T
    xs = _scatter_rows(h, dest, block_exp.shape[0] * bm)
    ys = _expert_ffn(xs, block_exp, n_valid, wg, wu, wd, bm, tf)
    picked = _gather_rows(ys, dest.reshape(-1))
    return _combine(x, picked, top_w, final_g, tm)


def _cast_kernel(w_ref, o_ref):
    o_ref[...] = w_ref[...].astype(o_ref.dtype)


def _to_bf16(w, axis, tile):
    e, r, c = w.shape
    block = (None, tile, c) if axis == 0 else (None, r, tile)
    index = (lambda k, j: (k, j, 0)) if axis == 0 else (lambda k, j: (k, 0, j))
    return pl.pallas_call(
        _cast_kernel, grid=(e, (r, c)[axis] // tile),
        in_specs=[pl.BlockSpec(block, index)], out_specs=pl.BlockSpec(block, index),
        out_shape=jax.ShapeDtypeStruct(w.shape, BF16), compiler_params=_params("parallel", "parallel"),
        name="expert_weight_cast",
    )(w)


def _final_norm_kernel(x_ref, g_ref, o_ref):
    o_ref[...] = _rms(x_ref[...], NORM_EPS) * g_ref[...]


def _final_norm(x, g, tm):
    t, d = x.shape
    return pl.pallas_call(
        _final_norm_kernel, grid=(t // tm,),
        in_specs=[pl.BlockSpec((tm, d), lambda i: (i, 0)), pl.BlockSpec((1, d), lambda i: (0, 0))],
        out_specs=pl.BlockSpec((tm, d), lambda i: (i, 0)),
        out_shape=jax.ShapeDtypeStruct((t, d), F32), compiler_params=_params("parallel"),
        name="final_norm",
    )(x, g.reshape(1, d))


def _tiles(t, s):
    pick = lambda n, pref: pref if n % pref == 0 else n
    return dict(tm_proj=pick(t, 1024), tn_proj=PROJ_COLS // 4, tb_attn=pick(s, 512), ts_conv=pick(s, 512),
                tm_merge=pick(t, 512), tm_ffn=pick(t, 512), tm_row=pick(t, 1024), bm_moe=pick(t, 512))


def kernel(x, ln_mix_w, w_in, attn_lq1, attn_lk1, attn_lq2, attn_lk2, attn_subln_w, attn_w_o, conv_dw_w, conv_dw_b, conv_ln_w, conv_ln_b, conv_w_pw, ssm_conv_w, ssm_conv_b, ssm_dt_bias, ssm_A_log, ssm_D, ssm_norm_w, ssm_w_out, w_mix_out, ln_ffn_w, ffn_w_gate, ffn_w_up, ffn_w_down, moe_w_router, moe_b_router, moe_w_gate, moe_w_up, moe_w_down, ln_final_w):
    b, s, d = x.shape
    t = b * s
    depth = w_in.shape[0]
    tl = _tiles(t, s)
    xf = x.reshape(t, d)
    for l in range(depth):
        w_main = jnp.concatenate([w_in[l][:, :COL_DT], w_in[l][:, COL_DT + SSM_HEADS:]], axis=1).astype(BF16)
        w_dt = jnp.pad(w_in[l][:, COL_DT:COL_DT + SSM_HEADS], ((0, 0), (0, LANES - SSM_HEADS))).astype(BF16)
        proj, dt = _norm_matmul(xf, ln_mix_w[l], w_main, w_dt, tl["tm_proj"], tl["tn_proj"])
        proj3 = proj.reshape(b, s, PROJ_COLS)

        lam_init = 0.8 - 0.6 * math.exp(-0.3 * l)
        lam = (jnp.exp(jnp.sum(attn_lq1[l].astype(F32) * attn_lk1[l].astype(F32)))
               - jnp.exp(jnp.sum(attn_lq2[l].astype(F32) * attn_lk2[l].astype(F32))) + lam_init)
        y_att = _diff_attention(proj3, lam, attn_subln_w[l], lam_init, tl["tb_attn"])
        y_conv = _conformer_conv(proj3, conv_dw_w[l], conv_dw_b[l], conv_ln_w[l], conv_ln_b[l], tl["ts_conv"])
        y_ssm = _mamba2_ssd(proj3, dt.reshape(b, s, LANES), ssm_conv_w[l], ssm_conv_b[l], ssm_dt_bias[l],
                            ssm_A_log[l], ssm_D[l], ssm_norm_w[l])
        xf = _merge(xf, y_att.reshape(t, ATT_WIDTH), y_conv.reshape(t, CONV_CH), y_ssm.reshape(t, SSM_D_INNER),
                    proj, attn_w_o[l].astype(BF16), conv_w_pw[l].astype(BF16), ssm_w_out[l].astype(BF16),
                    w_mix_out[l].astype(BF16), tl["tm_merge"])

        i = l // 2
        if l % 2 == 0:
            xf = _ffn_dense(xf, ln_ffn_w[l], ffn_w_gate[i].astype(BF16), ffn_w_up[i].astype(BF16),
                            ffn_w_down[i].astype(BF16), tl["tm_ffn"])
        else:
            f = moe_w_gate.shape[3]
            tf = f // 2 if (f // 2) % (2 * LANES) == 0 else f
            final_g = ln_final_w if l == depth - 1 else None
            xf = _moe(xf, ln_ffn_w[l], moe_w_router[i], moe_b_router[i], _to_bf16(moe_w_gate[i], 1, tf),
                      _to_bf16(moe_w_up[i], 1, tf), _to_bf16(moe_w_down[i], 0, tf), final_g,
                      tl["tm_row"], tl["bm_moe"], tf)
    if depth % 2 == 1:
        xf = _final_norm(xf, ln_final_w, tl["tm_row"])
    return xf.reshape(b, s, d)
```

```python
import functools
import math

import jax
import jax.numpy as jnp
from jax import lax
from jax.experimental import pallas as pl
from jax.experimental.pallas import tpu as pltpu
from jax.experimental.pallas import tpu_sc as plsc

F32 = jnp.float32
BF16 = jnp.bfloat16
HIGHEST = lax.Precision.HIGHEST

D_MODEL = 1024
ATT_HEADS = 4
ATT_DH = 64
ATT_WIDTH = ATT_HEADS * 2 * ATT_DH
CONV_CH = 512
CONV_K = 31
SSM_D_INNER = 512
SSM_HEADDIM = 64
SSM_HEADS = SSM_D_INNER // SSM_HEADDIM
SSM_GROUPS = 2
SSM_STATE = 128
SSM_CONV_K = 4
SSM_CONV_DIM = SSM_D_INNER + 2 * SSM_GROUPS * SSM_STATE
SSM_CHUNK = 128
N_EXPERTS = 8
TOP_K = 2
NORM_EPS = 1e-6

LANES = 128
SUBLANES = 8
VMEM_LIMIT_BYTES = 48 * 1024 * 1024

COL_Q = 0
COL_K = COL_Q + ATT_WIDTH
COL_V = COL_K + ATT_WIDTH
COL_U = COL_V + ATT_WIDTH
COL_Z = COL_U + 2 * CONV_CH
COL_XBC = COL_Z + SSM_D_INNER
COL_DT = COL_XBC + SSM_CONV_DIM
COL_G = COL_XBC + SSM_CONV_DIM
PROJ_COLS = COL_G + 3 * D_MODEL


def _params(*semantics):
    return pltpu.CompilerParams(dimension_semantics=semantics, vmem_limit_bytes=VMEM_LIMIT_BYTES)


def _sigmoid(x):
    return 1.0 / (1.0 + jnp.exp(-x))


def _rms(x, eps):
    return x * lax.rsqrt(jnp.mean(x * x, axis=-1, keepdims=True) + eps)


def _split3(x):
    hi = x.astype(BF16)
    rest = x - hi.astype(F32)
    mid = rest.astype(BF16)
    return hi, mid, (rest - mid.astype(F32)).astype(BF16)


def _pack_halves(x):
    n = x.shape[1] // 2
    bits = lambda v: lax.bitcast_convert_type(v.astype(BF16).astype(F32), jnp.uint32)
    return (bits(x[:, :n]) >> 16) | bits(x[:, n:])


def _unpack_halves(w):
    lo = lax.bitcast_convert_type(w << 16, F32)
    hi = lax.bitcast_convert_type(w & jnp.uint32(0xFFFF0000), F32)
    return jnp.concatenate([lo, hi], axis=1)


SC_GATHER_ROWS = 64


def _gather_rows(table, idx):
    n_idx = idx.shape[0]
    width = table.shape[1]
    info = plsc.get_sparse_core_info()
    n_workers = info.num_cores * info.num_subcores
    per_worker = n_idx // n_workers
    n_chunks = per_worker // SC_GATHER_ROWS
    assert n_idx % n_workers == 0 and per_worker % SC_GATHER_ROWS == 0 and n_chunks % 2 == 0, (n_idx, n_workers)
    mesh = plsc.VectorSubcoreMesh(core_axis_name="c", subcore_axis_name="s")
    idx_buf = pltpu.VMEM((SC_GATHER_ROWS,), jnp.int32)
    row_buf = pltpu.VMEM((SC_GATHER_ROWS, width), table.dtype)

    @functools.partial(
        pl.kernel, mesh=mesh,
        out_type=jax.ShapeDtypeStruct((n_idx, width), table.dtype),
        scratch_types=[idx_buf, idx_buf, row_buf, row_buf, pltpu.SemaphoreType.DMA, pltpu.SemaphoreType.DMA],
    )
    def gather(table_hbm, idx_hbm, out_hbm, idx0, idx1, rows0, rows1, sem0, sem1):
        worker = lax.axis_index("s") * info.num_cores + lax.axis_index("c")
        slots = ((idx0, rows0, sem0), (idx1, rows1, sem1))

        def offset(c):
            return pl.multiple_of(worker * per_worker + c * SC_GATHER_ROWS, SUBLANES)

        def stream(slot):
            idx_v, rows_v, sem = slot
            return pltpu.make_async_copy(table_hbm.at[idx_v], rows_v, sem)

        def start(c, slot):
            pltpu.sync_copy(idx_hbm.at[pl.ds(offset(c), SC_GATHER_ROWS)], slot[0])
            stream(slot).start()

        start(0, slots[0])

        @pl.loop(0, n_chunks, step=2)
        def _(c0):
            for b in range(2):
                c = c0 + b
                stream(slots[b]).wait()

                @pl.when(c + 1 < n_chunks)
                def _():
                    start(c + 1, slots[1 - b])

                pltpu.sync_copy(slots[b][1], out_hbm.at[pl.ds(offset(c), SC_GATHER_ROWS)])

    return gather(table, idx)


def _scatter_rows(rows, idx, n_out):
    n_rows, width = rows.shape
    fan = idx.shape[0]
    info = plsc.get_sparse_core_info()
    n_workers = info.num_cores * info.num_subcores
    per_worker = n_rows // n_workers
    assert n_rows % n_workers == 0 and per_worker % SC_GATHER_ROWS == 0, (n_rows, n_workers)
    mesh = plsc.VectorSubcoreMesh(core_axis_name="c", subcore_axis_name="s")
    idx_buf = pltpu.VMEM((SC_GATHER_ROWS,), jnp.int32)

    @functools.partial(
        pl.kernel, mesh=mesh,
        out_type=jax.ShapeDtypeStruct((n_out, width), rows.dtype),
        scratch_types=[idx_buf] * fan + [pltpu.VMEM((SC_GATHER_ROWS, width), rows.dtype)],
    )
    def scatter(rows_hbm, idx_hbm, out_hbm, *scratch):
        idx_vs, rows_v = scratch[:fan], scratch[fan]
        worker = lax.axis_index("s") * info.num_cores + lax.axis_index("c")

        @pl.loop(0, per_worker // SC_GATHER_ROWS)
        def _(c):
            off = pl.multiple_of(worker * per_worker + c * SC_GATHER_ROWS, SUBLANES)
            pltpu.sync_copy(rows_hbm.at[pl.ds(off, SC_GATHER_ROWS)], rows_v)
            for k in range(fan):
                pltpu.sync_copy(idx_hbm.at[k, pl.ds(off, SC_GATHER_ROWS)], idx_vs[k])
                pltpu.sync_copy(rows_v, out_hbm.at[idx_vs[k]])

    return scatter(rows, idx)


def _norm_matmul_kernel(x_ref, g_ref, w_ref, ws_ref, o_ref, os_ref, h_ref):
    @pl.when(pl.program_id(1) == 0)
    def _():
        h = (_rms(x_ref[...], NORM_EPS) * g_ref[...]).astype(h_ref.dtype)
        h_ref[...] = h
        os_ref[...] = jnp.dot(h, ws_ref[...], preferred_element_type=F32)

    o_ref[...] = jnp.dot(h_ref[...], w_ref[...], preferred_element_type=F32).astype(o_ref.dtype)


def _norm_matmul(x, g, w, w_side, tm, tn):
    t, d = x.shape
    n = w.shape[1]
    ns = w_side.shape[1]
    return pl.pallas_call(
        _norm_matmul_kernel,
        grid=(t // tm, n // tn),
        in_specs=[pl.BlockSpec((tm, d), lambda i, j: (i, 0)),
                  pl.BlockSpec((1, d), lambda i, j: (0, 0)),
                  pl.BlockSpec((d, tn), lambda i, j: (0, j)),
                  pl.BlockSpec((d, ns), lambda i, j: (0, 0))],
        out_specs=[pl.BlockSpec((tm, tn), lambda i, j: (i, j)),
                   pl.BlockSpec((tm, ns), lambda i, j: (i, 0))],
        out_shape=[jax.ShapeDtypeStruct((t, n), BF16), jax.ShapeDtypeStruct((t, ns), F32)],
        scratch_shapes=[pltpu.VMEM((tm, d), BF16)],
        compiler_params=_params("parallel", "arbitrary"),
        name="norm_matmul",
    )(x, g.reshape(1, d), w, w_side)


ATT_ONES = 16
LOG2E = 1.4426950408889634


def _attn_kernel(items_ref, lam_ref, q_ref, k_ref, v_ref, sw_ref, o_ref,
                 vt_ref, q2_ref, m_ref, acc_ref, sa_ref, ca_ref, sb_ref, cb_ref, *, tb, n_off, n_diag, out_scale):
    hw = 2 * ATT_DH
    nb = vt_ref.shape[0]

    lane = lax.broadcasted_iota(jnp.int32, (tb, hw), 1)
    for c in range(nb):
        rows = slice(c * tb, (c + 1) * tb)
        vt_ref[c, 0:hw, :] = v_ref[rows, :].astype(F32).T.astype(BF16)
        vt_ref[c, hw:hw + ATT_ONES, :] = jnp.ones((ATT_ONES, tb), BF16)
        q = (q_ref[rows, :].astype(F32) * (ATT_DH ** -0.5 * LOG2E)).astype(BF16)
        zero = jnp.zeros_like(q)
        q2_ref[c, 0:tb, :] = jnp.where(lane < ATT_DH, q, zero)
        q2_ref[c, tb:2 * tb, :] = jnp.where(lane >= ATT_DH, q, zero)
        m_ref[c] = jnp.full(m_ref.shape[1:], -jnp.inf, F32)
        acc_ref[c] = jnp.zeros(acc_ref.shape[1:], F32)

    def fill(buf, t, masked):
        s_ref, cmax_ref = buf
        qi, j = items_ref[0, t], items_ref[1, t]
        k = k_ref[pl.ds(pl.multiple_of(j * tb, tb), tb), :]
        st = lax.dot_general(k, q2_ref[qi], (((1,), (1,)), ((), ())), preferred_element_type=F32)
        if masked:
            r = lax.broadcasted_iota(jnp.int32, st.shape, 0)
            c = lax.broadcasted_iota(jnp.int32, st.shape, 1)
            st = jnp.where(r <= jnp.where(c >= tb, c - tb, c), st, -jnp.inf)
        s_ref[...] = st
        cmax_ref[...] = jnp.max(st, axis=0, keepdims=True)

    def update(buf, t, last):
        s_ref, cmax_ref = buf
        qi, j = items_ref[0, t], items_ref[1, t]
        m_old = m_ref[qi]
        m_new = jnp.maximum(m_old, cmax_ref[...])
        p = jnp.exp2(s_ref[...] - m_new).astype(BF16)
        acc = jnp.exp2(m_old - m_new) * acc_ref[qi] + jnp.dot(vt_ref[j], p, preferred_element_type=F32)
        if not last:
            acc_ref[qi] = acc
            m_ref[qi] = m_new
        else:
            o_t = acc[:hw] * (1.0 / acc[hw:hw + 1])
            o = (o_t[:, :tb] - lam_ref[0] * o_t[:, tb:]).T
            o_ref[pl.ds(pl.multiple_of(qi * tb, tb), tb), :] = (
                _rms(o, 1e-5) * (sw_ref[...] * out_scale)).astype(o_ref.dtype)

    def run(first, count, diag, cur, other):
        trips = (count - 1) // 2

        def body(i, carry):
            t = first + 2 * i
            fill(other, t + 1, diag)
            update(cur, t, diag)
            fill(cur, t + 2, diag)
            update(other, t + 1, diag)
            return carry

        lax.fori_loop(0, trips, body, 0)
        if count - 1 > 2 * trips:
            t = first + 2 * trips
            fill(other, t + 1, diag)
            update(cur, t, diag)
            cur, other = other, cur
        return cur, other

    cur, other = (sa_ref, ca_ref), (sb_ref, cb_ref)
    if n_off:
        fill(cur, 0, False)
        cur, other = run(0, n_off, False, cur, other)
        fill(other, n_off, True)
        update(cur, n_off - 1, False)
        cur, other = other, cur
    else:
        fill(cur, 0, True)
    cur, other = run(n_off, n_diag, True, cur, other)
    update(cur, n_off + n_diag - 1, True)


def _diff_attention(proj3, lam, subln_w, lam_init, tb):
    b, s, _ = proj3.shape
    hw = 2 * ATT_DH
    nb = s // tb
    off = [(qi, j) for qi in range(nb) for j in range(qi)]
    items = jnp.asarray(list(zip(*(off + [(qi, qi) for qi in range(nb)]))), jnp.int32)
    kern = functools.partial(_attn_kernel, tb=tb, n_off=len(off), n_diag=nb, out_scale=1.0 - lam_init)
    seq = lambda col: pl.BlockSpec((None, s, hw), lambda bb, h: (bb, 0, col // hw + h))
    return pl.pallas_call(
        kern,
        grid=(b, ATT_HEADS),
        in_specs=[pl.BlockSpec(memory_space=pltpu.SMEM), pl.BlockSpec(memory_space=pltpu.SMEM),
                  seq(COL_Q), seq(COL_K), seq(COL_V), pl.BlockSpec((1, hw), lambda bb, h: (0, 0))],
        out_specs=pl.BlockSpec((None, s, hw), lambda bb, h: (bb, 0, h)),
        out_shape=jax.ShapeDtypeStruct((b, s, ATT_WIDTH), BF16),
        scratch_shapes=[pltpu.VMEM((nb, hw + ATT_ONES, tb), BF16), pltpu.VMEM((nb, 2 * tb, hw), BF16),
                        pltpu.VMEM((nb, 1, 2 * tb), F32), pltpu.VMEM((nb, hw + ATT_ONES, 2 * tb), F32),
                        pltpu.VMEM((tb, 2 * tb), F32), pltpu.VMEM((1, 2 * tb), F32),
                        pltpu.VMEM((tb, 2 * tb), F32), pltpu.VMEM((1, 2 * tb), F32)],
        compiler_params=_params("parallel", "parallel"),
        name="diff_attention",
    )(items, lam.reshape(1).astype(F32), proj3, proj3, proj3, subln_w.reshape(1, hw).astype(F32))


CONV_HALO = 32
CONV_ROWS = 64


def _cconv_kernel(a_ref, g_ref, w_ref, b_ref, lnw_ref, lnb_ref, o_ref, buf_ref, *, ts):
    i = pl.program_id(1)

    @pl.when(i == 0)
    def _():
        buf_ref[0:CONV_HALO, :] = jnp.zeros((CONV_HALO, CONV_CH), F32)

    @pl.when(i > 0)
    def _():
        buf_ref[0:CONV_HALO, :] = buf_ref[ts:ts + CONV_HALO, :]

    buf_ref[CONV_HALO:CONV_HALO + ts, :] = a_ref[...].astype(F32) * _sigmoid(g_ref[...].astype(F32))

    first = CONV_HALO - (CONV_K - 1)
    for c in range(ts // CONV_ROWS):
        base = c * CONV_ROWS
        acc = jnp.zeros((CONV_ROWS, CONV_CH), F32) + b_ref[...]
        for res in range(SUBLANES):
            taps = [k for k in range(CONV_K) if (first + k) % SUBLANES == res]
            rows = CONV_ROWS + (SUBLANES if res else 0)
            part = None
            for k in taps:
                lo = base + first + k - res
                term = w_ref[k:k + 1, :] * buf_ref[lo:lo + rows, :]
                part = term if part is None else part + term
            acc = acc + part[res:res + CONV_ROWS]
        mu = jnp.mean(acc, axis=-1, keepdims=True)
        xc = acc - mu
        var = jnp.mean(xc * xc, axis=-1, keepdims=True)
        y = xc * lax.rsqrt(var + 1e-5) * lnw_ref[...] + lnb_ref[...]
        o_ref[base:base + CONV_ROWS, :] = (y * _sigmoid(y)).astype(o_ref.dtype)


def _conformer_conv(proj3, dw_w, dw_b, ln_w, ln_b, ts):
    b, s, _ = proj3.shape
    c = CONV_CH
    row = lambda a: a.reshape(1, c).astype(F32)
    return pl.pallas_call(
        functools.partial(_cconv_kernel, ts=ts),
        grid=(b, s // ts),
        in_specs=[pl.BlockSpec((None, ts, c), lambda bb, i: (bb, i, COL_U // c)),
                  pl.BlockSpec((None, ts, c), lambda bb, i: (bb, i, COL_U // c + 1)),
                  pl.BlockSpec((CONV_K, c), lambda bb, i: (0, 0)),
                  pl.BlockSpec((1, c), lambda bb, i: (0, 0)),
                  pl.BlockSpec((1, c), lambda bb, i: (0, 0)),
                  pl.BlockSpec((1, c), lambda bb, i: (0, 0))],
        out_specs=pl.BlockSpec((None, ts, c), lambda bb, i: (bb, i, 0)),
        out_shape=jax.ShapeDtypeStruct((b, s, c), BF16),
        scratch_shapes=[pltpu.VMEM((CONV_HALO + ts, c), F32)],
        compiler_params=_params("parallel", "arbitrary"),
        name="conformer_conv",
    )(proj3, proj3, dw_w.astype(F32), row(dw_b), row(ln_w), row(ln_b))


SSD_HALO = 16


def _ssd_kernel(z_ref, xbc_ref, dt_ref, cw_ref, cb_ref, dtb_ref, a_ref, dsk_ref, nw_ref, e_ref,
                o_ref, buf_ref, h_ref, *, chunk):
    L = chunk
    ci = pl.program_id(1)
    rows = range(z_ref.shape[0])

    @pl.when(ci == 0)
    def _():
        for bb in rows:
            buf_ref[bb, 0:SSD_HALO, :] = jnp.zeros((SSD_HALO, SSM_CONV_DIM), BF16)
        h_ref[...] = jnp.zeros(h_ref.shape, F32)

    @pl.when(ci > 0)
    def _():
        for bb in rows:
            buf_ref[bb, 0:SSD_HALO, :] = buf_ref[bb, L:L + SSD_HALO, :]

    for bb in rows:
        _ssd_chunk(z_ref.at[bb], xbc_ref.at[bb], dt_ref.at[bb], cw_ref, cb_ref, dtb_ref, a_ref, dsk_ref, nw_ref,
                   e_ref, o_ref.at[bb], buf_ref.at[bb], h_ref.at[bb], L)


def _ssd_chunk(z_ref, xbc_ref, dt_ref, cw_ref, cb_ref, dtb_ref, a_ref, dsk_ref, nw_ref, e_ref,
               o_ref, buf_ref, h_ref, L):
    gw = SSM_D_INNER // SSM_GROUPS
    hpg = SSM_HEADS // SSM_GROUPS

    buf_ref[SSD_HALO:SSD_HALO + L, :] = xbc_ref[...]
    xraw = buf_ref[...]
    r = lax.broadcasted_iota(jnp.int32, (L, SSD_HALO + L), 0)
    c = lax.broadcasted_iota(jnp.int32, (L, SSD_HALO + L), 1)
    xc = jnp.zeros((L, SSM_CONV_DIM), F32) + cb_ref[...]
    for k in range(SSM_CONV_K):
        shift = jnp.where(c == r + (SSD_HALO - (SSM_CONV_K - 1) + k), 1.0, 0.0).astype(BF16)
        xc = xc + cw_ref[k:k + 1, :] * jnp.dot(shift, xraw, preferred_element_type=F32)
    xc = xc * _sigmoid(xc)
    xs = xc[:, :SSM_D_INNER]
    bm = xc[:, SSM_D_INNER:SSM_D_INNER + SSM_GROUPS * SSM_STATE]
    cm = xc[:, SSM_D_INNER + SSM_GROUPS * SSM_STATE:]

    dtr = dt_ref[...] + dtb_ref[...]
    dt = jnp.maximum(dtr, 0.0) + jnp.log(1.0 + jnp.exp(-jnp.abs(dtr)))
    da = dt * a_ref[...]
    row = lax.broadcasted_iota(jnp.int32, (L, L), 0)
    col = lax.broadcasted_iota(jnp.int32, (L, L), 1)
    causal = col <= row
    a_cum = sum(jnp.dot(jnp.where(causal, 1.0, 0.0).astype(BF16), part, preferred_element_type=F32)
                for part in _split3(da))
    a_cum_t = a_cum.T

    expand = e_ref[...]
    per_head = jnp.concatenate([dt, jnp.exp(a_cum), jnp.exp(a_cum[L - 1:L, :] - a_cum)], axis=0)
    full = sum(jnp.dot(part, expand, preferred_element_type=F32) for part in _split3(per_head))
    dt_full, ea_full, dend_full = full[:L], full[L:2 * L], full[2 * L:]
    cdec_full = ea_full[L - 1:L, :]

    xdt = xs * dt_full
    xdt_b = xdt.astype(BF16)
    xw_b = (xdt * dend_full).astype(BF16)

    y_parts = []
    for g in range(SSM_GROUPS):
        cg = cm[:, g * SSM_STATE:(g + 1) * SSM_STATE].astype(BF16)
        bg_t = bm[:, g * SSM_STATE:(g + 1) * SSM_STATE].T.astype(BF16)
        cbm = jnp.dot(cg, bg_t, preferred_element_type=F32)
        h_in = h_ref[:, g * gw:(g + 1) * gw]
        y_off = jnp.dot(cg, h_in.astype(BF16), preferred_element_type=F32)
        states = jnp.dot(bg_t, xw_b[:, g * gw:(g + 1) * gw], preferred_element_type=F32)
        h_ref[:, g * gw:(g + 1) * gw] = h_in * cdec_full[:, g * gw:(g + 1) * gw] + states
        yd = []
        for r in range(hpg):
            hh = g * hpg + r
            seg = a_cum[:, hh:hh + 1] - a_cum_t[hh:hh + 1, :]
            dec = jnp.exp(jnp.where(causal, seg, -jnp.inf))
            yd.append(jnp.dot((cbm * dec).astype(BF16), xdt_b[:, hh * SSM_HEADDIM:(hh + 1) * SSM_HEADDIM],
                              preferred_element_type=F32))
        y_g = (jnp.concatenate(yd, axis=-1) + y_off * ea_full[:, g * gw:(g + 1) * gw]
               + xs[:, g * gw:(g + 1) * gw] * dsk_ref[:, g * gw:(g + 1) * gw])
        zg = z_ref[:, g * gw:(g + 1) * gw].astype(F32)
        y_parts.append(_rms(y_g * (zg * _sigmoid(zg)), 1e-5))
    o_ref[...] = (jnp.concatenate(y_parts, axis=-1) * nw_ref[...]).astype(o_ref.dtype)


def _mamba2_ssd(proj3, dt3, conv_w, conv_b, dt_bias, a_log, d_skip, norm_w):
    b, s, _ = proj3.shape
    L = SSM_CHUNK
    di, cd = SSM_D_INNER, SSM_CONV_DIM
    pad_h = lambda a: jnp.pad(a.astype(F32), (0, LANES - SSM_HEADS)).reshape(1, LANES)
    head_of_channel = jnp.arange(di, dtype=jnp.int32) // SSM_HEADDIM
    expand = (jnp.arange(LANES, dtype=jnp.int32)[:, None] == head_of_channel[None, :]).astype(BF16)
    const = lambda shape: pl.BlockSpec(shape, lambda bb, c: (0, 0))
    nb = next(n for n in (4, 2, 1) if b % n == 0)
    return pl.pallas_call(
        functools.partial(_ssd_kernel, chunk=L),
        grid=(b // nb, s // L),
        in_specs=[pl.BlockSpec((nb, L, di), lambda bb, c: (bb, c, COL_Z // di)),
                  pl.BlockSpec((nb, L, cd), lambda bb, c: (bb, c, COL_XBC // cd)),
                  pl.BlockSpec((nb, L, LANES), lambda bb, c: (bb, c, 0)),
                  const((SSM_CONV_K, cd)), const((1, cd)), const((1, LANES)), const((1, LANES)),
                  const((1, di)), const((1, di)), const((LANES, di))],
        out_specs=pl.BlockSpec((nb, L, di), lambda bb, c: (bb, c, 0)),
        out_shape=jax.ShapeDtypeStruct((b, s, di), BF16),
        scratch_shapes=[pltpu.VMEM((nb, SSD_HALO + L, cd), BF16), pltpu.VMEM((nb, SSM_STATE, di), F32)],
        compiler_params=_params("parallel", "arbitrary"),
        name="mamba2_ssd",
    )(proj3, proj3, dt3, conv_w.astype(F32), conv_b.reshape(1, cd).astype(F32), pad_h(dt_bias),
      pad_h(-jnp.exp(a_log.astype(F32))), jnp.repeat(d_skip.astype(F32), SSM_HEADDIM).reshape(1, di),
      norm_w.reshape(1, di).astype(F32), expand)


def _merge_kernel(x_ref, ya_ref, yc_ref, ys_ref, ga_ref, gc_ref, gs_ref, wa_ref, wc_ref, ws_ref, wm_ref, o_ref):
    def branch(y_ref, g_ref, w_ref):
        return _sigmoid(g_ref[...].astype(F32)) * jnp.dot(y_ref[...], w_ref[...], preferred_element_type=F32)

    merged = branch(ya_ref, ga_ref, wa_ref) + branch(yc_ref, gc_ref, wc_ref) + branch(ys_ref, gs_ref, ws_ref)
    o_ref[...] = x_ref[...] + jnp.dot(merged.astype(BF16), wm_ref[...], preferred_element_type=F32)


def _merge(x, ya, yc, ys, proj, wa, wc, ws, wm, tm):
    t, d = x.shape
    w = ya.shape[1]
    rows = lambda width, cb: pl.BlockSpec((tm, width), lambda i: (i, cb))
    const = lambda shape: pl.BlockSpec(shape, lambda i: (0, 0))
    return pl.pallas_call(
        _merge_kernel,
        grid=(t // tm,),
        in_specs=[rows(d, 0), rows(w, 0), rows(w, 0), rows(w, 0),
                  rows(d, COL_G // d), rows(d, COL_G // d + 1), rows(d, COL_G // d + 2),
                  const((w, d)), const((w, d)), const((w, d)), const((d, d))],
        out_specs=rows(d, 0),
        out_shape=jax.ShapeDtypeStruct((t, d), F32),
        compiler_params=_params("parallel"),
        name="gated_merge",
    )(x, ya, yc, ys, proj, proj, proj, wa, wc, ws, wm)


def _ffn_kernel(x_ref, g_ref, wg_ref, wu_ref, wd_ref, o_ref):
    x = x_ref[...]
    h = (_rms(x, NORM_EPS) * g_ref[...]).astype(BF16)
    a = jnp.dot(h, wg_ref[...], preferred_element_type=F32)
    u = jnp.dot(h, wu_ref[...], preferred_element_type=F32)
    o_ref[...] = x + jnp.dot((a * _sigmoid(a) * u).astype(BF16), wd_ref[...], preferred_element_type=F32)


def _ffn_dense(x, g, wg, wu, wd, tm):
    t, d = x.shape
    f = wg.shape[1]
    resident = lambda shape: pl.BlockSpec(shape, lambda i: (0, 0), pipeline_mode=pl.Buffered(1))
    return pl.pallas_call(
        _ffn_kernel,
        grid=(t // tm,),
        in_specs=[pl.BlockSpec((tm, d), lambda i: (i, 0)),
                  pl.BlockSpec((1, d), lambda i: (0, 0)),
                  resident((d, f)), resident((d, f)), resident((f, d))],
        out_specs=pl.BlockSpec((tm, d), lambda i: (i, 0)),
        out_shape=jax.ShapeDtypeStruct((t, d), F32),
        compiler_params=_params("parallel"),
        name="ffn_dense",
    )(x, g.reshape(1, d), wg, wu, wd)


def _router_kernel(x_ref, g_ref, wr_ref, br_ref, h_ref, idx_ref, wt_ref, rank_ref, cnt_ref, tri_ref, base_ref):
    i = pl.program_id(0)
    tm = x_ref.shape[0]

    @pl.when(i == 0)
    def _():
        r = lax.broadcasted_iota(jnp.int32, (tm, tm), 0)
        c = lax.broadcasted_iota(jnp.int32, (tm, tm), 1)
        tri_ref[...] = jnp.where(c < r, 1.0, 0.0).astype(BF16)
        base_ref[...] = jnp.zeros(base_ref.shape, F32)

    h = _rms(x_ref[...], NORM_EPS) * g_ref[...]
    h_ref[...] = _pack_halves(h)
    logits = jnp.dot(h, wr_ref[...], precision=HIGHEST, preferred_element_type=F32) + br_ref[...]
    lane = lax.broadcasted_iota(jnp.int32, logits.shape, 1)
    logits = jnp.where(lane < N_EXPERTS, logits, -jnp.inf)
    m1 = jnp.max(logits, axis=-1, keepdims=True)
    i1 = jnp.min(jnp.where(logits == m1, lane, LANES), axis=-1, keepdims=True)
    rest = jnp.where(lane == i1, -jnp.inf, logits)
    m2 = jnp.max(rest, axis=-1, keepdims=True)
    i2 = jnp.min(jnp.where(rest == m2, lane, LANES), axis=-1, keepdims=True)
    e = jnp.exp(m2 - m1)
    w1 = 1.0 / (1.0 + e)
    idx_ref[...] = jnp.where(lane == 0, i1, i2)[:, :TOP_K]
    wt_ref[...] = jnp.where(lane == 0, w1, e * w1)[:, :TOP_K]

    chosen = jnp.logical_or(lane == i1, lane == i2)
    before = jnp.dot(tri_ref[...], jnp.where(chosen, 1.0, 0.0).astype(BF16), preferred_element_type=F32)
    before = before + base_ref[...]
    r1 = jnp.sum(jnp.where(lane == i1, before, 0.0), axis=-1, keepdims=True)
    r2 = jnp.sum(jnp.where(lane == i2, before, 0.0), axis=-1, keepdims=True)
    rank_ref[...] = jnp.where(lane == 0, r1, r2)[:, :TOP_K].astype(jnp.int32)
    base_ref[...] += jnp.sum(jnp.where(chosen, 1.0, 0.0), axis=0, keepdims=True)
    cnt_ref[...] = base_ref[...].astype(jnp.int32)


def _router(x, g, w_router, b_router, tm):
    t, d = x.shape
    wr = jnp.pad(w_router.astype(F32), ((0, 0), (0, LANES - N_EXPERTS)))
    br = jnp.pad(b_router.astype(F32), (0, LANES - N_EXPERTS)).reshape(1, LANES)
    pair = pl.BlockSpec((tm, TOP_K), lambda i: (i, 0))
    return pl.pallas_call(
        _router_kernel,
        grid=(t // tm,),
        in_specs=[pl.BlockSpec((tm, d), lambda i: (i, 0)),
                  pl.BlockSpec((1, d), lambda i: (0, 0)),
                  pl.BlockSpec((d, LANES), lambda i: (0, 0)),
                  pl.BlockSpec((1, LANES), lambda i: (0, 0))],
        out_specs=[pl.BlockSpec((tm, d // 2), lambda i: (i, 0)), pair, pair, pair,
                   pl.BlockSpec((1, LANES), lambda i: (0, 0))],
        out_shape=[jax.ShapeDtypeStruct((t, d // 2), jnp.uint32),
                   jax.ShapeDtypeStruct((t, TOP_K), jnp.int32),
                   jax.ShapeDtypeStruct((t, TOP_K), F32),
                   jax.ShapeDtypeStruct((t, TOP_K), jnp.int32),
                   jax.ShapeDtypeStruct((1, LANES), jnp.int32)],
        scratch_shapes=[pltpu.VMEM((tm, tm), BF16), pltpu.VMEM((1, LANES), F32)],
        compiler_params=_params("arbitrary"),
        name="moe_router",
    )(x, g.reshape(1, d), wr, br)


def _expert_kernel(be_ref, nv_ref, x_ref, wg_ref, wu_ref, wd_ref, o_ref, xb_ref, acc_ref, *, nj):
    i = pl.program_id(0)
    j = pl.program_id(1)
    last = nj - 1
    used = nv_ref[i] > 0

    def partial_ffn():
        x = xb_ref[...]
        a = jnp.dot(x, wg_ref[...], preferred_element_type=F32)
        u = jnp.dot(x, wu_ref[...], preferred_element_type=F32)
        return jnp.dot((a * _sigmoid(a) * u).astype(BF16), wd_ref[...], preferred_element_type=F32)

    @pl.when(jnp.logical_and(used, j == 0))
    def _():
        x = _unpack_halves(x_ref[...])
        row = lax.broadcasted_iota(jnp.int32, x.shape, 0)
        xb_ref[...] = jnp.where(row < nv_ref[i], x, 0.0).astype(BF16)
        if nj == 1:
            o_ref[...] = _pack_halves(partial_ffn())
        else:
            acc_ref[...] = partial_ffn()

    if nj > 2:
        @pl.when(jnp.logical_and(used, jnp.logical_and(j > 0, j < last)))
        def _():
            acc_ref[...] += partial_ffn()

    if nj > 1:
        @pl.when(jnp.logical_and(used, j == last))
        def _():
            o_ref[...] = _pack_halves(acc_ref[...] + partial_ffn())

    @pl.when(jnp.logical_and(jnp.logical_not(used), j == last))
    def _():
        o_ref[...] = jnp.zeros(o_ref.shape, o_ref.dtype)


def _expert_ffn(xs, block_exp, n_valid, wg, wu, wd, bm, tf):
    n_rows, half = xs.shape
    d = 2 * half
    f = wg.shape[2]
    nj = f // tf

    def jj(i, j, nv):
        return jnp.where(nv[i] > 0, j, nj - 1)

    grid_spec = pltpu.PrefetchScalarGridSpec(
        num_scalar_prefetch=2,
        grid=(n_rows // bm, nj),
        in_specs=[pl.BlockSpec((bm, half), lambda i, j, be, nu: (i, 0)),
                  pl.BlockSpec((None, d, tf), lambda i, j, be, nu: (be[i], 0, jj(i, j, nu))),
                  pl.BlockSpec((None, d, tf), lambda i, j, be, nu: (be[i], 0, jj(i, j, nu))),
                  pl.BlockSpec((None, tf, d), lambda i, j, be, nu: (be[i], jj(i, j, nu), 0))],
        out_specs=pl.BlockSpec((bm, half), lambda i, j, be, nu: (i, 0)),
        scratch_shapes=[pltpu.VMEM((bm, d), BF16), pltpu.VMEM((bm, d), F32)],
    )
    return pl.pallas_call(
        functools.partial(_expert_kernel, nj=nj),
        grid_spec=grid_spec,
        out_shape=jax.ShapeDtypeStruct((n_rows, half), jnp.uint32),
        compiler_params=_params("parallel", "arbitrary"),
        name="moe_experts",
    )(block_exp, n_valid, xs, wg, wu, wd)


def _route(top_idx, rank, counts, bm):
    t = top_idx.shape[0]
    padded = (counts + bm - 1) // bm * bm
    end_padded = jnp.cumsum(padded)
    start_padded = end_padded - padded
    experts = jnp.arange(N_EXPERTS, dtype=jnp.int32)
    dest = rank + jnp.sum(jnp.where(top_idx[..., None] == experts, start_padded, 0), axis=-1)
    n_blocks = t * TOP_K // bm + N_EXPERTS
    block_start = jnp.arange(n_blocks, dtype=jnp.int32) * bm
    block_exp = jnp.minimum(jnp.sum(block_start[:, None] >= end_padded[None, :], axis=1), N_EXPERTS - 1)
    n_valid = jnp.clip((start_padded + counts)[block_exp] - block_start, 0, bm)
    return dest, block_exp.astype(jnp.int32), n_valid.astype(jnp.int32)


def _combine_kernel(x_ref, a_ref, b_ref, w_ref, g_ref, o_ref, *, final_norm):
    w = w_ref[...]
    y = x_ref[...] + w[:, 0:1] * _unpack_halves(a_ref[...]) + w[:, 1:2] * _unpack_halves(b_ref[...])
    o_ref[...] = _rms(y, NORM_EPS) * g_ref[...] if final_norm else y


def _combine(x, ab, w, g, tm):
    t, d = x.shape
    spec = pl.BlockSpec((tm, d), lambda i: (i, 0))
    final_norm = g is not None
    gain = g.reshape(1, d) if final_norm else jnp.ones((1, d), F32)
    return pl.pallas_call(
        functools.partial(_combine_kernel, final_norm=final_norm), grid=(t // tm,),
        in_specs=[spec, pl.BlockSpec((tm, d // 2), lambda i: (i, 0)),
                  pl.BlockSpec((tm, d // 2), lambda i: (i + t // tm, 0)),
                  pl.BlockSpec((tm, TOP_K), lambda i: (i, 0)),
                  pl.BlockSpec((1, d), lambda i: (0, 0))],
        out_specs=spec,
        out_shape=jax.ShapeDtypeStruct((t, d), F32), compiler_params=_params("parallel"),
        name="moe_combine",
    )(x, ab, ab, w, gain)


def _moe(x, g, w_router, b_router, wg, wu, wd, final_g, tm, bm, tf):
    h, top_idx, top_w, rank, counts = _router(x, g, w_router, b_router, tm)
    dest, block_exp, n_valid = _route(top_idx, rank, counts[0, :N_EXPERTS], bm)
    dest = dest.T
    xs = _scatter_rows(h, dest, block_exp.shape[0] * bm)
    ys = _expert_ffn(xs, block_exp, n_valid, wg, wu, wd, bm, tf)
    picked = _gather_rows(ys, dest.reshape(-1))
    return _combine(x, picked, top_w, final_g, tm)


def _cast_kernel(w_ref, o_ref):
    o_ref[...] = w_ref[...].astype(o_ref.dtype)


def _to_bf16(w, axis, tile):
    e, r, c = w.shape
    block = (None, tile, c) if axis == 0 else (None, r, tile)
    index = (lambda k, j: (k, j, 0)) if axis == 0 else (lambda k, j: (k, 0, j))
    return pl.pallas_call(
        _cast_kernel, grid=(e, (r, c)[axis] // tile),
        in_specs=[pl.BlockSpec(block, index)], out_specs=pl.BlockSpec(block, index),
        out_shape=jax.ShapeDtypeStruct(w.shape, BF16), compiler_params=_params("parallel", "parallel"),
        name="expert_weight_cast",
    )(w)


def _final_norm_kernel(x_ref, g_ref, o_ref):
    o_ref[...] = _rms(x_ref[...], NORM_EPS) * g_ref[...]


def _final_norm(x, g, tm):
    t, d = x.shape
    return pl.pallas_call(
        _final_norm_kernel, grid=(t // tm,),
        in_specs=[pl.BlockSpec((tm, d), lambda i: (i, 0)), pl.BlockSpec((1, d), lambda i: (0, 0))],
        out_specs=pl.BlockSpec((tm, d), lambda i: (i, 0)),
        out_shape=jax.ShapeDtypeStruct((t, d), F32), compiler_params=_params("parallel"),
        name="final_norm",
    )(x, g.reshape(1, d))


def _tiles(t, s):
    pick = lambda n, pref: pref if n % pref == 0 else n
    return dict(tm_proj=pick(t, 1024), tn_proj=PROJ_COLS // 4, tb_attn=pick(s, 512), ts_conv=pick(s, 512),
                tm_merge=pick(t, 512), tm_ffn=pick(t, 512), tm_row=pick(t, 1024), bm_moe=pick(t, 512))


def kernel(x, ln_mix_w, w_in, attn_lq1, attn_lk1, attn_lq2, attn_lk2, attn_subln_w, attn_w_o, conv_dw_w, conv_dw_b, conv_ln_w, conv_ln_b, conv_w_pw, ssm_conv_w, ssm_conv_b, ssm_dt_bias, ssm_A_log, ssm_D, ssm_norm_w, ssm_w_out, w_mix_out, ln_ffn_w, ffn_w_gate, ffn_w_up, ffn_w_down, moe_w_router, moe_b_router, moe_w_gate, moe_w_up, moe_w_down, ln_final_w):
    b, s, d = x.shape
    t = b * s
    depth = w_in.shape[0]
    tl = _tiles(t, s)
    xf = x.reshape(t, d)
    for l in range(depth):
        w_main = jnp.concatenate([w_in[l][:, :COL_DT], w_in[l][:, COL_DT + SSM_HEADS:]], axis=1).astype(BF16)
        w_dt = jnp.pad(w_in[l][:, COL_DT:COL_DT + SSM_HEADS], ((0, 0), (0, LANES - SSM_HEADS))).astype(BF16)
        proj, dt = _norm_matmul(xf, ln_mix_w[l], w_main, w_dt, tl["tm_proj"], tl["tn_proj"])
        proj3 = proj.reshape(b, s, PROJ_COLS)

        lam_init = 0.8 - 0.6 * math.exp(-0.3 * l)
        lam = (jnp.exp(jnp.sum(attn_lq1[l].astype(F32) * attn_lk1[l].astype(F32)))
               - jnp.exp(jnp.sum(attn_lq2[l].astype(F32) * attn_lk2[l].astype(F32))) + lam_init)
        y_att = _diff_attention(proj3, lam, attn_subln_w[l], lam_init, tl["tb_attn"])
        y_conv = _conformer_conv(proj3, conv_dw_w[l], conv_dw_b[l], conv_ln_w[l], conv_ln_b[l], tl["ts_conv"])
        y_ssm = _mamba2_ssd(proj3, dt.reshape(b, s, LANES), ssm_conv_w[l], ssm_conv_b[l], ssm_dt_bias[l],
                            ssm_A_log[l], ssm_D[l], ssm_norm_w[l])
        xf = _merge(xf, y_att.reshape(t, ATT_WIDTH), y_conv.reshape(t, CONV_CH), y_ssm.reshape(t, SSM_D_INNER),
                    proj, attn_w_o[l].astype(BF16), conv_w_pw[l].astype(BF16), ssm_w_out[l].astype(BF16),
                    w_mix_out[l].astype(BF16), tl["tm_merge"])

        i = l // 2
        if l % 2 == 0:
            xf = _ffn_dense(xf, ln_ffn_w[l], ffn_w_gate[i].astype(BF16), ffn_w_up[i].astype(BF16),
                            ffn_w_down[i].astype(BF16), tl["tm_ffn"])
        else:
            f = moe_w_gate.shape[3]
            tf = f // 2 if (f // 2) % (2 * LANES) == 0 else f
            final_g = ln_final_w if l == depth - 1 else None
            xf = _moe(xf, ln_ffn_w[l], moe_w_router[i], moe_b_router[i], _to_bf16(moe_w_gate[i], 1, tf),
                      _to_bf16(moe_w_up[i], 1, tf), _to_bf16(moe_w_down[i], 0, tf), final_g,
                      tl["tm_row"], tl["bm_moe"], tf)
    if depth % 2 == 1:
        xf = _final_norm(xf, ln_final_w, tl["tm_row"])
    return xf.reshape(b, s, d)
```

```python
import functools
import math

import jax
import jax.numpy as jnp
from jax import lax
from jax.experimental import pallas as pl
from jax.experimental.pallas import tpu as pltpu
from jax.experimental.pallas import tpu_sc as plsc

F32 = jnp.float32
BF16 = jnp.bfloat16
HIGHEST = lax.Precision.HIGHEST

D_MODEL = 1024
ATT_HEADS = 4
ATT_DH = 64
ATT_WIDTH = ATT_HEADS * 2 * ATT_DH
CONV_CH = 512
CONV_K = 31
SSM_D_INNER = 512
SSM_HEADDIM = 64
SSM_HEADS = SSM_D_INNER // SSM_HEADDIM
SSM_GROUPS = 2
SSM_STATE = 128
SSM_CONV_K = 4
SSM_CONV_DIM = SSM_D_INNER + 2 * SSM_GROUPS * SSM_STATE
SSM_CHUNK = 128
N_EXPERTS = 8
TOP_K = 2
NORM_EPS = 1e-6

LANES = 128
SUBLANES = 8
VMEM_LIMIT_BYTES = 48 * 1024 * 1024

COL_Q = 0
COL_K = COL_Q + ATT_WIDTH
COL_V = COL_K + ATT_WIDTH
COL_U = COL_V + ATT_WIDTH
COL_Z = COL_U + 2 * CONV_CH
COL_XBC = COL_Z + SSM_D_INNER
COL_DT = COL_XBC + SSM_CONV_DIM
COL_G = COL_XBC + SSM_CONV_DIM
PROJ_COLS = COL_G + 3 * D_MODEL


def _params(*semantics):
    return pltpu.CompilerParams(dimension_semantics=semantics, vmem_limit_bytes=VMEM_LIMIT_BYTES)


def _sigmoid(x):
    return 1.0 / (1.0 + jnp.exp(-x))


def _rms(x, eps):
    return x * lax.rsqrt(jnp.mean(x * x, axis=-1, keepdims=True) + eps)


def _split3(x):
    hi = x.astype(BF16)
    rest = x - hi.astype(F32)
    mid = rest.astype(BF16)
    return hi, mid, (rest - mid.astype(F32)).astype(BF16)


def _pack_halves(x):
    n = x.shape[1] // 2
    bits = lambda v: lax.bitcast_convert_type(v.astype(BF16).astype(F32), jnp.uint32)
    return (bits(x[:, :n]) >> 16) | bits(x[:, n:])


def _unpack_halves(w):
    lo = lax.bitcast_convert_type(w << 16, F32)
    hi = lax.bitcast_convert_type(w & jnp.uint32(0xFFFF0000), F32)
    return jnp.concatenate([lo, hi], axis=1)


SC_GATHER_ROWS = 64


def _gather_rows(table, idx):
    n_idx = idx.shape[0]
    width = table.shape[1]
    info = plsc.get_sparse_core_info()
    n_workers = info.num_cores * info.num_subcores
    per_worker = n_idx // n_workers
    n_chunks = per_worker // SC_GATHER_ROWS
    assert n_idx % n_workers == 0 and per_worker % SC_GATHER_ROWS == 0 and n_chunks % 2 == 0, (n_idx, n_workers)
    mesh = plsc.VectorSubcoreMesh(core_axis_name="c", subcore_axis_name="s")
    idx_buf = pltpu.VMEM((SC_GATHER_ROWS,), jnp.int32)
    row_buf = pltpu.VMEM((SC_GATHER_ROWS, width), table.dtype)

    @functools.partial(
        pl.kernel, mesh=mesh,
        out_type=jax.ShapeDtypeStruct((n_idx, width), table.dtype),
        scratch_types=[idx_buf, idx_buf, row_buf, row_buf, pltpu.SemaphoreType.DMA, pltpu.SemaphoreType.DMA],
    )
    def gather(table_hbm, idx_hbm, out_hbm, idx0, idx1, rows0, rows1, sem0, sem1):
        worker = lax.axis_index("s") * info.num_cores + lax.axis_index("c")
        slots = ((idx0, rows0, sem0), (idx1, rows1, sem1))

        def offset(c):
            return pl.multiple_of(worker * per_worker + c * SC_GATHER_ROWS, SUBLANES)

        def stream(slot):
            idx_v, rows_v, sem = slot
            return pltpu.make_async_copy(table_hbm.at[idx_v], rows_v, sem)

        def start(c, slot):
            pltpu.sync_copy(idx_hbm.at[pl.ds(offset(c), SC_GATHER_ROWS)], slot[0])
            stream(slot).start()

        start(0, slots[0])

        @pl.loop(0, n_chunks, step=2)
        def _(c0):
            for b in range(2):
                c = c0 + b
                stream(slots[b]).wait()

                @pl.when(c + 1 < n_chunks)
                def _():
                    start(c + 1, slots[1 - b])

                pltpu.sync_copy(slots[b][1], out_hbm.at[pl.ds(offset(c), SC_GATHER_ROWS)])

    return gather(table, idx)


def _scatter_rows(rows, idx, n_out):
    n_rows, width = rows.shape
    fan = idx.shape[0]
    info = plsc.get_sparse_core_info()
    n_workers = info.num_cores * info.num_subcores
    per_worker = n_rows // n_workers
    assert n_rows % n_workers == 0 and per_worker % SC_GATHER_ROWS == 0, (n_rows, n_workers)
    mesh = plsc.VectorSubcoreMesh(core_axis_name="c", subcore_axis_name="s")
    idx_buf = pltpu.VMEM((SC_GATHER_ROWS,), jnp.int32)

    @functools.partial(
        pl.kernel, mesh=mesh,
        out_type=jax.ShapeDtypeStruct((n_out, width), rows.dtype),
        scratch_types=[idx_buf] * fan + [pltpu.VMEM((SC_GATHER_ROWS, width), rows.dtype)],
    )
    def scatter(rows_hbm, idx_hbm, out_hbm, *scratch):
        idx_vs, rows_v = scratch[:fan], scratch[fan]
        worker = lax.axis_index("s") * info.num_cores + lax.axis_index("c")

        @pl.loop(0, per_worker // SC_GATHER_ROWS)
        def _(c):
            off = pl.multiple_of(worker * per_worker + c * SC_GATHER_ROWS, SUBLANES)
            pltpu.sync_copy(rows_hbm.at[pl.ds(off, SC_GATHER_ROWS)], rows_v)
            for k in range(fan):
                pltpu.sync_copy(idx_hbm.at[k, pl.ds(off, SC_GATHER_ROWS)], idx_vs[k])
                pltpu.sync_copy(rows_v, out_hbm.at[idx_vs[k]])

    return scatter(rows, idx)


def _norm_matmul_kernel(x_ref, g_ref, w_ref, ws_ref, o_ref, os_ref, h_ref):
    j = pl.program_id(1)

    @pl.when(j == 0)
    def _():
        h = (_rms(x_ref[...], NORM_EPS) * g_ref[...]).astype(h_ref.dtype)
        h_ref[...] = h
        o_ref[...] = jnp.dot(h, w_ref[...], preferred_element_type=F32).astype(o_ref.dtype)
        os_ref[...] = jnp.dot(h, ws_ref[...], preferred_element_type=F32)

    @pl.when(j > 0)
    def _():
        o_ref[...] = jnp.dot(h_ref[...], w_ref[...], preferred_element_type=F32).astype(o_ref.dtype)


def _norm_matmul(x, g, w, w_side, tm, tn):
    t, d = x.shape
    n = w.shape[1]
    ns = w_side.shape[1]
    return pl.pallas_call(
        _norm_matmul_kernel,
        grid=(t // tm, n // tn),
        in_specs=[pl.BlockSpec((tm, d), lambda i, j: (i, 0)),
                  pl.BlockSpec((1, d), lambda i, j: (0, 0)),
                  pl.BlockSpec((d, tn), lambda i, j: (0, j)),
                  pl.BlockSpec((d, ns), lambda i, j: (0, 0))],
        out_specs=[pl.BlockSpec((tm, tn), lambda i, j: (i, j)),
                   pl.BlockSpec((tm, ns), lambda i, j: (i, 0))],
        out_shape=[jax.ShapeDtypeStruct((t, n), BF16), jax.ShapeDtypeStruct((t, ns), F32)],
        scratch_shapes=[pltpu.VMEM((tm, d), BF16)],
        compiler_params=_params("parallel", "arbitrary"),
        name="norm_matmul",
    )(x, g.reshape(1, d), w, w_side)


ATT_ONES = 16
ATT_UNROLL = 4
LOG2E = 1.4426950408889634


def _attn_kernel(items_ref, lam_ref, q_ref, k_ref, v_ref, sw_ref, o_ref,
                 vt_ref, q2_ref, m_ref, acc_ref, sa_ref, ca_ref, sb_ref, cb_ref, *, tb, n_off, n_diag, out_scale):
    hw = 2 * ATT_DH
    nb = vt_ref.shape[0]

    lane = lax.broadcasted_iota(jnp.int32, (tb, hw), 1)
    for c in range(nb):
        rows = slice(c * tb, (c + 1) * tb)
        vt_ref[c, 0:hw, :] = v_ref[rows, :].astype(F32).T.astype(BF16)
        vt_ref[c, hw:hw + ATT_ONES, :] = jnp.ones((ATT_ONES, tb), BF16)
        q = (q_ref[rows, :].astype(F32) * (ATT_DH ** -0.5 * LOG2E)).astype(BF16)
        zero = jnp.zeros_like(q)
        q2_ref[c, 0:tb, :] = jnp.where(lane < ATT_DH, q, zero)
        q2_ref[c, tb:2 * tb, :] = jnp.where(lane >= ATT_DH, q, zero)
        m_ref[c] = jnp.full(m_ref.shape[1:], -jnp.inf, F32)
        acc_ref[c] = jnp.zeros(acc_ref.shape[1:], F32)

    def fill(buf, t, masked):
        s_ref, cmax_ref = buf
        qi, j = items_ref[0, t], items_ref[1, t]
        k = k_ref[pl.ds(pl.multiple_of(j * tb, tb), tb), :]
        st = lax.dot_general(k, q2_ref[qi], (((1,), (1,)), ((), ())), preferred_element_type=F32)
        if masked:
            r = lax.broadcasted_iota(jnp.int32, st.shape, 0)
            c = lax.broadcasted_iota(jnp.int32, st.shape, 1)
            st = jnp.where(r <= jnp.where(c >= tb, c - tb, c), st, -jnp.inf)
        s_ref[...] = st
        cmax_ref[...] = jnp.max(st, axis=0, keepdims=True)

    def update(buf, t, last):
        s_ref, cmax_ref = buf
        qi, j = items_ref[0, t], items_ref[1, t]
        m_old = m_ref[qi]
        m_new = jnp.maximum(m_old, cmax_ref[...])
        p = jnp.exp2(s_ref[...] - m_new).astype(BF16)
        acc = jnp.exp2(m_old - m_new) * acc_ref[qi] + jnp.dot(vt_ref[j], p, preferred_element_type=F32)
        if not last:
            acc_ref[qi] = acc
            m_ref[qi] = m_new
        else:
            o_t = acc[:hw] * (1.0 / acc[hw:hw + 1])
            o = (o_t[:, :tb] - lam_ref[0] * o_t[:, tb:]).T
            o_ref[pl.ds(pl.multiple_of(qi * tb, tb), tb), :] = (
                _rms(o, 1e-5) * (sw_ref[...] * out_scale)).astype(o_ref.dtype)

    def run(first, count, diag, cur, other):
        trips = (count - 1) // ATT_UNROLL
        bufs = (cur, other)

        def body(i, carry):
            t = first + ATT_UNROLL * i
            for u in range(ATT_UNROLL):
                fill(bufs[(u + 1) % 2], t + u + 1, diag)
                update(bufs[u % 2], t + u, diag)
            return carry

        if trips:
            lax.fori_loop(0, trips, body, 0)
        for t in range(first + ATT_UNROLL * trips, first + count - 1):
            fill(other, t + 1, diag)
            update(cur, t, diag)
            cur, other = other, cur
        return cur, other

    cur, other = (sa_ref, ca_ref), (sb_ref, cb_ref)
    if n_off:
        fill(cur, 0, False)
        cur, other = run(0, n_off, False, cur, other)
        fill(other, n_off, True)
        update(cur, n_off - 1, False)
        cur, other = other, cur
    else:
        fill(cur, 0, True)
    cur, other = run(n_off, n_diag, True, cur, other)
    update(cur, n_off + n_diag - 1, True)


def _diff_attention(proj3, lam, subln_w, lam_init, tb):
    b, s, _ = proj3.shape
    hw = 2 * ATT_DH
    nb = s // tb
    off = [(qi, j) for qi in range(nb) for j in range(qi)]
    items = jnp.asarray(list(zip(*(off + [(qi, qi) for qi in range(nb)]))), jnp.int32)
    kern = functools.partial(_attn_kernel, tb=tb, n_off=len(off), n_diag=nb, out_scale=1.0 - lam_init)
    seq = lambda col: pl.BlockSpec((None, s, hw), lambda bb, h: (bb, 0, col // hw + h))
    return pl.pallas_call(
        kern,
        grid=(b, ATT_HEADS),
        in_specs=[pl.BlockSpec(memory_space=pltpu.SMEM), pl.BlockSpec(memory_space=pltpu.SMEM),
                  seq(COL_Q), seq(COL_K), seq(COL_V), pl.BlockSpec((1, hw), lambda bb, h: (0, 0))],
        out_specs=pl.BlockSpec((None, s, hw), lambda bb, h: (bb, 0, h)),
        out_shape=jax.ShapeDtypeStruct((b, s, ATT_WIDTH), BF16),
        scratch_shapes=[pltpu.VMEM((nb, hw + ATT_ONES, tb), BF16), pltpu.VMEM((nb, 2 * tb, hw), BF16),
                        pltpu.VMEM((nb, 1, 2 * tb), F32), pltpu.VMEM((nb, hw + ATT_ONES, 2 * tb), F32),
                        pltpu.VMEM((tb, 2 * tb), F32), pltpu.VMEM((1, 2 * tb), F32),
                        pltpu.VMEM((tb, 2 * tb), F32), pltpu.VMEM((1, 2 * tb), F32)],
        compiler_params=_params("parallel", "parallel"),
        name="diff_attention",
    )(items, lam.reshape(1).astype(F32), proj3, proj3, proj3, subln_w.reshape(1, hw).astype(F32))


CONV_HALO = 32
CONV_ROWS = 64


def _cconv_kernel(a_ref, g_ref, w_ref, b_ref, lnw_ref, lnb_ref, o_ref, buf_ref, *, ts):
    i = pl.program_id(1)

    @pl.when(i == 0)
    def _():
        buf_ref[0:CONV_HALO, :] = jnp.zeros((CONV_HALO, CONV_CH), F32)

    @pl.when(i > 0)
    def _():
        buf_ref[0:CONV_HALO, :] = buf_ref[ts:ts + CONV_HALO, :]

    buf_ref[CONV_HALO:CONV_HALO + ts, :] = a_ref[...].astype(F32) * _sigmoid(g_ref[...].astype(F32))

    first = CONV_HALO - (CONV_K - 1)
    for c in range(ts // CONV_ROWS):
        base = c * CONV_ROWS
        acc = jnp.zeros((CONV_ROWS, CONV_CH), F32) + b_ref[...]
        for res in range(SUBLANES):
            taps = [k for k in range(CONV_K) if (first + k) % SUBLANES == res]
            rows = CONV_ROWS + (SUBLANES if res else 0)
            part = None
            for k in taps:
                lo = base + first + k - res
                term = w_ref[k:k + 1, :] * buf_ref[lo:lo + rows, :]
                part = term if part is None else part + term
            acc = acc + part[res:res + CONV_ROWS]
        mu = jnp.mean(acc, axis=-1, keepdims=True)
        xc = acc - mu
        var = jnp.mean(xc * xc, axis=-1, keepdims=True)
        y = xc * lax.rsqrt(var + 1e-5) * lnw_ref[...] + lnb_ref[...]
        o_ref[base:base + CONV_ROWS, :] = (y * _sigmoid(y)).astype(o_ref.dtype)


def _conformer_conv(proj3, dw_w, dw_b, ln_w, ln_b, ts):
    b, s, _ = proj3.shape
    c = CONV_CH
    row = lambda a: a.reshape(1, c).astype(F32)
    return pl.pallas_call(
        functools.partial(_cconv_kernel, ts=ts),
        grid=(b, s // ts),
        in_specs=[pl.BlockSpec((None, ts, c), lambda bb, i: (bb, i, COL_U // c)),
                  pl.BlockSpec((None, ts, c), lambda bb, i: (bb, i, COL_U // c + 1)),
                  pl.BlockSpec((CONV_K, c), lambda bb, i: (0, 0)),
                  pl.BlockSpec((1, c), lambda bb, i: (0, 0)),
                  pl.BlockSpec((1, c), lambda bb, i: (0, 0)),
                  pl.BlockSpec((1, c), lambda bb, i: (0, 0))],
        out_specs=pl.BlockSpec((None, ts, c), lambda bb, i: (bb, i, 0)),
        out_shape=jax.ShapeDtypeStruct((b, s, c), BF16),
        scratch_shapes=[pltpu.VMEM((CONV_HALO + ts, c), F32)],
        compiler_params=_params("parallel", "arbitrary"),
        name="conformer_conv",
    )(proj3, proj3, dw_w.astype(F32), row(dw_b), row(ln_w), row(ln_b))


SSD_HALO = 16


def _ssd_kernel(z_ref, xbc_ref, dt_ref, cw_ref, cb_ref, dtb_ref, a_ref, dsk_ref, nw_ref, e_ref,
                o_ref, buf_ref, h_ref, *, chunk):
    L = chunk
    ci = pl.program_id(1)
    rows = range(z_ref.shape[0])

    @pl.when(ci == 0)
    def _():
        for bb in rows:
            buf_ref[bb, 0:SSD_HALO, :] = jnp.zeros((SSD_HALO, SSM_CONV_DIM), BF16)
        h_ref[...] = jnp.zeros(h_ref.shape, F32)

    @pl.when(ci > 0)
    def _():
        for bb in rows:
            buf_ref[bb, 0:SSD_HALO, :] = buf_ref[bb, L:L + SSD_HALO, :]

    for bb in rows:
        _ssd_chunk(z_ref.at[bb], xbc_ref.at[bb], dt_ref.at[bb], cw_ref, cb_ref, dtb_ref, a_ref, dsk_ref, nw_ref,
                   e_ref, o_ref.at[bb], buf_ref.at[bb], h_ref.at[bb], L)


def _ssd_chunk(z_ref, xbc_ref, dt_ref, cw_ref, cb_ref, dtb_ref, a_ref, dsk_ref, nw_ref, e_ref,
               o_ref, buf_ref, h_ref, L):
    gw = SSM_D_INNER // SSM_GROUPS
    hpg = SSM_HEADS // SSM_GROUPS

    buf_ref[SSD_HALO:SSD_HALO + L, :] = xbc_ref[...]
    xraw = buf_ref[...]
    r = lax.broadcasted_iota(jnp.int32, (L, SSD_HALO + L), 0)
    c = lax.broadcasted_iota(jnp.int32, (L, SSD_HALO + L), 1)
    xc = jnp.zeros((L, SSM_CONV_DIM), F32) + cb_ref[...]
    for k in range(SSM_CONV_K):
        shift = jnp.where(c == r + (SSD_HALO - (SSM_CONV_K - 1) + k), 1.0, 0.0).astype(BF16)
        xc = xc + cw_ref[k:k + 1, :] * jnp.dot(shift, xraw, preferred_element_type=F32)
    xc = xc * _sigmoid(xc)
    xs = xc[:, :SSM_D_INNER]
    bm = xc[:, SSM_D_INNER:SSM_D_INNER + SSM_GROUPS * SSM_STATE]
    cm = xc[:, SSM_D_INNER + SSM_GROUPS * SSM_STATE:]

    dtr = dt_ref[...] + dtb_ref[...]
    dt = jnp.maximum(dtr, 0.0) + jnp.log(1.0 + jnp.exp(-jnp.abs(dtr)))
    da = dt * a_ref[...]
    row = lax.broadcasted_iota(jnp.int32, (L, L), 0)
    col = lax.broadcasted_iota(jnp.int32, (L, L), 1)
    causal = col <= row
    a_cum = sum(jnp.dot(jnp.where(causal, 1.0, 0.0).astype(BF16), part, preferred_element_type=F32)
                for part in _split3(da))
    a_cum_t = a_cum.T

    expand = e_ref[...]
    per_head = jnp.concatenate([dt, jnp.exp(a_cum), jnp.exp(a_cum[L - 1:L, :] - a_cum)], axis=0)
    full = sum(jnp.dot(part, expand, preferred_element_type=F32) for part in _split3(per_head))
    dt_full, ea_full, dend_full = full[:L], full[L:2 * L], full[2 * L:]
    cdec_full = ea_full[L - 1:L, :]

    xdt = xs * dt_full
    xdt_b = xdt.astype(BF16)
    xw_b = (xdt * dend_full).astype(BF16)

    y_parts = []
    for g in range(SSM_GROUPS):
        cg = cm[:, g * SSM_STATE:(g + 1) * SSM_STATE].astype(BF16)
        bg_t = bm[:, g * SSM_STATE:(g + 1) * SSM_STATE].T.astype(BF16)
        cbm = jnp.dot(cg, bg_t, preferred_element_type=F32)
        h_in = h_ref[:, g * gw:(g + 1) * gw]
        y_off = jnp.dot(cg, h_in.astype(BF16), preferred_element_type=F32)
        states = jnp.dot(bg_t, xw_b[:, g * gw:(g + 1) * gw], preferred_element_type=F32)
        h_ref[:, g * gw:(g + 1) * gw] = h_in * cdec_full[:, g * gw:(g + 1) * gw] + states
        yd = []
        for r in range(hpg):
            hh = g * hpg + r
            seg = a_cum[:, hh:hh + 1] - a_cum_t[hh:hh + 1, :]
            dec = jnp.exp(jnp.where(causal, seg, -jnp.inf))
            yd.append(jnp.dot((cbm * dec).astype(BF16), xdt_b[:, hh * SSM_HEADDIM:(hh + 1) * SSM_HEADDIM],
                              preferred_element_type=F32))
        y_g = (jnp.concatenate(yd, axis=-1) + y_off * ea_full[:, g * gw:(g + 1) * gw]
               + xs[:, g * gw:(g + 1) * gw] * dsk_ref[:, g * gw:(g + 1) * gw])
        zg = z_ref[:, g * gw:(g + 1) * gw].astype(F32)
        y_parts.append(_rms(y_g * (zg * _sigmoid(zg)), 1e-5))
    o_ref[...] = (jnp.concatenate(y_parts, axis=-1) * nw_ref[...]).astype(o_ref.dtype)


def _mamba2_ssd(proj3, dt3, conv_w, conv_b, dt_bias, a_log, d_skip, norm_w):
    b, s, _ = proj3.shape
    L = SSM_CHUNK
    di, cd = SSM_D_INNER, SSM_CONV_DIM
    pad_h = lambda a: jnp.pad(a.astype(F32), (0, LANES - SSM_HEADS)).reshape(1, LANES)
    head_of_channel = jnp.arange(di, dtype=jnp.int32) // SSM_HEADDIM
    expand = (jnp.arange(LANES, dtype=jnp.int32)[:, None] == head_of_channel[None, :]).astype(BF16)
    const = lambda shape: pl.BlockSpec(shape, lambda bb, c: (0, 0))
    nb = next(n for n in (4, 2, 1) if b % n == 0)
    return pl.pallas_call(
        functools.partial(_ssd_kernel, chunk=L),
        grid=(b // nb, s // L),
        in_specs=[pl.BlockSpec((nb, L, di), lambda bb, c: (bb, c, COL_Z // di)),
                  pl.BlockSpec((nb, L, cd), lambda bb, c: (bb, c, COL_XBC // cd)),
                  pl.BlockSpec((nb, L, LANES), lambda bb, c: (bb, c, 0)),
                  const((SSM_CONV_K, cd)), const((1, cd)), const((1, LANES)), const((1, LANES)),
                  const((1, di)), const((1, di)), const((LANES, di))],
        out_specs=pl.BlockSpec((nb, L, di), lambda bb, c: (bb, c, 0)),
        out_shape=jax.ShapeDtypeStruct((b, s, di), BF16),
        scratch_shapes=[pltpu.VMEM((nb, SSD_HALO + L, cd), BF16), pltpu.VMEM((nb, SSM_STATE, di), F32)],
        compiler_params=_params("parallel", "arbitrary"),
        name="mamba2_ssd",
    )(proj3, proj3, dt3, conv_w.astype(F32), conv_b.reshape(1, cd).astype(F32), pad_h(dt_bias),
      pad_h(-jnp.exp(a_log.astype(F32))), jnp.repeat(d_skip.astype(F32), SSM_HEADDIM).reshape(1, di),
      norm_w.reshape(1, di).astype(F32), expand)


def _merge_kernel(x_ref, ya_ref, yc_ref, ys_ref, ga_ref, gc_ref, gs_ref, wa_ref, wc_ref, ws_ref, wm_ref, o_ref):
    def branch(y_ref, g_ref, w_ref):
        return _sigmoid(g_ref[...].astype(F32)) * jnp.dot(y_ref[...], w_ref[...], preferred_element_type=F32)

    merged = branch(ya_ref, ga_ref, wa_ref) + branch(yc_ref, gc_ref, wc_ref) + branch(ys_ref, gs_ref, ws_ref)
    o_ref[...] = x_ref[...] + jnp.dot(merged.astype(BF16), wm_ref[...], preferred_element_type=F32)


def _merge(x, ya, yc, ys, proj, wa, wc, ws, wm, tm):
    t, d = x.shape
    w = ya.shape[1]
    rows = lambda width, cb: pl.BlockSpec((tm, width), lambda i: (i, cb))
    const = lambda shape: pl.BlockSpec(shape, lambda i: (0, 0))
    return pl.pallas_call(
        _merge_kernel,
        grid=(t // tm,),
        in_specs=[rows(d, 0), rows(w, 0), rows(w, 0), rows(w, 0),
                  rows(d, COL_G // d), rows(d, COL_G // d + 1), rows(d, COL_G // d + 2),
                  const((w, d)), const((w, d)), const((w, d)), const((d, d))],
        out_specs=rows(d, 0),
        out_shape=jax.ShapeDtypeStruct((t, d), F32),
        compiler_params=_params("parallel"),
        name="gated_merge",
    )(x, ya, yc, ys, proj, proj, proj, wa, wc, ws, wm)


def _ffn_kernel(x_ref, g_ref, wg_ref, wu_ref, wd_ref, o_ref):
    x = x_ref[...]
    h = (_rms(x, NORM_EPS) * g_ref[...]).astype(BF16)
    a = jnp.dot(h, wg_ref[...], preferred_element_type=F32)
    u = jnp.dot(h, wu_ref[...], preferred_element_type=F32)
    o_ref[...] = x + jnp.dot((a * _sigmoid(a) * u).astype(BF16), wd_ref[...], preferred_element_type=F32)


def _ffn_dense(x, g, wg, wu, wd, tm):
    t, d = x.shape
    f = wg.shape[1]
    resident = lambda shape: pl.BlockSpec(shape, lambda i: (0, 0), pipeline_mode=pl.Buffered(1))
    return pl.pallas_call(
        _ffn_kernel,
        grid=(t // tm,),
        in_specs=[pl.BlockSpec((tm, d), lambda i: (i, 0)),
                  pl.BlockSpec((1, d), lambda i: (0, 0)),
                  resident((d, f)), resident((d, f)), resident((f, d))],
        out_specs=pl.BlockSpec((tm, d), lambda i: (i, 0)),
        out_shape=jax.ShapeDtypeStruct((t, d), F32),
        compiler_params=_params("parallel"),
        name="ffn_dense",
    )(x, g.reshape(1, d), wg, wu, wd)


def _router_kernel(x_ref, g_ref, wr_ref, br_ref, h_ref, idx_ref, wt_ref, rank_ref, cnt_ref, tri_ref, base_ref):
    i = pl.program_id(0)
    tm = x_ref.shape[0]

    @pl.when(i == 0)
    def _():
        r = lax.broadcasted_iota(jnp.int32, (tm, tm), 0)
        c = lax.broadcasted_iota(jnp.int32, (tm, tm), 1)
        tri_ref[...] = jnp.where(c < r, 1.0, 0.0).astype(BF16)
        base_ref[...] = jnp.zeros(base_ref.shape, F32)

    h = _rms(x_ref[...], NORM_EPS) * g_ref[...]
    h_ref[...] = _pack_halves(h)
    logits = jnp.dot(h, wr_ref[...], precision=HIGHEST, preferred_element_type=F32) + br_ref[...]
    lane = lax.broadcasted_iota(jnp.int32, logits.shape, 1)
    logits = jnp.where(lane < N_EXPERTS, logits, -jnp.inf)
    m1 = jnp.max(logits, axis=-1, keepdims=True)
    i1 = jnp.min(jnp.where(logits == m1, lane, LANES), axis=-1, keepdims=True)
    rest = jnp.where(lane == i1, -jnp.inf, logits)
    m2 = jnp.max(rest, axis=-1, keepdims=True)
    i2 = jnp.min(jnp.where(rest == m2, lane, LANES), axis=-1, keepdims=True)
    e = jnp.exp(m2 - m1)
    w1 = 1.0 / (1.0 + e)
    idx_ref[...] = jnp.where(lane == 0, i1, i2)[:, :TOP_K]
    wt_ref[...] = jnp.where(lane == 0, w1, e * w1)[:, :TOP_K]

    chosen = jnp.logical_or(lane == i1, lane == i2)
    before = jnp.dot(tri_ref[...], jnp.where(chosen, 1.0, 0.0).astype(BF16), preferred_element_type=F32)
    before = before + base_ref[...]
    r1 = jnp.sum(jnp.where(lane == i1, before, 0.0), axis=-1, keepdims=True)
    r2 = jnp.sum(jnp.where(lane == i2, before, 0.0), axis=-1, keepdims=True)
    rank_ref[...] = jnp.where(lane == 0, r1, r2)[:, :TOP_K].astype(jnp.int32)
    base_ref[...] += jnp.sum(jnp.where(chosen, 1.0, 0.0), axis=0, keepdims=True)
    cnt_ref[...] = base_ref[...].astype(jnp.int32)


def _router(x, g, w_router, b_router, tm):
    t, d = x.shape
    wr = jnp.pad(w_router.astype(F32), ((0, 0), (0, LANES - N_EXPERTS)))
    br = jnp.pad(b_router.astype(F32), (0, LANES - N_EXPERTS)).reshape(1, LANES)
    pair = pl.BlockSpec((tm, TOP_K), lambda i: (i, 0))
    return pl.pallas_call(
        _router_kernel,
        grid=(t // tm,),
        in_specs=[pl.BlockSpec((tm, d), lambda i: (i, 0)),
                  pl.BlockSpec((1, d), lambda i: (0, 0)),
                  pl.BlockSpec((d, LANES), lambda i: (0, 0)),
                  pl.BlockSpec((1, LANES), lambda i: (0, 0))],
        out_specs=[pl.BlockSpec((tm, d // 2), lambda i: (i, 0)), pair, pair, pair,
                   pl.BlockSpec((1, LANES), lambda i: (0, 0))],
        out_shape=[jax.ShapeDtypeStruct((t, d // 2), jnp.uint32),
                   jax.ShapeDtypeStruct((t, TOP_K), jnp.int32),
                   jax.ShapeDtypeStruct((t, TOP_K), F32),
                   jax.ShapeDtypeStruct((t, TOP_K), jnp.int32),
                   jax.ShapeDtypeStruct((1, LANES), jnp.int32)],
        scratch_shapes=[pltpu.VMEM((tm, tm), BF16), pltpu.VMEM((1, LANES), F32)],
        compiler_params=_params("arbitrary"),
        name="moe_router",
    )(x, g.reshape(1, d), wr, br)


def _expert_kernel(be_ref, nv_ref, x_ref, wg_ref, wu_ref, wd_ref, o_ref, xb_ref, acc_ref, *, nj):
    i = pl.program_id(0)
    j = pl.program_id(1)
    last = nj - 1
    used = nv_ref[i] > 0

    def partial_ffn():
        x = xb_ref[...]
        a = jnp.dot(x, wg_ref[...], preferred_element_type=F32)
        u = jnp.dot(x, wu_ref[...], preferred_element_type=F32)
        return jnp.dot((a * _sigmoid(a) * u).astype(BF16), wd_ref[...], preferred_element_type=F32)

    @pl.when(jnp.logical_and(used, j == 0))
    def _():
        x = _unpack_halves(x_ref[...])
        row = lax.broadcasted_iota(jnp.int32, x.shape, 0)
        xb_ref[...] = jnp.where(row < nv_ref[i], x, 0.0).astype(BF16)
        if nj == 1:
            o_ref[...] = _pack_halves(partial_ffn())
        else:
            acc_ref[...] = partial_ffn()

    if nj > 2:
        @pl.when(jnp.logical_and(used, jnp.logical_and(j > 0, j < last)))
        def _():
            acc_ref[...] += partial_ffn()

    if nj > 1:
        @pl.when(jnp.logical_and(used, j == last))
        def _():
            o_ref[...] = _pack_halves(acc_ref[...] + partial_ffn())

    @pl.when(jnp.logical_and(jnp.logical_not(used), j == last))
    def _():
        o_ref[...] = jnp.zeros(o_ref.shape, o_ref.dtype)


def _expert_ffn(xs, block_exp, n_valid, wg, wu, wd, bm, tf):
    n_rows, half = xs.shape
    d = 2 * half
    f = wg.shape[2]
    nj = f // tf

    def jj(i, j, nv):
        return jnp.where(nv[i] > 0, j, nj - 1)

    grid_spec = pltpu.PrefetchScalarGridSpec(
        num_scalar_prefetch=2,
        grid=(n_rows // bm, nj),
        in_specs=[pl.BlockSpec((bm, half), lambda i, j, be, nu: (i, 0)),
                  pl.BlockSpec((None, d, tf), lambda i, j, be, nu: (be[i], 0, jj(i, j, nu))),
                  pl.BlockSpec((None, d, tf), lambda i, j, be, nu: (be[i], 0, jj(i, j, nu))),
                  pl.BlockSpec((None, tf, d), lambda i, j, be, nu: (be[i], jj(i, j, nu), 0))],
        out_specs=pl.BlockSpec((bm, half), lambda i, j, be, nu: (i, 0)),
        scratch_shapes=[pltpu.VMEM((bm, d), BF16), pltpu.VMEM((bm, d), F32)],
    )
    return pl.pallas_call(
        functools.partial(_expert_kernel, nj=nj),
        grid_spec=grid_spec,
        out_shape=jax.ShapeDtypeStruct((n_rows, half), jnp.uint32),
        compiler_params=_params("parallel", "arbitrary"),
        name="moe_experts",
    )(block_exp, n_valid, xs, wg, wu, wd)


def _route(top_idx, rank, counts, bm):
    t = top_idx.shape[0]
    padded = (counts + bm - 1) // bm * bm
    end_padded = jnp.cumsum(padded)
    start_padded = end_padded - padded
    experts = jnp.arange(N_EXPERTS, dtype=jnp.int32)
    dest = rank + jnp.sum(jnp.where(top_idx[..., None] == experts, start_padded, 0), axis=-1)
    n_blocks = t * TOP_K // bm + N_EXPERTS
    block_start = jnp.arange(n_blocks, dtype=jnp.int32) * bm
    block_exp = jnp.minimum(jnp.sum(block_start[:, None] >= end_padded[None, :], axis=1), N_EXPERTS - 1)
    n_valid = jnp.clip((start_padded + counts)[block_exp] - block_start, 0, bm)
    return dest, block_exp.astype(jnp.int32), n_valid.astype(jnp.int32)


def _combine_kernel(x_ref, a_ref, b_ref, w_ref, g_ref, o_ref, *, final_norm):
    w = w_ref[...]
    y = x_ref[...] + w[:, 0:1] * _unpack_halves(a_ref[...]) + w[:, 1:2] * _unpack_halves(b_ref[...])
    o_ref[...] = _rms(y, NORM_EPS) * g_ref[...] if final_norm else y


def _combine(x, ab, w, g, tm):
    t, d = x.shape
    spec = pl.BlockSpec((tm, d), lambda i: (i, 0))
    final_norm = g is not None
    gain = g.reshape(1, d) if final_norm else jnp.ones((1, d), F32)
    return pl.pallas_call(
        functools.partial(_combine_kernel, final_norm=final_norm), grid=(t // tm,),
        in_specs=[spec, pl.BlockSpec((tm, d // 2), lambda i: (i, 0)),
                  pl.BlockSpec((tm, d // 2), lambda i: (i + t // tm, 0)),
                  pl.BlockSpec((tm, TOP_K), lambda i: (i, 0)),
                  pl.BlockSpec((1, d), lambda i: (0, 0))],
        out_specs=spec,
        out_shape=jax.ShapeDtypeStruct((t, d), F32), compiler_params=_params("parallel"),
        name="moe_combine",
    )(x, ab, ab, w, gain)


def _moe(x, g, w_router, b_router, wg, wu, wd, final_g, tm, bm, tf):
    h, top_idx, top_w, rank, counts = _router(x, g, w_router, b_router, tm)
    dest, block_exp, n_valid = _route(top_idx, rank, counts[0, :N_EXPERTS], bm)
    dest = dest.T
    xs = _scatter_rows(h, dest, block_exp.shape[0] * bm)
    ys = _expert_ffn(xs, block_exp, n_valid, wg, wu, wd, bm, tf)
    picked = _gather_rows(ys, dest.reshape(-1))
    return _combine(x, picked, top_w, final_g, tm)


def _cast_kernel(w_ref, o_ref):
    o_ref[...] = w_ref[...].astype(o_ref.dtype)


def _to_bf16(w, axis, tile):
    e, r, c = w.shape
    block = (None, tile, c) if axis == 0 else (None, r, tile)
    index = (lambda k, j: (k, j, 0)) if axis == 0 else (lambda k, j: (k, 0, j))
    return pl.pallas_call(
        _cast_kernel, grid=(e, (r, c)[axis] // tile),
        in_specs=[pl.BlockSpec(block, index)], out_specs=pl.BlockSpec(block, index),
        out_shape=jax.ShapeDtypeStruct(w.shape, BF16), compiler_params=_params("parallel", "parallel"),
        name="expert_weight_cast",
    )(w)


def _final_norm_kernel(x_ref, g_ref, o_ref):
    o_ref[...] = _rms(x_ref[...], NORM_EPS) * g_ref[...]


def _final_norm(x, g, tm):
    t, d = x.shape
    return pl.pallas_call(
        _final_norm_kernel, grid=(t // tm,),
        in_specs=[pl.BlockSpec((tm, d), lambda i: (i, 0)), pl.BlockSpec((1, d), lambda i: (0, 0))],
        out_specs=pl.BlockSpec((tm, d), lambda i: (i, 0)),
        out_shape=jax.ShapeDtypeStruct((t, d), F32), compiler_params=_params("parallel"),
        name="final_norm",
    )(x, g.reshape(1, d))


def _tiles(t, s):
    pick = lambda n, pref: pref if n % pref == 0 else n
    return dict(tm_proj=pick(t, 1024), tn_proj=PROJ_COLS // 4, tb_attn=pick(s, 512), ts_conv=pick(s, 512),
                tm_merge=pick(t, 512), tm_ffn=pick(t, 512), tm_row=pick(t, 1024), bm_moe=pick(t, 512))


def kernel(x, ln_mix_w, w_in, attn_lq1, attn_lk1, attn_lq2, attn_lk2, attn_subln_w, attn_w_o, conv_dw_w, conv_dw_b, conv_ln_w, conv_ln_b, conv_w_pw, ssm_conv_w, ssm_conv_b, ssm_dt_bias, ssm_A_log, ssm_D, ssm_norm_w, ssm_w_out, w_mix_out, ln_ffn_w, ffn_w_gate, ffn_w_up, ffn_w_down, moe_w_router, moe_b_router, moe_w_gate, moe_w_up, moe_w_down, ln_final_w):
    b, s, d = x.shape
    t = b * s
    depth = w_in.shape[0]
    tl = _tiles(t, s)
    xf = x.reshape(t, d)
    for l in range(depth):
        w_main = jnp.concatenate([w_in[l][:, :COL_DT], w_in[l][:, COL_DT + SSM_HEADS:]], axis=1).astype(BF16)
        w_dt = jnp.pad(w_in[l][:, COL_DT:COL_DT + SSM_HEADS], ((0, 0), (0, LANES - SSM_HEADS))).astype(BF16)
        proj, dt = _norm_matmul(xf, ln_mix_w[l], w_main, w_dt, tl["tm_proj"], tl["tn_proj"])
        proj3 = proj.reshape(b, s, PROJ_COLS)

        lam_init = 0.8 - 0.6 * math.exp(-0.3 * l)
        lam = (jnp.exp(jnp.sum(attn_lq1[l].astype(F32) * attn_lk1[l].astype(F32)))
               - jnp.exp(jnp.sum(attn_lq2[l].astype(F32) * attn_lk2[l].astype(F32))) + lam_init)
        y_att = _diff_attention(proj3, lam, attn_subln_w[l], lam_init, tl["tb_attn"])
        y_conv = _conformer_conv(proj3, conv_dw_w[l], conv_dw_b[l], conv_ln_w[l], conv_ln_b[l], tl["ts_conv"])
        y_ssm = _mamba2_ssd(proj3, dt.reshape(b, s, LANES), ssm_conv_w[l], ssm_conv_b[l], ssm_dt_bias[l],
                            ssm_A_log[l], ssm_D[l], ssm_norm_w[l])
        xf = _merge(xf, y_att.reshape(t, ATT_WIDTH), y_conv.reshape(t, CONV_CH), y_ssm.reshape(t, SSM_D_INNER),
                    proj, attn_w_o[l].astype(BF16), conv_w_pw[l].astype(BF16), ssm_w_out[l].astype(BF16),
                    w_mix_out[l].astype(BF16), tl["tm_merge"])

        i = l // 2
        if l % 2 == 0:
            xf = _ffn_dense(xf, ln_ffn_w[l], ffn_w_gate[i].astype(BF16), ffn_w_up[i].astype(BF16),
                            ffn_w_down[i].astype(BF16), tl["tm_ffn"])
        else:
            f = moe_w_gate.shape[3]
            tf = f // 2 if (f // 2) % (2 * LANES) == 0 else f
            final_g = ln_final_w if l == depth - 1 else None
            xf = _moe(xf, ln_ffn_w[l], moe_w_router[i], moe_b_router[i], _to_bf16(moe_w_gate[i], 1, tf),
                      _to_bf16(moe_w_up[i], 1, tf), _to_bf16(moe_w_down[i], 0, tf), final_g,
                      tl["tm_row"], tl["bm_moe"], tf)
    if depth % 2 == 1:
        xf = _final_norm(xf, ln_final_w, tl["tm_row"])
    return xf.reshape(b, s, d)
```

```python
import functools
import math

import jax
import jax.numpy as jnp
from jax import lax
from jax.experimental import pallas as pl
from jax.experimental.pallas import tpu as pltpu
from jax.experimental.pallas import tpu_sc as plsc

F32 = jnp.float32
BF16 = jnp.bfloat16
HIGHEST = lax.Precision.HIGHEST

D_MODEL = 1024
ATT_HEADS = 4
ATT_DH = 64
ATT_WIDTH = ATT_HEADS * 2 * ATT_DH
CONV_CH = 512
CONV_K = 31
SSM_D_INNER = 512
SSM_HEADDIM = 64
SSM_HEADS = SSM_D_INNER // SSM_HEADDIM
SSM_GROUPS = 2
SSM_STATE = 128
SSM_CONV_K = 4
SSM_CONV_DIM = SSM_D_INNER + 2 * SSM_GROUPS * SSM_STATE
SSM_CHUNK = 128
N_EXPERTS = 8
TOP_K = 2
NORM_EPS = 1e-6

LANES = 128
SUBLANES = 8
VMEM_LIMIT_BYTES = 48 * 1024 * 1024

COL_Q = 0
COL_K = COL_Q + ATT_WIDTH
COL_V = COL_K + ATT_WIDTH
COL_U = COL_V + ATT_WIDTH
COL_Z = COL_U + 2 * CONV_CH
COL_XBC = COL_Z + SSM_D_INNER
COL_DT = COL_XBC + SSM_CONV_DIM
COL_G = COL_XBC + SSM_CONV_DIM
PROJ_COLS = COL_G + 3 * D_MODEL


def _params(*semantics):
    return pltpu.CompilerParams(dimension_semantics=semantics, vmem_limit_bytes=VMEM_LIMIT_BYTES)


def _sigmoid(x):
    return 1.0 / (1.0 + jnp.exp(-x))


def _rms(x, eps):
    return x * lax.rsqrt(jnp.mean(x * x, axis=-1, keepdims=True) + eps)


def _split3(x):
    hi = x.astype(BF16)
    rest = x - hi.astype(F32)
    mid = rest.astype(BF16)
    return hi, mid, (rest - mid.astype(F32)).astype(BF16)


def _pack_halves(x):
    n = x.shape[1] // 2
    bits = lambda v: lax.bitcast_convert_type(v.astype(BF16).astype(F32), jnp.uint32)
    return (bits(x[:, :n]) >> 16) | bits(x[:, n:])


def _unpack_halves(w):
    lo = lax.bitcast_convert_type(w << 16, F32)
    hi = lax.bitcast_convert_type(w & jnp.uint32(0xFFFF0000), F32)
    return jnp.concatenate([lo, hi], axis=1)


SC_GATHER_ROWS = 64


def _gather_rows(table, idx):
    n_idx = idx.shape[0]
    width = table.shape[1]
    info = plsc.get_sparse_core_info()
    n_workers = info.num_cores * info.num_subcores
    per_worker = n_idx // n_workers
    n_chunks = per_worker // SC_GATHER_ROWS
    assert n_idx % n_workers == 0 and per_worker % SC_GATHER_ROWS == 0 and n_chunks % 2 == 0, (n_idx, n_workers)
    mesh = plsc.VectorSubcoreMesh(core_axis_name="c", subcore_axis_name="s")
    idx_buf = pltpu.VMEM((SC_GATHER_ROWS,), jnp.int32)
    row_buf = pltpu.VMEM((SC_GATHER_ROWS, width), table.dtype)

    @functools.partial(
        pl.kernel, mesh=mesh,
        out_type=jax.ShapeDtypeStruct((n_idx, width), table.dtype),
        scratch_types=[idx_buf, idx_buf, row_buf, row_buf, pltpu.SemaphoreType.DMA, pltpu.SemaphoreType.DMA],
    )
    def gather(table_hbm, idx_hbm, out_hbm, idx0, idx1, rows0, rows1, sem0, sem1):
        worker = lax.axis_index("s") * info.num_cores + lax.axis_index("c")
        slots = ((idx0, rows0, sem0), (idx1, rows1, sem1))

        def offset(c):
            return pl.multiple_of(worker * per_worker + c * SC_GATHER_ROWS, SUBLANES)

        def stream(slot):
            idx_v, rows_v, sem = slot
            return pltpu.make_async_copy(table_hbm.at[idx_v], rows_v, sem)

        def start(c, slot):
            pltpu.sync_copy(idx_hbm.at[pl.ds(offset(c), SC_GATHER_ROWS)], slot[0])
            stream(slot).start()

        start(0, slots[0])

        @pl.loop(0, n_chunks, step=2)
        def _(c0):
            for b in range(2):
                c = c0 + b
                stream(slots[b]).wait()

                @pl.when(c + 1 < n_chunks)
                def _():
                    start(c + 1, slots[1 - b])

                pltpu.sync_copy(slots[b][1], out_hbm.at[pl.ds(offset(c), SC_GATHER_ROWS)])

    return gather(table, idx)


def _scatter_rows(rows, idx, n_out):
    n_rows, width = rows.shape
    fan = idx.shape[0]
    info = plsc.get_sparse_core_info()
    n_workers = info.num_cores * info.num_subcores
    per_worker = n_rows // n_workers
    assert n_rows % n_workers == 0 and per_worker % SC_GATHER_ROWS == 0, (n_rows, n_workers)
    mesh = plsc.VectorSubcoreMesh(core_axis_name="c", subcore_axis_name="s")
    idx_buf = pltpu.VMEM((SC_GATHER_ROWS,), jnp.int32)

    @functools.partial(
        pl.kernel, mesh=mesh,
        out_type=jax.ShapeDtypeStruct((n_out, width), rows.dtype),
        scratch_types=[idx_buf] * fan + [pltpu.VMEM((SC_GATHER_ROWS, width), rows.dtype)],
    )
    def scatter(rows_hbm, idx_hbm, out_hbm, *scratch):
        idx_vs, rows_v = scratch[:fan], scratch[fan]
        worker = lax.axis_index("s") * info.num_cores + lax.axis_index("c")

        @pl.loop(0, per_worker // SC_GATHER_ROWS)
        def _(c):
            off = pl.multiple_of(worker * per_worker + c * SC_GATHER_ROWS, SUBLANES)
            pltpu.sync_copy(rows_hbm.at[pl.ds(off, SC_GATHER_ROWS)], rows_v)
            for k in range(fan):
                pltpu.sync_copy(idx_hbm.at[k, pl.ds(off, SC_GATHER_ROWS)], idx_vs[k])
                pltpu.sync_copy(rows_v, out_hbm.at[idx_vs[k]])

    return scatter(rows, idx)


def _norm_matmul_kernel(x_ref, g_ref, w_ref, ws_ref, o_ref, os_ref, h_ref):
    j = pl.program_id(1)

    @pl.when(j == 0)
    def _():
        h = (_rms(x_ref[...], NORM_EPS) * g_ref[...]).astype(h_ref.dtype)
        h_ref[...] = h
        o_ref[...] = jnp.dot(h, w_ref[...], preferred_element_type=F32).astype(o_ref.dtype)
        os_ref[...] = jnp.dot(h, ws_ref[...], preferred_element_type=F32)

    @pl.when(j > 0)
    def _():
        o_ref[...] = jnp.dot(h_ref[...], w_ref[...], preferred_element_type=F32).astype(o_ref.dtype)


def _norm_matmul(x, g, w, w_side, tm, tn):
    t, d = x.shape
    n = w.shape[1]
    ns = w_side.shape[1]
    return pl.pallas_call(
        _norm_matmul_kernel,
        grid=(t // tm, n // tn),
        in_specs=[pl.BlockSpec((tm, d), lambda i, j: (i, 0)),
                  pl.BlockSpec((1, d), lambda i, j: (0, 0)),
                  pl.BlockSpec((d, tn), lambda i, j: (0, j)),
                  pl.BlockSpec((d, ns), lambda i, j: (0, 0))],
        out_specs=[pl.BlockSpec((tm, tn), lambda i, j: (i, j)),
                   pl.BlockSpec((tm, ns), lambda i, j: (i, 0))],
        out_shape=[jax.ShapeDtypeStruct((t, n), BF16), jax.ShapeDtypeStruct((t, ns), F32)],
        scratch_shapes=[pltpu.VMEM((tm, d), BF16)],
        compiler_params=_params("parallel", "arbitrary"),
        name="norm_matmul",
    )(x, g.reshape(1, d), w, w_side)


ATT_ONES = 16
ATT_UNROLL = 8
LOG2E = 1.4426950408889634


def _attn_kernel(items_ref, lam_ref, q_ref, k_ref, v_ref, sw_ref, o_ref,
                 vt_ref, q2_ref, m_ref, acc_ref, sa_ref, ca_ref, sb_ref, cb_ref, *, tb, n_off, n_diag, out_scale):
    hw = 2 * ATT_DH
    nb = vt_ref.shape[0]

    lane = lax.broadcasted_iota(jnp.int32, (tb, hw), 1)
    for c in range(nb):
        rows = slice(c * tb, (c + 1) * tb)
        vt_ref[c, 0:hw, :] = v_ref[rows, :].astype(F32).T.astype(BF16)
        vt_ref[c, hw:hw + ATT_ONES, :] = jnp.ones((ATT_ONES, tb), BF16)
        q = (q_ref[rows, :].astype(F32) * (ATT_DH ** -0.5 * LOG2E)).astype(BF16)
        zero = jnp.zeros_like(q)
        q2_ref[c, 0:tb, :] = jnp.where(lane < ATT_DH, q, zero)
        q2_ref[c, tb:2 * tb, :] = jnp.where(lane >= ATT_DH, q, zero)
        m_ref[c] = jnp.full(m_ref.shape[1:], -jnp.inf, F32)
        acc_ref[c] = jnp.zeros(acc_ref.shape[1:], F32)

    def fill(buf, t, masked):
        s_ref, cmax_ref = buf
        qi, j = items_ref[0, t], items_ref[1, t]
        k = k_ref[pl.ds(pl.multiple_of(j * tb, tb), tb), :]
        st = lax.dot_general(k, q2_ref[qi], (((1,), (1,)), ((), ())), preferred_element_type=F32)
        if masked:
            r = lax.broadcasted_iota(jnp.int32, st.shape, 0)
            c = lax.broadcasted_iota(jnp.int32, st.shape, 1)
            st = jnp.where(r <= jnp.where(c >= tb, c - tb, c), st, -jnp.inf)
        s_ref[...] = st
        cmax_ref[...] = jnp.max(st, axis=0, keepdims=True)

    def update(buf, t, last):
        s_ref, cmax_ref = buf
        qi, j = items_ref[0, t], items_ref[1, t]
        m_old = m_ref[qi]
        m_new = jnp.maximum(m_old, cmax_ref[...])
        p = jnp.exp2(s_ref[...] - m_new).astype(BF16)
        acc = jnp.exp2(m_old - m_new) * acc_ref[qi] + jnp.dot(vt_ref[j], p, preferred_element_type=F32)
        if not last:
            acc_ref[qi] = acc
            m_ref[qi] = m_new
        else:
            o_t = acc[:hw] * (1.0 / acc[hw:hw + 1])
            o = (o_t[:, :tb] - lam_ref[0] * o_t[:, tb:]).T
            o_ref[pl.ds(pl.multiple_of(qi * tb, tb), tb), :] = (
                _rms(o, 1e-5) * (sw_ref[...] * out_scale)).astype(o_ref.dtype)

    def run(first, count, diag, cur, other):
        trips = (count - 1) // ATT_UNROLL
        bufs = (cur, other)

        def body(i, carry):
            t = first + ATT_UNROLL * i
            for u in range(ATT_UNROLL):
                fill(bufs[(u + 1) % 2], t + u + 1, diag)
                update(bufs[u % 2], t + u, diag)
            return carry

        if trips:
            lax.fori_loop(0, trips, body, 0)
        for t in range(first + ATT_UNROLL * trips, first + count - 1):
            fill(other, t + 1, diag)
            update(cur, t, diag)
            cur, other = other, cur
        return cur, other

    cur, other = (sa_ref, ca_ref), (sb_ref, cb_ref)
    if n_off:
        fill(cur, 0, False)
        cur, other = run(0, n_off, False, cur, other)
        fill(other, n_off, True)
        update(cur, n_off - 1, False)
        cur, other = other, cur
    else:
        fill(cur, 0, True)
    cur, other = run(n_off, n_diag, True, cur, other)
    update(cur, n_off + n_diag - 1, True)


def _diff_attention(proj3, lam, subln_w, lam_init, tb):
    b, s, _ = proj3.shape
    hw = 2 * ATT_DH
    nb = s // tb
    off = [(qi, j) for qi in range(nb) for j in range(qi)]
    items = jnp.asarray(list(zip(*(off + [(qi, qi) for qi in range(nb)]))), jnp.int32)
    kern = functools.partial(_attn_kernel, tb=tb, n_off=len(off), n_diag=nb, out_scale=1.0 - lam_init)
    seq = lambda col: pl.BlockSpec((None, s, hw), lambda bb, h: (bb, 0, col // hw + h))
    return pl.pallas_call(
        kern,
        grid=(b, ATT_HEADS),
        in_specs=[pl.BlockSpec(memory_space=pltpu.SMEM), pl.BlockSpec(memory_space=pltpu.SMEM),
                  seq(COL_Q), seq(COL_K), seq(COL_V), pl.BlockSpec((1, hw), lambda bb, h: (0, 0))],
        out_specs=pl.BlockSpec((None, s, hw), lambda bb, h: (bb, 0, h)),
        out_shape=jax.ShapeDtypeStruct((b, s, ATT_WIDTH), BF16),
        scratch_shapes=[pltpu.VMEM((nb, hw + ATT_ONES, tb), BF16), pltpu.VMEM((nb, 2 * tb, hw), BF16),
                        pltpu.VMEM((nb, 1, 2 * tb), F32), pltpu.VMEM((nb, hw + ATT_ONES, 2 * tb), F32),
                        pltpu.VMEM((tb, 2 * tb), F32), pltpu.VMEM((1, 2 * tb), F32),
                        pltpu.VMEM((tb, 2 * tb), F32), pltpu.VMEM((1, 2 * tb), F32)],
        compiler_params=_params("parallel", "parallel"),
        name="diff_attention",
    )(items, lam.reshape(1).astype(F32), proj3, proj3, proj3, subln_w.reshape(1, hw).astype(F32))


CONV_HALO = 32
CONV_ROWS = 64


def _cconv_kernel(a_ref, g_ref, w_ref, b_ref, lnw_ref, lnb_ref, o_ref, buf_ref, *, ts):
    i = pl.program_id(1)

    @pl.when(i == 0)
    def _():
        buf_ref[0:CONV_HALO, :] = jnp.zeros((CONV_HALO, CONV_CH), F32)

    @pl.when(i > 0)
    def _():
        buf_ref[0:CONV_HALO, :] = buf_ref[ts:ts + CONV_HALO, :]

    buf_ref[CONV_HALO:CONV_HALO + ts, :] = a_ref[...].astype(F32) * _sigmoid(g_ref[...].astype(F32))

    first = CONV_HALO - (CONV_K - 1)
    for c in range(ts // CONV_ROWS):
        base = c * CONV_ROWS
        acc = jnp.zeros((CONV_ROWS, CONV_CH), F32) + b_ref[...]
        for res in range(SUBLANES):
            taps = [k for k in range(CONV_K) if (first + k) % SUBLANES == res]
            rows = CONV_ROWS + (SUBLANES if res else 0)
            part = None
            for k in taps:
                lo = base + first + k - res
                term = w_ref[k:k + 1, :] * buf_ref[lo:lo + rows, :]
                part = term if part is None else part + term
            acc = acc + part[res:res + CONV_ROWS]
        mu = jnp.mean(acc, axis=-1, keepdims=True)
        xc = acc - mu
        var = jnp.mean(xc * xc, axis=-1, keepdims=True)
        y = xc * lax.rsqrt(var + 1e-5) * lnw_ref[...] + lnb_ref[...]
        o_ref[base:base + CONV_ROWS, :] = (y * _sigmoid(y)).astype(o_ref.dtype)


def _conformer_conv(proj3, dw_w, dw_b, ln_w, ln_b, ts):
    b, s, _ = proj3.shape
    c = CONV_CH
    row = lambda a: a.reshape(1, c).astype(F32)
    return pl.pallas_call(
        functools.partial(_cconv_kernel, ts=ts),
        grid=(b, s // ts),
        in_specs=[pl.BlockSpec((None, ts, c), lambda bb, i: (bb, i, COL_U // c)),
                  pl.BlockSpec((None, ts, c), lambda bb, i: (bb, i, COL_U // c + 1)),
                  pl.BlockSpec((CONV_K, c), lambda bb, i: (0, 0)),
                  pl.BlockSpec((1, c), lambda bb, i: (0, 0)),
                  pl.BlockSpec((1, c), lambda bb, i: (0, 0)),
                  pl.BlockSpec((1, c), lambda bb, i: (0, 0))],
        out_specs=pl.BlockSpec((None, ts, c), lambda bb, i: (bb, i, 0)),
        out_shape=jax.ShapeDtypeStruct((b, s, c), BF16),
        scratch_shapes=[pltpu.VMEM((CONV_HALO + ts, c), F32)],
        compiler_params=_params("parallel", "arbitrary"),
        name="conformer_conv",
    )(proj3, proj3, dw_w.astype(F32), row(dw_b), row(ln_w), row(ln_b))


SSD_HALO = 16


def _ssd_kernel(z_ref, xbc_ref, dt_ref, cw_ref, cb_ref, dtb_ref, a_ref, dsk_ref, nw_ref, e_ref,
                o_ref, buf_ref, h_ref, *, chunk):
    L = chunk
    ci = pl.program_id(1)
    rows = range(z_ref.shape[0])

    @pl.when(ci == 0)
    def _():
        for bb in rows:
            buf_ref[bb, 0:SSD_HALO, :] = jnp.zeros((SSD_HALO, SSM_CONV_DIM), BF16)
        h_ref[...] = jnp.zeros(h_ref.shape, F32)

    @pl.when(ci > 0)
    def _():
        for bb in rows:
            buf_ref[bb, 0:SSD_HALO, :] = buf_ref[bb, L:L + SSD_HALO, :]

    for bb in rows:
        _ssd_chunk(z_ref.at[bb], xbc_ref.at[bb], dt_ref.at[bb], cw_ref, cb_ref, dtb_ref, a_ref, dsk_ref, nw_ref,
                   e_ref, o_ref.at[bb], buf_ref.at[bb], h_ref.at[bb], L)


def _ssd_chunk(z_ref, xbc_ref, dt_ref, cw_ref, cb_ref, dtb_ref, a_ref, dsk_ref, nw_ref, e_ref,
               o_ref, buf_ref, h_ref, L):
    gw = SSM_D_INNER // SSM_GROUPS
    hpg = SSM_HEADS // SSM_GROUPS

    buf_ref[SSD_HALO:SSD_HALO + L, :] = xbc_ref[...]
    xraw = buf_ref[...]
    r = lax.broadcasted_iota(jnp.int32, (L, SSD_HALO + L), 0)
    c = lax.broadcasted_iota(jnp.int32, (L, SSD_HALO + L), 1)
    xc = jnp.zeros((L, SSM_CONV_DIM), F32) + cb_ref[...]
    for k in range(SSM_CONV_K):
        shift = jnp.where(c == r + (SSD_HALO - (SSM_CONV_K - 1) + k), 1.0, 0.0).astype(BF16)
        xc = xc + cw_ref[k:k + 1, :] * jnp.dot(shift, xraw, preferred_element_type=F32)
    xc = xc * _sigmoid(xc)
    xs = xc[:, :SSM_D_INNER]
    bm = xc[:, SSM_D_INNER:SSM_D_INNER + SSM_GROUPS * SSM_STATE]
    cm = xc[:, SSM_D_INNER + SSM_GROUPS * SSM_STATE:]

    dtr = dt_ref[...] + dtb_ref[...]
    dt = jnp.maximum(dtr, 0.0) + jnp.log(1.0 + jnp.exp(-jnp.abs(dtr)))
    da = dt * a_ref[...]
    row = lax.broadcasted_iota(jnp.int32, (L, L), 0)
    col = lax.broadcasted_iota(jnp.int32, (L, L), 1)
    causal = col <= row
    a_cum = sum(jnp.dot(jnp.where(causal, 1.0, 0.0).astype(BF16), part, preferred_element_type=F32)
                for part in _split3(da))
    a_cum_t = a_cum.T

    expand = e_ref[...]
    per_head = jnp.concatenate([dt, jnp.exp(a_cum), jnp.exp(a_cum[L - 1:L, :] - a_cum)], axis=0)
    full = sum(jnp.dot(part, expand, preferred_element_type=F32) for part in _split3(per_head))
    dt_full, ea_full, dend_full = full[:L], full[L:2 * L], full[2 * L:]
    cdec_full = ea_full[L - 1:L, :]

    xdt = xs * dt_full
    xdt_b = xdt.astype(BF16)
    xw_b = (xdt * dend_full).astype(BF16)

    y_parts = []
    for g in range(SSM_GROUPS):
        cg = cm[:, g * SSM_STATE:(g + 1) * SSM_STATE].astype(BF16)
        bg_t = bm[:, g * SSM_STATE:(g + 1) * SSM_STATE].T.astype(BF16)
        cbm = jnp.dot(cg, bg_t, preferred_element_type=F32)
        h_in = h_ref[:, g * gw:(g + 1) * gw]
        y_off = jnp.dot(cg, h_in.astype(BF16), preferred_element_type=F32)
        states = jnp.dot(bg_t, xw_b[:, g * gw:(g + 1) * gw], preferred_element_type=F32)
        h_ref[:, g * gw:(g + 1) * gw] = h_in * cdec_full[:, g * gw:(g + 1) * gw] + states
        yd = []
        for r in range(hpg):
            hh = g * hpg + r
            seg = a_cum[:, hh:hh + 1] - a_cum_t[hh:hh + 1, :]
            dec = jnp.exp(jnp.where(causal, seg, -jnp.inf))
            yd.append(jnp.dot((cbm * dec).astype(BF16), xdt_b[:, hh * SSM_HEADDIM:(hh + 1) * SSM_HEADDIM],
                              preferred_element_type=F32))
        y_g = (jnp.concatenate(yd, axis=-1) + y_off * ea_full[:, g * gw:(g + 1) * gw]
               + xs[:, g * gw:(g + 1) * gw] * dsk_ref[:, g * gw:(g + 1) * gw])
        zg = z_ref[:, g * gw:(g + 1) * gw].astype(F32)
        y_parts.append(_rms(y_g * (zg * _sigmoid(zg)), 1e-5))
    o_ref[...] = (jnp.concatenate(y_parts, axis=-1) * nw_ref[...]).astype(o_ref.dtype)


def _mamba2_ssd(proj3, dt3, conv_w, conv_b, dt_bias, a_log, d_skip, norm_w):
    b, s, _ = proj3.shape
    L = SSM_CHUNK
    di, cd = SSM_D_INNER, SSM_CONV_DIM
    pad_h = lambda a: jnp.pad(a.astype(F32), (0, LANES - SSM_HEADS)).reshape(1, LANES)
    head_of_channel = jnp.arange(di, dtype=jnp.int32) // SSM_HEADDIM
    expand = (jnp.arange(LANES, dtype=jnp.int32)[:, None] == head_of_channel[None, :]).astype(BF16)
    const = lambda shape: pl.BlockSpec(shape, lambda bb, c: (0, 0))
    nb = next(n for n in (8, 4, 2, 1) if b % n == 0)
    return pl.pallas_call(
        functools.partial(_ssd_kernel, chunk=L),
        grid=(b // nb, s // L),
        in_specs=[pl.BlockSpec((nb, L, di), lambda bb, c: (bb, c, COL_Z // di)),
                  pl.BlockSpec((nb, L, cd), lambda bb, c: (bb, c, COL_XBC // cd)),
                  pl.BlockSpec((nb, L, LANES), lambda bb, c: (bb, c, 0)),
                  const((SSM_CONV_K, cd)), const((1, cd)), const((1, LANES)), const((1, LANES)),
                  const((1, di)), const((1, di)), const((LANES, di))],
        out_specs=pl.BlockSpec((nb, L, di), lambda bb, c: (bb, c, 0)),
        out_shape=jax.ShapeDtypeStruct((b, s, di), BF16),
        scratch_shapes=[pltpu.VMEM((nb, SSD_HALO + L, cd), BF16), pltpu.VMEM((nb, SSM_STATE, di), F32)],
        compiler_params=_params("parallel", "arbitrary"),
        name="mamba2_ssd",
    )(proj3, proj3, dt3, conv_w.astype(F32), conv_b.reshape(1, cd).astype(F32), pad_h(dt_bias),
      pad_h(-jnp.exp(a_log.astype(F32))), jnp.repeat(d_skip.astype(F32), SSM_HEADDIM).reshape(1, di),
      norm_w.reshape(1, di).astype(F32), expand)


def _merge_kernel(x_ref, ya_ref, yc_ref, ys_ref, ga_ref, gc_ref, gs_ref, wa_ref, wc_ref, ws_ref, wm_ref, o_ref):
    def branch(y_ref, g_ref, w_ref):
        return _sigmoid(g_ref[...].astype(F32)) * jnp.dot(y_ref[...], w_ref[...], preferred_element_type=F32)

    merged = branch(ya_ref, ga_ref, wa_ref) + branch(yc_ref, gc_ref, wc_ref) + branch(ys_ref, gs_ref, ws_ref)
    o_ref[...] = x_ref[...] + jnp.dot(merged.astype(BF16), wm_ref[...], preferred_element_type=F32)


def _merge(x, ya, yc, ys, proj, wa, wc, ws, wm, tm):
    t, d = x.shape
    w = ya.shape[1]
    rows = lambda width, cb: pl.BlockSpec((tm, width), lambda i: (i, cb))
    const = lambda shape: pl.BlockSpec(shape, lambda i: (0, 0))
    return pl.pallas_call(
        _merge_kernel,
        grid=(t // tm,),
        in_specs=[rows(d, 0), rows(w, 0), rows(w, 0), rows(w, 0),
                  rows(d, COL_G // d), rows(d, COL_G // d + 1), rows(d, COL_G // d + 2),
                  const((w, d)), const((w, d)), const((w, d)), const((d, d))],
        out_specs=rows(d, 0),
        out_shape=jax.ShapeDtypeStruct((t, d), F32),
        compiler_params=_params("parallel"),
        name="gated_merge",
    )(x, ya, yc, ys, proj, proj, proj, wa, wc, ws, wm)


def _ffn_kernel(x_ref, g_ref, wg_ref, wu_ref, wd_ref, o_ref):
    x = x_ref[...]
    h = (_rms(x, NORM_EPS) * g_ref[...]).astype(BF16)
    a = jnp.dot(h, wg_ref[...], preferred_element_type=F32)
    u = jnp.dot(h, wu_ref[...], preferred_element_type=F32)
    o_ref[...] = x + jnp.dot((a * _sigmoid(a) * u).astype(BF16), wd_ref[...], preferred_element_type=F32)


def _ffn_dense(x, g, wg, wu, wd, tm):
    t, d = x.shape
    f = wg.shape[1]
    resident = lambda shape: pl.BlockSpec(shape, lambda i: (0, 0), pipeline_mode=pl.Buffered(1))
    return pl.pallas_call(
        _ffn_kernel,
        grid=(t // tm,),
        in_specs=[pl.BlockSpec((tm, d), lambda i: (i, 0)),
                  pl.BlockSpec((1, d), lambda i: (0, 0)),
                  resident((d, f)), resident((d, f)), resident((f, d))],
        out_specs=pl.BlockSpec((tm, d), lambda i: (i, 0)),
        out_shape=jax.ShapeDtypeStruct((t, d), F32),
        compiler_params=_params("parallel"),
        name="ffn_dense",
    )(x, g.reshape(1, d), wg, wu, wd)


def _router_kernel(x_ref, g_ref, wr_ref, br_ref, h_ref, idx_ref, wt_ref, rank_ref, cnt_ref, tri_ref, base_ref):
    i = pl.program_id(0)
    tm = x_ref.shape[0]

    @pl.when(i == 0)
    def _():
        r = lax.broadcasted_iota(jnp.int32, (tm, tm), 0)
        c = lax.broadcasted_iota(jnp.int32, (tm, tm), 1)
        tri_ref[...] = jnp.where(c < r, 1.0, 0.0).astype(BF16)
        base_ref[...] = jnp.zeros(base_ref.shape, F32)

    h = _rms(x_ref[...], NORM_EPS) * g_ref[...]
    h_ref[...] = _pack_halves(h)
    logits = jnp.dot(h, wr_ref[...], precision=HIGHEST, preferred_element_type=F32) + br_ref[...]
    lane = lax.broadcasted_iota(jnp.int32, logits.shape, 1)
    logits = jnp.where(lane < N_EXPERTS, logits, -jnp.inf)
    m1 = jnp.max(logits, axis=-1, keepdims=True)
    i1 = jnp.min(jnp.where(logits == m1, lane, LANES), axis=-1, keepdims=True)
    rest = jnp.where(lane == i1, -jnp.inf, logits)
    m2 = jnp.max(rest, axis=-1, keepdims=True)
    i2 = jnp.min(jnp.where(rest == m2, lane, LANES), axis=-1, keepdims=True)
    e = jnp.exp(m2 - m1)
    w1 = 1.0 / (1.0 + e)
    idx_ref[...] = jnp.where(lane == 0, i1, i2)[:, :TOP_K]
    wt_ref[...] = jnp.where(lane == 0, w1, e * w1)[:, :TOP_K]

    chosen = jnp.logical_or(lane == i1, lane == i2)
    before = jnp.dot(tri_ref[...], jnp.where(chosen, 1.0, 0.0).astype(BF16), preferred_element_type=F32)
    before = before + base_ref[...]
    r1 = jnp.sum(jnp.where(lane == i1, before, 0.0), axis=-1, keepdims=True)
    r2 = jnp.sum(jnp.where(lane == i2, before, 0.0), axis=-1, keepdims=True)
    rank_ref[...] = jnp.where(lane == 0, r1, r2)[:, :TOP_K].astype(jnp.int32)
    base_ref[...] += jnp.sum(jnp.where(chosen, 1.0, 0.0), axis=0, keepdims=True)
    cnt_ref[...] = base_ref[...].astype(jnp.int32)


def _router(x, g, w_router, b_router, tm):
    t, d = x.shape
    wr = jnp.pad(w_router.astype(F32), ((0, 0), (0, LANES - N_EXPERTS)))
    br = jnp.pad(b_router.astype(F32), (0, LANES - N_EXPERTS)).reshape(1, LANES)
    pair = pl.BlockSpec((tm, TOP_K), lambda i: (i, 0))
    return pl.pallas_call(
        _router_kernel,
        grid=(t // tm,),
        in_specs=[pl.BlockSpec((tm, d), lambda i: (i, 0)),
                  pl.BlockSpec((1, d), lambda i: (0, 0)),
                  pl.BlockSpec((d, LANES), lambda i: (0, 0)),
                  pl.BlockSpec((1, LANES), lambda i: (0, 0))],
        out_specs=[pl.BlockSpec((tm, d // 2), lambda i: (i, 0)), pair, pair, pair,
                   pl.BlockSpec((1, LANES), lambda i: (0, 0))],
        out_shape=[jax.ShapeDtypeStruct((t, d // 2), jnp.uint32),
                   jax.ShapeDtypeStruct((t, TOP_K), jnp.int32),
                   jax.ShapeDtypeStruct((t, TOP_K), F32),
                   jax.ShapeDtypeStruct((t, TOP_K), jnp.int32),
                   jax.ShapeDtypeStruct((1, LANES), jnp.int32)],
        scratch_shapes=[pltpu.VMEM((tm, tm), BF16), pltpu.VMEM((1, LANES), F32)],
        compiler_params=_params("arbitrary"),
        name="moe_router",
    )(x, g.reshape(1, d), wr, br)


def _expert_kernel(be_ref, nv_ref, x_ref, wg_ref, wu_ref, wd_ref, o_ref, xb_ref, acc_ref, *, nj):
    i = pl.program_id(0)
    j = pl.program_id(1)
    last = nj - 1
    used = nv_ref[i] > 0

    def partial_ffn():
        x = xb_ref[...]
        a = jnp.dot(x, wg_ref[...], preferred_element_type=F32)
        u = jnp.dot(x, wu_ref[...], preferred_element_type=F32)
        return jnp.dot((a * _sigmoid(a) * u).astype(BF16), wd_ref[...], preferred_element_type=F32)

    @pl.when(jnp.logical_and(used, j == 0))
    def _():
        x = _unpack_halves(x_ref[...])
        row = lax.broadcasted_iota(jnp.int32, x.shape, 0)
        xb_ref[...] = jnp.where(row < nv_ref[i], x, 0.0).astype(BF16)
        if nj == 1:
            o_ref[...] = _pack_halves(partial_ffn())
        else:
            acc_ref[...] = partial_ffn()

    if nj > 2:
        @pl.when(jnp.logical_and(used, jnp.logical_and(j > 0, j < last)))
        def _():
            acc_ref[...] += partial_ffn()

    if nj > 1:
        @pl.when(jnp.logical_and(used, j == last))
        def _():
            o_ref[...] = _pack_halves(acc_ref[...] + partial_ffn())

    @pl.when(jnp.logical_and(jnp.logical_not(used), j == last))
    def _():
        o_ref[...] = jnp.zeros(o_ref.shape, o_ref.dtype)


def _expert_ffn(xs, block_exp, n_valid, wg, wu, wd, bm, tf):
    n_rows, half = xs.shape
    d = 2 * half
    f = wg.shape[2]
    nj = f // tf

    def jj(i, j, nv):
        return jnp.where(nv[i] > 0, j, nj - 1)

    grid_spec = pltpu.PrefetchScalarGridSpec(
        num_scalar_prefetch=2,
        grid=(n_rows // bm, nj),
        in_specs=[pl.BlockSpec((bm, half), lambda i, j, be, nu: (i, 0)),
                  pl.BlockSpec((None, d, tf), lambda i, j, be, nu: (be[i], 0, jj(i, j, nu))),
                  pl.BlockSpec((None, d, tf), lambda i, j, be, nu: (be[i], 0, jj(i, j, nu))),
                  pl.BlockSpec((None, tf, d), lambda i, j, be, nu: (be[i], jj(i, j, nu), 0))],
        out_specs=pl.BlockSpec((bm, half), lambda i, j, be, nu: (i, 0)),
        scratch_shapes=[pltpu.VMEM((bm, d), BF16), pltpu.VMEM((bm, d), F32)],
    )
    return pl.pallas_call(
        functools.partial(_expert_kernel, nj=nj),
        grid_spec=grid_spec,
        out_shape=jax.ShapeDtypeStruct((n_rows, half), jnp.uint32),
        compiler_params=_params("parallel", "arbitrary"),
        name="moe_experts",
    )(block_exp, n_valid, xs, wg, wu, wd)


def _route(top_idx, rank, counts, bm):
    t = top_idx.shape[0]
    padded = (counts + bm - 1) // bm * bm
    end_padded = jnp.cumsum(padded)
    start_padded = end_padded - padded
    experts = jnp.arange(N_EXPERTS, dtype=jnp.int32)
    dest = rank + jnp.sum(jnp.where(top_idx[..., None] == experts, start_padded, 0), axis=-1)
    n_blocks = t * TOP_K // bm + N_EXPERTS
    block_start = jnp.arange(n_blocks, dtype=jnp.int32) * bm
    block_exp = jnp.minimum(jnp.sum(block_start[:, None] >= end_padded[None, :], axis=1), N_EXPERTS - 1)
    n_valid = jnp.clip((start_padded + counts)[block_exp] - block_start, 0, bm)
    return dest, block_exp.astype(jnp.int32), n_valid.astype(jnp.int32)


def _combine_kernel(x_ref, a_ref, b_ref, w_ref, g_ref, o_ref, *, final_norm):
    w = w_ref[...]
    y = x_ref[...] + w[:, 0:1] * _unpack_halves(a_ref[...]) + w[:, 1:2] * _unpack_halves(b_ref[...])
    o_ref[...] = _rms(y, NORM_EPS) * g_ref[...] if final_norm else y


def _combine(x, ab, w, g, tm):
    t, d = x.shape
    spec = pl.BlockSpec((tm, d), lambda i: (i, 0))
    final_norm = g is not None
    gain = g.reshape(1, d) if final_norm else jnp.ones((1, d), F32)
    return pl.pallas_call(
        functools.partial(_combine_kernel, final_norm=final_norm), grid=(t // tm,),
        in_specs=[spec, pl.BlockSpec((tm, d // 2), lambda i: (i, 0)),
                  pl.BlockSpec((tm, d // 2), lambda i: (i + t // tm, 0)),
                  pl.BlockSpec((tm, TOP_K), lambda i: (i, 0)),
                  pl.BlockSpec((1, d), lambda i: (0, 0))],
        out_specs=spec,
        out_shape=jax.ShapeDtypeStruct((t, d), F32), compiler_params=_params("parallel"),
        name="moe_combine",
    )(x, ab, ab, w, gain)


def _moe(x, g, w_router, b_router, wg, wu, wd, final_g, tm, bm, tf):
    h, top_idx, top_w, rank, counts = _router(x, g, w_router, b_router, tm)
    dest, block_exp, n_valid = _route(top_idx, rank, counts[0, :N_EXPERTS], bm)
    dest = dest.T
    xs = _scatter_rows(h, dest, block_exp.shape[0] * bm)
    ys = _expert_ffn(xs, block_exp, n_valid, wg, wu, wd, bm, tf)
    picked = _gather_rows(ys, dest.reshape(-1))
    return _combine(x, picked, top_w, final_g, tm)


def _cast_kernel(w_ref, o_ref):
    o_ref[...] = w_ref[...].astype(o_ref.dtype)


def _to_bf16(w, axis, tile):
    e, r, c = w.shape
    block = (None, tile, c) if axis == 0 else (None, r, tile)
    index = (lambda k, j: (k, j, 0)) if axis == 0 else (lambda k, j: (k, 0, j))
    return pl.pallas_call(
        _cast_kernel, grid=(e, (r, c)[axis] // tile),
        in_specs=[pl.BlockSpec(block, index)], out_specs=pl.BlockSpec(block, index),
        out_shape=jax.ShapeDtypeStruct(w.shape, BF16), compiler_params=_params("parallel", "parallel"),
        name="expert_weight_cast",
    )(w)


def _final_norm_kernel(x_ref, g_ref, o_ref):
    o_ref[...] = _rms(x_ref[...], NORM_EPS) * g_ref[...]


def _final_norm(x, g, tm):
    t, d = x.shape
    return pl.pallas_call(
        _final_norm_kernel, grid=(t // tm,),
        in_specs=[pl.BlockSpec((tm, d), lambda i: (i, 0)), pl.BlockSpec((1, d), lambda i: (0, 0))],
        out_specs=pl.BlockSpec((tm, d), lambda i: (i, 0)),
        out_shape=jax.ShapeDtypeStruct((t, d), F32), compiler_params=_params("parallel"),
        name="final_norm",
    )(x, g.reshape(1, d))


def _tiles(t, s):
    pick = lambda n, pref: pref if n % pref == 0 else n
    return dict(tm_proj=pick(t, 1024), tn_proj=PROJ_COLS // 4, tb_attn=pick(s, 512), ts_conv=pick(s, 512),
                tm_merge=pick(t, 512), tm_ffn=pick(t, 512), tm_row=pick(t, 1024), bm_moe=pick(t, 512))


def kernel(x, ln_mix_w, w_in, attn_lq1, attn_lk1, attn_lq2, attn_lk2, attn_subln_w, attn_w_o, conv_dw_w, conv_dw_b, conv_ln_w, conv_ln_b, conv_w_pw, ssm_conv_w, ssm_conv_b, ssm_dt_bias, ssm_A_log, ssm_D, ssm_norm_w, ssm_w_out, w_mix_out, ln_ffn_w, ffn_w_gate, ffn_w_up, ffn_w_down, moe_w_router, moe_b_router, moe_w_gate, moe_w_up, moe_w_down, ln_final_w):
    b, s, d = x.shape
    t = b * s
    depth = w_in.shape[0]
    tl = _tiles(t, s)
    xf = x.reshape(t, d)
    for l in range(depth):
        w_main = jnp.concatenate([w_in[l][:, :COL_DT], w_in[l][:, COL_DT + SSM_HEADS:]], axis=1).astype(BF16)
        w_dt = jnp.pad(w_in[l][:, COL_DT:COL_DT + SSM_HEADS], ((0, 0), (0, LANES - SSM_HEADS))).astype(BF16)
        proj, dt = _norm_matmul(xf, ln_mix_w[l], w_main, w_dt, tl["tm_proj"], tl["tn_proj"])
        proj3 = proj.reshape(b, s, PROJ_COLS)

        lam_init = 0.8 - 0.6 * math.exp(-0.3 * l)
        lam = (jnp.exp(jnp.sum(attn_lq1[l].astype(F32) * attn_lk1[l].astype(F32)))
               - jnp.exp(jnp.sum(attn_lq2[l].astype(F32) * attn_lk2[l].astype(F32))) + lam_init)
        y_att = _diff_attention(proj3, lam, attn_subln_w[l], lam_init, tl["tb_attn"])
        y_conv = _conformer_conv(proj3, conv_dw_w[l], conv_dw_b[l], conv_ln_w[l], conv_ln_b[l], tl["ts_conv"])
        y_ssm = _mamba2_ssd(proj3, dt.reshape(b, s, LANES), ssm_conv_w[l], ssm_conv_b[l], ssm_dt_bias[l],
                            ssm_A_log[l], ssm_D[l], ssm_norm_w[l])
        xf = _merge(xf, y_att.reshape(t, ATT_WIDTH), y_conv.reshape(t, CONV_CH), y_ssm.reshape(t, SSM_D_INNER),
                    proj, attn_w_o[l].astype(BF16), conv_w_pw[l].astype(BF16), ssm_w_out[l].astype(BF16),
                    w_mix_out[l].astype(BF16), tl["tm_merge"])

        i = l // 2
        if l % 2 == 0:
            xf = _ffn_dense(xf, ln_ffn_w[l], ffn_w_gate[i].astype(BF16), ffn_w_up[i].astype(BF16),
                            ffn_w_down[i].astype(BF16), tl["tm_ffn"])
        else:
            f = moe_w_gate.shape[3]
            tf = f // 2 if (f // 2) % (2 * LANES) == 0 else f
            final_g = ln_final_w if l == depth - 1 else None
            xf = _moe(xf, ln_ffn_w[l], moe_w_router[i], moe_b_router[i], _to_bf16(moe_w_gate[i], 1, tf),
                      _to_bf16(moe_w_up[i], 1, tf), _to_bf16(moe_w_down[i], 0, tf), final_g,
                      tl["tm_row"], tl["bm_moe"], tf)
    if depth % 2 == 1:
        xf = _final_norm(xf, ln_final_w, tl["tm_row"])
    return xf.reshape(b, s, d)
```

```python
import functools
import math

import jax
import jax.numpy as jnp
from jax import lax
from jax.experimental import pallas as pl
from jax.experimental.pallas import tpu as pltpu
from jax.experimental.pallas import tpu_sc as plsc

F32 = jnp.float32
BF16 = jnp.bfloat16
HIGHEST = lax.Precision.HIGHEST

D_MODEL = 1024
ATT_HEADS = 4
ATT_DH = 64
ATT_WIDTH = ATT_HEADS * 2 * ATT_DH
CONV_CH = 512
CONV_K = 31
SSM_D_INNER = 512
SSM_HEADDIM = 64
SSM_HEADS = SSM_D_INNER // SSM_HEADDIM
SSM_GROUPS = 2
SSM_STATE = 128
SSM_CONV_K = 4
SSM_CONV_DIM = SSM_D_INNER + 2 * SSM_GROUPS * SSM_STATE
SSM_CHUNK = 128
N_EXPERTS = 8
TOP_K = 2
NORM_EPS = 1e-6

LANES = 128
SUBLANES = 8
VMEM_LIMIT_BYTES = 48 * 1024 * 1024

COL_Q = 0
COL_K = COL_Q + ATT_WIDTH
COL_V = COL_K + ATT_WIDTH
COL_U = COL_V + ATT_WIDTH
COL_Z = COL_U + 2 * CONV_CH
COL_XBC = COL_Z + SSM_D_INNER
COL_DT = COL_XBC + SSM_CONV_DIM
COL_G = COL_XBC + SSM_CONV_DIM
PROJ_COLS = COL_G + 3 * D_MODEL


def _params(*semantics):
    return pltpu.CompilerParams(dimension_semantics=semantics, vmem_limit_bytes=VMEM_LIMIT_BYTES)


def _sigmoid(x):
    return 1.0 / (1.0 + jnp.exp(-x))


def _rms(x, eps):
    return x * lax.rsqrt(jnp.mean(x * x, axis=-1, keepdims=True) + eps)


def _split3(x):
    hi = x.astype(BF16)
    rest = x - hi.astype(F32)
    mid = rest.astype(BF16)
    return hi, mid, (rest - mid.astype(F32)).astype(BF16)


def _pack_halves(x):
    n = x.shape[1] // 2
    bits = lambda v: lax.bitcast_convert_type(v.astype(BF16).astype(F32), jnp.uint32)
    return (bits(x[:, :n]) >> 16) | bits(x[:, n:])


def _unpack_halves(w):
    lo = lax.bitcast_convert_type(w << 16, F32)
    hi = lax.bitcast_convert_type(w & jnp.uint32(0xFFFF0000), F32)
    return jnp.concatenate([lo, hi], axis=1)


SC_GATHER_ROWS = 64


def _gather_rows(table, idx):
    n_idx = idx.shape[0]
    width = table.shape[1]
    info = plsc.get_sparse_core_info()
    n_workers = info.num_cores * info.num_subcores
    per_worker = n_idx // n_workers
    n_chunks = per_worker // SC_GATHER_ROWS
    assert n_idx % n_workers == 0 and per_worker % SC_GATHER_ROWS == 0 and n_chunks % 2 == 0, (n_idx, n_workers)
    mesh = plsc.VectorSubcoreMesh(core_axis_name="c", subcore_axis_name="s")
    idx_buf = pltpu.VMEM((SC_GATHER_ROWS,), jnp.int32)
    row_buf = pltpu.VMEM((SC_GATHER_ROWS, width), table.dtype)

    @functools.partial(
        pl.kernel, mesh=mesh,
        out_type=jax.ShapeDtypeStruct((n_idx, width), table.dtype),
        scratch_types=[idx_buf, idx_buf, row_buf, row_buf, pltpu.SemaphoreType.DMA, pltpu.SemaphoreType.DMA],
    )
    def gather(table_hbm, idx_hbm, out_hbm, idx0, idx1, rows0, rows1, sem0, sem1):
        worker = lax.axis_index("s") * info.num_cores + lax.axis_index("c")
        slots = ((idx0, rows0, sem0), (idx1, rows1, sem1))

        def offset(c):
            return pl.multiple_of(worker * per_worker + c * SC_GATHER_ROWS, SUBLANES)

        def stream(slot):
            idx_v, rows_v, sem = slot
            return pltpu.make_async_copy(table_hbm.at[idx_v], rows_v, sem)

        def start(c, slot):
            pltpu.sync_copy(idx_hbm.at[pl.ds(offset(c), SC_GATHER_ROWS)], slot[0])
            stream(slot).start()

        start(0, slots[0])

        @pl.loop(0, n_chunks, step=2)
        def _(c0):
            for b in range(2):
                c = c0 + b
                stream(slots[b]).wait()

                @pl.when(c + 1 < n_chunks)
                def _():
                    start(c + 1, slots[1 - b])

                pltpu.sync_copy(slots[b][1], out_hbm.at[pl.ds(offset(c), SC_GATHER_ROWS)])

    return gather(table, idx)


def _scatter_rows(rows, idx, n_out):
    n_rows, width = rows.shape
    fan = idx.shape[0]
    info = plsc.get_sparse_core_info()
    n_workers = info.num_cores * info.num_subcores
    per_worker = n_rows // n_workers
    assert n_rows % n_workers == 0 and per_worker % SC_GATHER_ROWS == 0, (n_rows, n_workers)
    mesh = plsc.VectorSubcoreMesh(core_axis_name="c", subcore_axis_name="s")
    idx_buf = pltpu.VMEM((SC_GATHER_ROWS,), jnp.int32)

    @functools.partial(
        pl.kernel, mesh=mesh,
        out_type=jax.ShapeDtypeStruct((n_out, width), rows.dtype),
        scratch_types=[idx_buf] * fan + [pltpu.VMEM((SC_GATHER_ROWS, width), rows.dtype)],
    )
    def scatter(rows_hbm, idx_hbm, out_hbm, *scratch):
        idx_vs, rows_v = scratch[:fan], scratch[fan]
        worker = lax.axis_index("s") * info.num_cores + lax.axis_index("c")

        @pl.loop(0, per_worker // SC_GATHER_ROWS)
        def _(c):
            off = pl.multiple_of(worker * per_worker + c * SC_GATHER_ROWS, SUBLANES)
            pltpu.sync_copy(rows_hbm.at[pl.ds(off, SC_GATHER_ROWS)], rows_v)
            for k in range(fan):
                pltpu.sync_copy(idx_hbm.at[k, pl.ds(off, SC_GATHER_ROWS)], idx_vs[k])
                pltpu.sync_copy(rows_v, out_hbm.at[idx_vs[k]])

    return scatter(rows, idx)


def _norm_matmul_kernel(x_ref, g_ref, w_ref, ws_ref, o_ref, os_ref, *, tn):
    h = (_rms(x_ref[...], NORM_EPS) * g_ref[...]).astype(BF16)
    for c in range(w_ref.shape[1] // tn):
        cols = slice(c * tn, (c + 1) * tn)
        o_ref[:, cols] = jnp.dot(h, w_ref[:, cols], preferred_element_type=F32).astype(o_ref.dtype)
    os_ref[...] = jnp.dot(h, ws_ref[...], preferred_element_type=F32)


def _norm_matmul(x, g, w, w_side, tm, tn):
    t, d = x.shape
    n = w.shape[1]
    ns = w_side.shape[1]
    resident = lambda shape: pl.BlockSpec(shape, lambda i: (0, 0), pipeline_mode=pl.Buffered(1))
    return pl.pallas_call(
        functools.partial(_norm_matmul_kernel, tn=tn),
        grid=(t // tm,),
        in_specs=[pl.BlockSpec((tm, d), lambda i: (i, 0)),
                  pl.BlockSpec((1, d), lambda i: (0, 0)),
                  resident((d, n)), resident((d, ns))],
        out_specs=[pl.BlockSpec((tm, n), lambda i: (i, 0)),
                   pl.BlockSpec((tm, ns), lambda i: (i, 0))],
        out_shape=[jax.ShapeDtypeStruct((t, n), BF16), jax.ShapeDtypeStruct((t, ns), F32)],
        compiler_params=_params("parallel"),
        name="norm_matmul",
    )(x, g.reshape(1, d), w, w_side)


ATT_ONES = 16
ATT_UNROLL = 8
LOG2E = 1.4426950408889634


def _attn_kernel(items_ref, lam_ref, q_ref, k_ref, v_ref, sw_ref, o_ref,
                 vt_ref, q2_ref, m_ref, acc_ref, sa_ref, ca_ref, sb_ref, cb_ref, *, tb, n_off, n_diag, out_scale):
    hw = 2 * ATT_DH
    nb = vt_ref.shape[0]

    lane = lax.broadcasted_iota(jnp.int32, (tb, hw), 1)
    for c in range(nb):
        rows = slice(c * tb, (c + 1) * tb)
        vt_ref[c, 0:hw, :] = v_ref[rows, :].astype(F32).T.astype(BF16)
        vt_ref[c, hw:hw + ATT_ONES, :] = jnp.ones((ATT_ONES, tb), BF16)
        q = (q_ref[rows, :].astype(F32) * (ATT_DH ** -0.5 * LOG2E)).astype(BF16)
        zero = jnp.zeros_like(q)
        q2_ref[c, 0:tb, :] = jnp.where(lane < ATT_DH, q, zero)
        q2_ref[c, tb:2 * tb, :] = jnp.where(lane >= ATT_DH, q, zero)
        m_ref[c] = jnp.full(m_ref.shape[1:], -jnp.inf, F32)
        acc_ref[c] = jnp.zeros(acc_ref.shape[1:], F32)

    def fill(buf, t, masked):
        s_ref, cmax_ref = buf
        qi, j = items_ref[0, t], items_ref[1, t]
        k = k_ref[pl.ds(pl.multiple_of(j * tb, tb), tb), :]
        st = lax.dot_general(k, q2_ref[qi], (((1,), (1,)), ((), ())), preferred_element_type=F32)
        if masked:
            r = lax.broadcasted_iota(jnp.int32, st.shape, 0)
            c = lax.broadcasted_iota(jnp.int32, st.shape, 1)
            st = jnp.where(r <= jnp.where(c >= tb, c - tb, c), st, -jnp.inf)
        s_ref[...] = st
        cmax_ref[...] = jnp.max(st, axis=0, keepdims=True)

    def update(buf, t, last):
        s_ref, cmax_ref = buf
        qi, j = items_ref[0, t], items_ref[1, t]
        m_old = m_ref[qi]
        m_new = jnp.maximum(m_old, cmax_ref[...])
        p = jnp.exp2(s_ref[...] - m_new).astype(BF16)
        acc = jnp.exp2(m_old - m_new) * acc_ref[qi] + jnp.dot(vt_ref[j], p, preferred_element_type=F32)
        if not last:
            acc_ref[qi] = acc
            m_ref[qi] = m_new
        else:
            o_t = acc[:hw] * (1.0 / acc[hw:hw + 1])
            o = (o_t[:, :tb] - lam_ref[0] * o_t[:, tb:]).T
            o_ref[pl.ds(pl.multiple_of(qi * tb, tb), tb), :] = (
                _rms(o, 1e-5) * (sw_ref[...] * out_scale)).astype(o_ref.dtype)

    def run(first, count, diag, cur, other):
        trips = (count - 1) // ATT_UNROLL
        bufs = (cur, other)

        def body(i, carry):
            t = first + ATT_UNROLL * i
            for u in range(ATT_UNROLL):
                fill(bufs[(u + 1) % 2], t + u + 1, diag)
                update(bufs[u % 2], t + u, diag)
            return carry

        if trips:
            lax.fori_loop(0, trips, body, 0)
        for t in range(first + ATT_UNROLL * trips, first + count - 1):
            fill(other, t + 1, diag)
            update(cur, t, diag)
            cur, other = other, cur
        return cur, other

    cur, other = (sa_ref, ca_ref), (sb_ref, cb_ref)
    if n_off:
        fill(cur, 0, False)
        cur, other = run(0, n_off, False, cur, other)
        fill(other, n_off, True)
        update(cur, n_off - 1, False)
        cur, other = other, cur
    else:
        fill(cur, 0, True)
    cur, other = run(n_off, n_diag, True, cur, other)
    update(cur, n_off + n_diag - 1, True)


def _diff_attention(proj3, lam, subln_w, lam_init, tb):
    b, s, _ = proj3.shape
    hw = 2 * ATT_DH
    nb = s // tb
    off = [(qi, j) for qi in range(nb) for j in range(qi)]
    items = jnp.asarray(list(zip(*(off + [(qi, qi) for qi in range(nb)]))), jnp.int32)
    kern = functools.partial(_attn_kernel, tb=tb, n_off=len(off), n_diag=nb, out_scale=1.0 - lam_init)
    seq = lambda col: pl.BlockSpec((None, s, hw), lambda bb, h: (bb, 0, col // hw + h))
    return pl.pallas_call(
        kern,
        grid=(b, ATT_HEADS),
        in_specs=[pl.BlockSpec(memory_space=pltpu.SMEM), pl.BlockSpec(memory_space=pltpu.SMEM),
                  seq(COL_Q), seq(COL_K), seq(COL_V), pl.BlockSpec((1, hw), lambda bb, h: (0, 0))],
        out_specs=pl.BlockSpec((None, s, hw), lambda bb, h: (bb, 0, h)),
        out_shape=jax.ShapeDtypeStruct((b, s, ATT_WIDTH), BF16),
        scratch_shapes=[pltpu.VMEM((nb, hw + ATT_ONES, tb), BF16), pltpu.VMEM((nb, 2 * tb, hw), BF16),
                        pltpu.VMEM((nb, 1, 2 * tb), F32), pltpu.VMEM((nb, hw + ATT_ONES, 2 * tb), F32),
                        pltpu.VMEM((tb, 2 * tb), F32), pltpu.VMEM((1, 2 * tb), F32),
                        pltpu.VMEM((tb, 2 * tb), F32), pltpu.VMEM((1, 2 * tb), F32)],
        compiler_params=_params("parallel", "parallel"),
        name="diff_attention",
    )(items, lam.reshape(1).astype(F32), proj3, proj3, proj3, subln_w.reshape(1, hw).astype(F32))


CONV_HALO = 32
CONV_ROWS = 64


def _cconv_kernel(a_ref, g_ref, w_ref, b_ref, lnw_ref, lnb_ref, o_ref, buf_ref, *, ts):
    i = pl.program_id(1)

    @pl.when(i == 0)
    def _():
        buf_ref[0:CONV_HALO, :] = jnp.zeros((CONV_HALO, CONV_CH), F32)

    @pl.when(i > 0)
    def _():
        buf_ref[0:CONV_HALO, :] = buf_ref[ts:ts + CONV_HALO, :]

    buf_ref[CONV_HALO:CONV_HALO + ts, :] = a_ref[...].astype(F32) * _sigmoid(g_ref[...].astype(F32))

    first = CONV_HALO - (CONV_K - 1)
    for c in range(ts // CONV_ROWS):
        base = c * CONV_ROWS
        acc = jnp.zeros((CONV_ROWS, CONV_CH), F32) + b_ref[...]
        for res in range(SUBLANES):
            taps = [k for k in range(CONV_K) if (first + k) % SUBLANES == res]
            rows = CONV_ROWS + (SUBLANES if res else 0)
            part = None
            for k in taps:
                lo = base + first + k - res
                term = w_ref[k:k + 1, :] * buf_ref[lo:lo + rows, :]
                part = term if part is None else part + term
            acc = acc + part[res:res + CONV_ROWS]
        mu = jnp.mean(acc, axis=-1, keepdims=True)
        xc = acc - mu
        var = jnp.mean(xc * xc, axis=-1, keepdims=True)
        y = xc * lax.rsqrt(var + 1e-5) * lnw_ref[...] + lnb_ref[...]
        o_ref[base:base + CONV_ROWS, :] = (y * _sigmoid(y)).astype(o_ref.dtype)


def _conformer_conv(proj3, dw_w, dw_b, ln_w, ln_b, ts):
    b, s, _ = proj3.shape
    c = CONV_CH
    row = lambda a: a.reshape(1, c).astype(F32)
    return pl.pallas_call(
        functools.partial(_cconv_kernel, ts=ts),
        grid=(b, s // ts),
        in_specs=[pl.BlockSpec((None, ts, c), lambda bb, i: (bb, i, COL_U // c)),
                  pl.BlockSpec((None, ts, c), lambda bb, i: (bb, i, COL_U // c + 1)),
                  pl.BlockSpec((CONV_K, c), lambda bb, i: (0, 0)),
                  pl.BlockSpec((1, c), lambda bb, i: (0, 0)),
                  pl.BlockSpec((1, c), lambda bb, i: (0, 0)),
                  pl.BlockSpec((1, c), lambda bb, i: (0, 0))],
        out_specs=pl.BlockSpec((None, ts, c), lambda bb, i: (bb, i, 0)),
        out_shape=jax.ShapeDtypeStruct((b, s, c), BF16),
        scratch_shapes=[pltpu.VMEM((CONV_HALO + ts, c), F32)],
        compiler_params=_params("parallel", "arbitrary"),
        name="conformer_conv",
    )(proj3, proj3, dw_w.astype(F32), row(dw_b), row(ln_w), row(ln_b))


SSD_HALO = 16


def _ssd_kernel(z_ref, xbc_ref, dt_ref, cw_ref, cb_ref, dtb_ref, a_ref, dsk_ref, nw_ref, e_ref,
                o_ref, buf_ref, h_ref, *, chunk):
    L = chunk
    ci = pl.program_id(1)
    rows = range(z_ref.shape[0])

    @pl.when(ci == 0)
    def _():
        for bb in rows:
            buf_ref[bb, 0:SSD_HALO, :] = jnp.zeros((SSD_HALO, SSM_CONV_DIM), BF16)
        h_ref[...] = jnp.zeros(h_ref.shape, F32)

    @pl.when(ci > 0)
    def _():
        for bb in rows:
            buf_ref[bb, 0:SSD_HALO, :] = buf_ref[bb, L:L + SSD_HALO, :]

    for bb in rows:
        _ssd_chunk(z_ref.at[bb], xbc_ref.at[bb], dt_ref.at[bb], cw_ref, cb_ref, dtb_ref, a_ref, dsk_ref, nw_ref,
                   e_ref, o_ref.at[bb], buf_ref.at[bb], h_ref.at[bb], L)


def _ssd_chunk(z_ref, xbc_ref, dt_ref, cw_ref, cb_ref, dtb_ref, a_ref, dsk_ref, nw_ref, e_ref,
               o_ref, buf_ref, h_ref, L):
    gw = SSM_D_INNER // SSM_GROUPS
    hpg = SSM_HEADS // SSM_GROUPS

    buf_ref[SSD_HALO:SSD_HALO + L, :] = xbc_ref[...]
    xraw = buf_ref[...]
    r = lax.broadcasted_iota(jnp.int32, (L, SSD_HALO + L), 0)
    c = lax.broadcasted_iota(jnp.int32, (L, SSD_HALO + L), 1)
    xc = jnp.zeros((L, SSM_CONV_DIM), F32) + cb_ref[...]
    for k in range(SSM_CONV_K):
        shift = jnp.where(c == r + (SSD_HALO - (SSM_CONV_K - 1) + k), 1.0, 0.0).astype(BF16)
        xc = xc + cw_ref[k:k + 1, :] * jnp.dot(shift, xraw, preferred_element_type=F32)
    xc = xc * _sigmoid(xc)
    xs = xc[:, :SSM_D_INNER]
    bm = xc[:, SSM_D_INNER:SSM_D_INNER + SSM_GROUPS * SSM_STATE]
    cm = xc[:, SSM_D_INNER + SSM_GROUPS * SSM_STATE:]

    dtr = dt_ref[...] + dtb_ref[...]
    dt = jnp.maximum(dtr, 0.0) + jnp.log(1.0 + jnp.exp(-jnp.abs(dtr)))
    da = dt * a_ref[...]
    row = lax.broadcasted_iota(jnp.int32, (L, L), 0)
    col = lax.broadcasted_iota(jnp.int32, (L, L), 1)
    causal = col <= row
    a_cum = sum(jnp.dot(jnp.where(causal, 1.0, 0.0).astype(BF16), part, preferred_element_type=F32)
                for part in _split3(da))
    a_cum_t = a_cum.T

    expand = e_ref[...]
    per_head = jnp.concatenate([dt, jnp.exp(a_cum), jnp.exp(a_cum[L - 1:L, :] - a_cum)], axis=0)
    full = sum(jnp.dot(part, expand, preferred_element_type=F32) for part in _split3(per_head))
    dt_full, ea_full, dend_full = full[:L], full[L:2 * L], full[2 * L:]
    cdec_full = ea_full[L - 1:L, :]

    xdt = xs * dt_full
    xdt_b = xdt.astype(BF16)
    xw_b = (xdt * dend_full).astype(BF16)

    y_parts = []
    for g in range(SSM_GROUPS):
        cg = cm[:, g * SSM_STATE:(g + 1) * SSM_STATE].astype(BF16)
        bg_t = bm[:, g * SSM_STATE:(g + 1) * SSM_STATE].T.astype(BF16)
        cbm = jnp.dot(cg, bg_t, preferred_element_type=F32)
        h_in = h_ref[:, g * gw:(g + 1) * gw]
        y_off = jnp.dot(cg, h_in.astype(BF16), preferred_element_type=F32)
        states = jnp.dot(bg_t, xw_b[:, g * gw:(g + 1) * gw], preferred_element_type=F32)
        h_ref[:, g * gw:(g + 1) * gw] = h_in * cdec_full[:, g * gw:(g + 1) * gw] + states
        yd = []
        for r in range(hpg):
            hh = g * hpg + r
            seg = a_cum[:, hh:hh + 1] - a_cum_t[hh:hh + 1, :]
            dec = jnp.exp(jnp.where(causal, seg, -jnp.inf))
            yd.append(jnp.dot((cbm * dec).astype(BF16), xdt_b[:, hh * SSM_HEADDIM:(hh + 1) * SSM_HEADDIM],
                              preferred_element_type=F32))
        y_g = (jnp.concatenate(yd, axis=-1) + y_off * ea_full[:, g * gw:(g + 1) * gw]
               + xs[:, g * gw:(g + 1) * gw] * dsk_ref[:, g * gw:(g + 1) * gw])
        zg = z_ref[:, g * gw:(g + 1) * gw].astype(F32)
        y_parts.append(_rms(y_g * (zg * _sigmoid(zg)), 1e-5))
    o_ref[...] = (jnp.concatenate(y_parts, axis=-1) * nw_ref[...]).astype(o_ref.dtype)


def _mamba2_ssd(proj3, dt3, conv_w, conv_b, dt_bias, a_log, d_skip, norm_w):
    b, s, _ = proj3.shape
    L = SSM_CHUNK
    di, cd = SSM_D_INNER, SSM_CONV_DIM
    pad_h = lambda a: jnp.pad(a.astype(F32), (0, LANES - SSM_HEADS)).reshape(1, LANES)
    head_of_channel = jnp.arange(di, dtype=jnp.int32) // SSM_HEADDIM
    expand = (jnp.arange(LANES, dtype=jnp.int32)[:, None] == head_of_channel[None, :]).astype(BF16)
    const = lambda shape: pl.BlockSpec(shape, lambda bb, c: (0, 0))
    nb = next(n for n in (8, 4, 2, 1) if b % n == 0)
    return pl.pallas_call(
        functools.partial(_ssd_kernel, chunk=L),
        grid=(b // nb, s // L),
        in_specs=[pl.BlockSpec((nb, L, di), lambda bb, c: (bb, c, COL_Z // di)),
                  pl.BlockSpec((nb, L, cd), lambda bb, c: (bb, c, COL_XBC // cd)),
                  pl.BlockSpec((nb, L, LANES), lambda bb, c: (bb, c, 0)),
                  const((SSM_CONV_K, cd)), const((1, cd)), const((1, LANES)), const((1, LANES)),
                  const((1, di)), const((1, di)), const((LANES, di))],
        out_specs=pl.BlockSpec((nb, L, di), lambda bb, c: (bb, c, 0)),
        out_shape=jax.ShapeDtypeStruct((b, s, di), BF16),
        scratch_shapes=[pltpu.VMEM((nb, SSD_HALO + L, cd), BF16), pltpu.VMEM((nb, SSM_STATE, di), F32)],
        compiler_params=_params("parallel", "arbitrary"),
        name="mamba2_ssd",
    )(proj3, proj3, dt3, conv_w.astype(F32), conv_b.reshape(1, cd).astype(F32), pad_h(dt_bias),
      pad_h(-jnp.exp(a_log.astype(F32))), jnp.repeat(d_skip.astype(F32), SSM_HEADDIM).reshape(1, di),
      norm_w.reshape(1, di).astype(F32), expand)


def _merge_kernel(x_ref, ya_ref, yc_ref, ys_ref, ga_ref, gc_ref, gs_ref, wa_ref, wc_ref, ws_ref, wm_ref, o_ref):
    def branch(y_ref, g_ref, w_ref):
        return _sigmoid(g_ref[...].astype(F32)) * jnp.dot(y_ref[...], w_ref[...], preferred_element_type=F32)

    merged = branch(ya_ref, ga_ref, wa_ref) + branch(yc_ref, gc_ref, wc_ref) + branch(ys_ref, gs_ref, ws_ref)
    o_ref[...] = x_ref[...] + jnp.dot(merged.astype(BF16), wm_ref[...], preferred_element_type=F32)


def _merge(x, ya, yc, ys, proj, wa, wc, ws, wm, tm):
    t, d = x.shape
    w = ya.shape[1]
    rows = lambda width, cb: pl.BlockSpec((tm, width), lambda i: (i, cb))
    const = lambda shape: pl.BlockSpec(shape, lambda i: (0, 0))
    return pl.pallas_call(
        _merge_kernel,
        grid=(t // tm,),
        in_specs=[rows(d, 0), rows(w, 0), rows(w, 0), rows(w, 0),
                  rows(d, COL_G // d), rows(d, COL_G // d + 1), rows(d, COL_G // d + 2),
                  const((w, d)), const((w, d)), const((w, d)), const((d, d))],
        out_specs=rows(d, 0),
        out_shape=jax.ShapeDtypeStruct((t, d), F32),
        compiler_params=_params("parallel"),
        name="gated_merge",
    )(x, ya, yc, ys, proj, proj, proj, wa, wc, ws, wm)


def _ffn_kernel(x_ref, g_ref, wg_ref, wu_ref, wd_ref, o_ref):
    x = x_ref[...]
    h = (_rms(x, NORM_EPS) * g_ref[...]).astype(BF16)
    a = jnp.dot(h, wg_ref[...], preferred_element_type=F32)
    u = jnp.dot(h, wu_ref[...], preferred_element_type=F32)
    o_ref[...] = x + jnp.dot((a * _sigmoid(a) * u).astype(BF16), wd_ref[...], preferred_element_type=F32)


def _ffn_dense(x, g, wg, wu, wd, tm):
    t, d = x.shape
    f = wg.shape[1]
    resident = lambda shape: pl.BlockSpec(shape, lambda i: (0, 0), pipeline_mode=pl.Buffered(1))
    return pl.pallas_call(
        _ffn_kernel,
        grid=(t // tm,),
        in_specs=[pl.BlockSpec((tm, d), lambda i: (i, 0)),
                  pl.BlockSpec((1, d), lambda i: (0, 0)),
                  resident((d, f)), resident((d, f)), resident((f, d))],
        out_specs=pl.BlockSpec((tm, d), lambda i: (i, 0)),
        out_shape=jax.ShapeDtypeStruct((t, d), F32),
        compiler_params=_params("parallel"),
        name="ffn_dense",
    )(x, g.reshape(1, d), wg, wu, wd)


def _router_kernel(x_ref, g_ref, wr_ref, br_ref, h_ref, idx_ref, wt_ref, rank_ref, cnt_ref, tri_ref, base_ref):
    i = pl.program_id(0)
    tm = x_ref.shape[0]

    @pl.when(i == 0)
    def _():
        r = lax.broadcasted_iota(jnp.int32, (tm, tm), 0)
        c = lax.broadcasted_iota(jnp.int32, (tm, tm), 1)
        tri_ref[...] = jnp.where(c < r, 1.0, 0.0).astype(BF16)
        base_ref[...] = jnp.zeros(base_ref.shape, F32)

    h = _rms(x_ref[...], NORM_EPS) * g_ref[...]
    h_ref[...] = _pack_halves(h)
    logits = jnp.dot(h, wr_ref[...], precision=HIGHEST, preferred_element_type=F32) + br_ref[...]
    lane = lax.broadcasted_iota(jnp.int32, logits.shape, 1)
    logits = jnp.where(lane < N_EXPERTS, logits, -jnp.inf)
    m1 = jnp.max(logits, axis=-1, keepdims=True)
    i1 = jnp.min(jnp.where(logits == m1, lane, LANES), axis=-1, keepdims=True)
    rest = jnp.where(lane == i1, -jnp.inf, logits)
    m2 = jnp.max(rest, axis=-1, keepdims=True)
    i2 = jnp.min(jnp.where(rest == m2, lane, LANES), axis=-1, keepdims=True)
    e = jnp.exp(m2 - m1)
    w1 = 1.0 / (1.0 + e)
    idx_ref[...] = jnp.where(lane == 0, i1, i2)[:, :TOP_K]
    wt_ref[...] = jnp.where(lane == 0, w1, e * w1)[:, :TOP_K]

    chosen = jnp.logical_or(lane == i1, lane == i2)
    before = jnp.dot(tri_ref[...], jnp.where(chosen, 1.0, 0.0).astype(BF16), preferred_element_type=F32)
    before = before + base_ref[...]
    r1 = jnp.sum(jnp.where(lane == i1, before, 0.0), axis=-1, keepdims=True)
    r2 = jnp.sum(jnp.where(lane == i2, before, 0.0), axis=-1, keepdims=True)
    rank_ref[...] = jnp.where(lane == 0, r1, r2)[:, :TOP_K].astype(jnp.int32)
    base_ref[...] += jnp.sum(jnp.where(chosen, 1.0, 0.0), axis=0, keepdims=True)
    cnt_ref[...] = base_ref[...].astype(jnp.int32)


def _router(x, g, w_router, b_router, tm):
    t, d = x.shape
    wr = jnp.pad(w_router.astype(F32), ((0, 0), (0, LANES - N_EXPERTS)))
    br = jnp.pad(b_router.astype(F32), (0, LANES - N_EXPERTS)).reshape(1, LANES)
    pair = pl.BlockSpec((tm, TOP_K), lambda i: (i, 0))
    return pl.pallas_call(
        _router_kernel,
        grid=(t // tm,),
        in_specs=[pl.BlockSpec((tm, d), lambda i: (i, 0)),
                  pl.BlockSpec((1, d), lambda i: (0, 0)),
                  pl.BlockSpec((d, LANES), lambda i: (0, 0)),
                  pl.BlockSpec((1, LANES), lambda i: (0, 0))],
        out_specs=[pl.BlockSpec((tm, d // 2), lambda i: (i, 0)), pair, pair, pair,
                   pl.BlockSpec((1, LANES), lambda i: (0, 0))],
        out_shape=[jax.ShapeDtypeStruct((t, d // 2), jnp.uint32),
                   jax.ShapeDtypeStruct((t, TOP_K), jnp.int32),
                   jax.ShapeDtypeStruct((t, TOP_K), F32),
                   jax.ShapeDtypeStruct((t, TOP_K), jnp.int32),
                   jax.ShapeDtypeStruct((1, LANES), jnp.int32)],
        scratch_shapes=[pltpu.VMEM((tm, tm), BF16), pltpu.VMEM((1, LANES), F32)],
        compiler_params=_params("arbitrary"),
        name="moe_router",
    )(x, g.reshape(1, d), wr, br)


def _expert_kernel(be_ref, nv_ref, x_ref, wg_ref, wu_ref, wd_ref, o_ref, xb_ref, acc_ref, *, nj):
    i = pl.program_id(0)
    j = pl.program_id(1)
    last = nj - 1
    used = nv_ref[i] > 0

    def partial_ffn():
        x = xb_ref[...]
        a = jnp.dot(x, wg_ref[...], preferred_element_type=F32)
        u = jnp.dot(x, wu_ref[...], preferred_element_type=F32)
        return jnp.dot((a * _sigmoid(a) * u).astype(BF16), wd_ref[...], preferred_element_type=F32)

    @pl.when(jnp.logical_and(used, j == 0))
    def _():
        x = _unpack_halves(x_ref[...])
        row = lax.broadcasted_iota(jnp.int32, x.shape, 0)
        xb_ref[...] = jnp.where(row < nv_ref[i], x, 0.0).astype(BF16)
        if nj == 1:
            o_ref[...] = _pack_halves(partial_ffn())
        else:
            acc_ref[...] = partial_ffn()

    if nj > 2:
        @pl.when(jnp.logical_and(used, jnp.logical_and(j > 0, j < last)))
        def _():
            acc_ref[...] += partial_ffn()

    if nj > 1:
        @pl.when(jnp.logical_and(used, j == last))
        def _():
            o_ref[...] = _pack_halves(acc_ref[...] + partial_ffn())

    @pl.when(jnp.logical_and(jnp.logical_not(used), j == last))
    def _():
        o_ref[...] = jnp.zeros(o_ref.shape, o_ref.dtype)


def _expert_ffn(xs, block_exp, n_valid, wg, wu, wd, bm, tf):
    n_rows, half = xs.shape
    d = 2 * half
    f = wg.shape[2]
    nj = f // tf

    def jj(i, j, nv):
        return jnp.where(nv[i] > 0, j, nj - 1)

    grid_spec = pltpu.PrefetchScalarGridSpec(
        num_scalar_prefetch=2,
        grid=(n_rows // bm, nj),
        in_specs=[pl.BlockSpec((bm, half), lambda i, j, be, nu: (i, 0)),
                  pl.BlockSpec((None, d, tf), lambda i, j, be, nu: (be[i], 0, jj(i, j, nu))),
                  pl.BlockSpec((None, d, tf), lambda i, j, be, nu: (be[i], 0, jj(i, j, nu))),
                  pl.BlockSpec((None, tf, d), lambda i, j, be, nu: (be[i], jj(i, j, nu), 0))],
        out_specs=pl.BlockSpec((bm, half), lambda i, j, be, nu: (i, 0)),
        scratch_shapes=[pltpu.VMEM((bm, d), BF16), pltpu.VMEM((bm, d), F32)],
    )
    return pl.pallas_call(
        functools.partial(_expert_kernel, nj=nj),
        grid_spec=grid_spec,
        out_shape=jax.ShapeDtypeStruct((n_rows, half), jnp.uint32),
        compiler_params=_params("parallel", "arbitrary"),
        name="moe_experts",
    )(block_exp, n_valid, xs, wg, wu, wd)


def _route(top_idx, rank, counts, bm):
    t = top_idx.shape[0]
    padded = (counts + bm - 1) // bm * bm
    end_padded = jnp.cumsum(padded)
    start_padded = end_padded - padded
    experts = jnp.arange(N_EXPERTS, dtype=jnp.int32)
    dest = rank + jnp.sum(jnp.where(top_idx[..., None] == experts, start_padded, 0), axis=-1)
    n_blocks = t * TOP_K // bm + N_EXPERTS
    block_start = jnp.arange(n_blocks, dtype=jnp.int32) * bm
    block_exp = jnp.minimum(jnp.sum(block_start[:, None] >= end_padded[None, :], axis=1), N_EXPERTS - 1)
    n_valid = jnp.clip((start_padded + counts)[block_exp] - block_start, 0, bm)
    return dest, block_exp.astype(jnp.int32), n_valid.astype(jnp.int32)


def _combine_kernel(x_ref, a_ref, b_ref, w_ref, g_ref, o_ref, *, final_norm):
    w = w_ref[...]
    y = x_ref[...] + w[:, 0:1] * _unpack_halves(a_ref[...]) + w[:, 1:2] * _unpack_halves(b_ref[...])
    o_ref[...] = _rms(y, NORM_EPS) * g_ref[...] if final_norm else y


def _combine(x, ab, w, g, tm):
    t, d = x.shape
    spec = pl.BlockSpec((tm, d), lambda i: (i, 0))
    final_norm = g is not None
    gain = g.reshape(1, d) if final_norm else jnp.ones((1, d), F32)
    return pl.pallas_call(
        functools.partial(_combine_kernel, final_norm=final_norm), grid=(t // tm,),
        in_specs=[spec, pl.BlockSpec((tm, d // 2), lambda i: (i, 0)),
                  pl.BlockSpec((tm, d // 2), lambda i: (i + t // tm, 0)),
                  pl.BlockSpec((tm, TOP_K), lambda i: (i, 0)),
                  pl.BlockSpec((1, d), lambda i: (0, 0))],
        out_specs=spec,
        out_shape=jax.ShapeDtypeStruct((t, d), F32), compiler_params=_params("parallel"),
        name="moe_combine",
    )(x, ab, ab, w, gain)


def _moe(x, g, w_router, b_router, wg, wu, wd, final_g, tm, bm, tf):
    h, top_idx, top_w, rank, counts = _router(x, g, w_router, b_router, tm)
    dest, block_exp, n_valid = _route(top_idx, rank, counts[0, :N_EXPERTS], bm)
    dest = dest.T
    xs = _scatter_rows(h, dest, block_exp.shape[0] * bm)
    ys = _expert_ffn(xs, block_exp, n_valid, wg, wu, wd, bm, tf)
    picked = _gather_rows(ys, dest.reshape(-1))
    return _combine(x, picked, top_w, final_g, tm)


def _cast_kernel(w_ref, o_ref):
    o_ref[...] = w_ref[...].astype(o_ref.dtype)


def _to_bf16(w, axis, tile):
    e, r, c = w.shape
    block = (None, tile, c) if axis == 0 else (None, r, tile)
    index = (lambda k, j: (k, j, 0)) if axis == 0 else (lambda k, j: (k, 0, j))
    return pl.pallas_call(
        _cast_kernel, grid=(e, (r, c)[axis] // tile),
        in_specs=[pl.BlockSpec(block, index)], out_specs=pl.BlockSpec(block, index),
        out_shape=jax.ShapeDtypeStruct(w.shape, BF16), compiler_params=_params("parallel", "parallel"),
        name="expert_weight_cast",
    )(w)


def _final_norm_kernel(x_ref, g_ref, o_ref):
    o_ref[...] = _rms(x_ref[...], NORM_EPS) * g_ref[...]


def _final_norm(x, g, tm):
    t, d = x.shape
    return pl.pallas_call(
        _final_norm_kernel, grid=(t // tm,),
        in_specs=[pl.BlockSpec((tm, d), lambda i: (i, 0)), pl.BlockSpec((1, d), lambda i: (0, 0))],
        out_specs=pl.BlockSpec((tm, d), lambda i: (i, 0)),
        out_shape=jax.ShapeDtypeStruct((t, d), F32), compiler_params=_params("parallel"),
        name="final_norm",
    )(x, g.reshape(1, d))


def _tiles(t, s):
    pick = lambda n, pref: pref if n % pref == 0 else n
    return dict(tm_proj=pick(t, 512), tn_proj=PROJ_COLS // 4, tb_attn=pick(s, 512), ts_conv=pick(s, 512),
                tm_merge=pick(t, 512), tm_ffn=pick(t, 512), tm_row=pick(t, 1024), bm_moe=pick(t, 512))


def kernel(x, ln_mix_w, w_in, attn_lq1, attn_lk1, attn_lq2, attn_lk2, attn_subln_w, attn_w_o, conv_dw_w, conv_dw_b, conv_ln_w, conv_ln_b, conv_w_pw, ssm_conv_w, ssm_conv_b, ssm_dt_bias, ssm_A_log, ssm_D, ssm_norm_w, ssm_w_out, w_mix_out, ln_ffn_w, ffn_w_gate, ffn_w_up, ffn_w_down, moe_w_router, moe_b_router, moe_w_gate, moe_w_up, moe_w_down, ln_final_w):
    b, s, d = x.shape
    t = b * s
    depth = w_in.shape[0]
    tl = _tiles(t, s)
    xf = x.reshape(t, d)
    for l in range(depth):
        w_main = jnp.concatenate([w_in[l][:, :COL_DT], w_in[l][:, COL_DT + SSM_HEADS:]], axis=1).astype(BF16)
        w_dt = jnp.pad(w_in[l][:, COL_DT:COL_DT + SSM_HEADS], ((0, 0), (0, LANES - SSM_HEADS))).astype(BF16)
        proj, dt = _norm_matmul(xf, ln_mix_w[l], w_main, w_dt, tl["tm_proj"], tl["tn_proj"])
        proj3 = proj.reshape(b, s, PROJ_COLS)

        lam_init = 0.8 - 0.6 * math.exp(-0.3 * l)
        lam = (jnp.exp(jnp.sum(attn_lq1[l].astype(F32) * attn_lk1[l].astype(F32)))
               - jnp.exp(jnp.sum(attn_lq2[l].astype(F32) * attn_lk2[l].astype(F32))) + lam_init)
        y_att = _diff_attention(proj3, lam, attn_subln_w[l], lam_init, tl["tb_attn"])
        y_conv = _conformer_conv(proj3, conv_dw_w[l], conv_dw_b[l], conv_ln_w[l], conv_ln_b[l], tl["ts_conv"])
        y_ssm = _mamba2_ssd(proj3, dt.reshape(b, s, LANES), ssm_conv_w[l], ssm_conv_b[l], ssm_dt_bias[l],
                            ssm_A_log[l], ssm_D[l], ssm_norm_w[l])
        xf = _merge(xf, y_att.reshape(t, ATT_WIDTH), y_conv.reshape(t, CONV_CH), y_ssm.reshape(t, SSM_D_INNER),
                    proj, attn_w_o[l].astype(BF16), conv_w_pw[l].astype(BF16), ssm_w_out[l].astype(BF16),
                    w_mix_out[l].astype(BF16), tl["tm_merge"])

        i = l // 2
        if l % 2 == 0:
            xf = _ffn_dense(xf, ln_ffn_w[l], ffn_w_gate[i].astype(BF16), ffn_w_up[i].astype(BF16),
                            ffn_w_down[i].astype(BF16), tl["tm_ffn"])
        else:
            f = moe_w_gate.shape[3]
            tf = f // 2 if (f // 2) % (2 * LANES) == 0 else f
            final_g = ln_final_w if l == depth - 1 else None
            xf = _moe(xf, ln_ffn_w[l], moe_w_router[i], moe_b_router[i], _to_bf16(moe_w_gate[i], 1, tf),
                      _to_bf16(moe_w_up[i], 1, tf), _to_bf16(moe_w_down[i], 0, tf), final_g,
                      tl["tm_row"], tl["bm_moe"], tf)
    if depth % 2 == 1:
        xf = _final_norm(xf, ln_final_w, tl["tm_row"])
    return xf.reshape(b, s, d)
```
